```python
import math
import jax
import jax.numpy as jnp
from jax import lax
import numpy as np

D_MODEL = 1024
BATCH = 8
SEQ = 4096
DEPTH = 2

GRID_W = 64
CTX_LEN = 256
ATTN_BLOCK = 128
ROPE_BASE = 10000.0
RMS_EPS = 1e-6

DIFF_HEADS = 4
DIFF_HEAD_DIM = 64
NA_HEADS = 8
NA_HEAD_DIM = 64
NA_WIN_ROWS_MAX = 8
NA_WIN_COLS = 16
MLA_HEADS = 8
MLA_Q_RANK = 256
MLA_KV_RANK = 128
MLA_NOPE_DIM = 64
MLA_ROPE_DIM = 32
MLA_V_DIM = 64
RET_HEADS = 4
RET_QK_DIM = 64
RET_V_DIM = 128
RET_CHUNK = 128
N_EXPERTS = 16
EC_CAPACITY_FACTOR = 2
EXPERT_FF = 1024

DIFF_WIDTH = DIFF_HEADS * 2 * DIFF_HEAD_DIM
NA_WIDTH = NA_HEADS * NA_HEAD_DIM
EVEN_IN = 3 * DIFF_WIDTH + 3 * NA_WIDTH
EVEN_OUT = DIFF_WIDTH + NA_WIDTH
MLA_QK_DIM = MLA_NOPE_DIM + MLA_ROPE_DIM
MLA_WIDTH = MLA_HEADS * MLA_V_DIM
RET_QK_WIDTH = RET_HEADS * RET_QK_DIM
RET_V_WIDTH = RET_HEADS * RET_V_DIM
ODD_IN = MLA_Q_RANK + MLA_KV_RANK + MLA_ROPE_DIM + 2 * RET_QK_WIDTH + 2 * RET_V_WIDTH
ODD_OUT = MLA_WIDTH + RET_V_WIDTH
N_EVEN = (DEPTH + 1) // 2
N_ODD = DEPTH // 2

kernel_name = 'hybrid_diffusion_diffattn_natten_mla_retention_ecmoe'


def rms_norm(x, gain=None):
    xf = x.astype(jnp.float32)
    y = xf * lax.rsqrt(jnp.mean(xf * xf, axis=-1, keepdims=True) + RMS_EPS)
    if gain is not None:
        y = y * gain.astype(jnp.float32)
    return y.astype(x.dtype)


def split_cols(p, widths):
    idx, acc = [], 0
    for w in widths[:-1]:
        acc += w
        idx.append(acc)
    return jnp.split(p, idx, axis=-1)


def axial_angles(n_tokens, rot_dim):
    t = jnp.arange(n_tokens)
    rows = (t // GRID_W).astype(jnp.float32)
    cols = (t % GRID_W).astype(jnp.float32)
    m = rot_dim // 4
    freqs = ROPE_BASE ** (-jnp.arange(m, dtype=jnp.float32) / m)
    return rows[:, None] * freqs, cols[:, None] * freqs


def rope_half(x, ang):
    a, b = jnp.split(x, 2, axis=-1)
    cos = jnp.cos(ang).astype(x.dtype)
    sin = jnp.sin(ang).astype(x.dtype)
    return jnp.concatenate([a * cos - b * sin, b * cos + a * sin], axis=-1)


def apply_axial_rope(x, ang_r, ang_c):
    shape = ang_r.shape[:1] + (1,) * (x.ndim - 3) + ang_r.shape[1:]
    xr, xc = jnp.split(x, 2, axis=-1)
    return jnp.concatenate([rope_half(xr, ang_r.reshape(shape)), rope_half(xc, ang_c.reshape(shape))], axis=-1)


def sweep_query_blocks(fn, queries):
    B, L = queries[0].shape[:2]
    nb = L // ATTN_BLOCK
    blocks = tuple(jnp.moveaxis(q.reshape((B, nb, ATTN_BLOCK) + q.shape[2:]), 1, 0) for q in queries)
    out = jnp.moveaxis(lax.map(fn, blocks), 0, 1)
    return out.reshape((B, L) + out.shape[3:])


def softmax_attention(q, k, v, q_rope=None, k_rope=None):
    d = q.shape[-1] + (0 if q_rope is None else q_rope.shape[-1])
    scale = d ** -0.5

    def block(qs):
        s = jnp.einsum('bqhd,bkhd->bhqk', qs[0], k).astype(jnp.float32)
        if q_rope is not None:
            s = s + jnp.einsum('bqhr,bkr->bhqk', qs[1], k_rope).astype(jnp.float32)
        p = jax.nn.softmax(s * scale, axis=-1).astype(v.dtype)
        return jnp.einsum('bhqk,bkhe->bqhe', p, v)

    queries = (q,) if q_rope is None else (q, q_rope)
    return sweep_query_blocks(block, queries)


def diff_attention(q, k, v, lam):
    scale = DIFF_HEAD_DIM ** -0.5

    def block(qs):
        s = jnp.einsum('bqhcd,bkhcd->bhcqk', qs[0], k).astype(jnp.float32) * scale
        p = jax.nn.softmax(s, axis=-1)
        a = p[:, :, 0] - lam * p[:, :, 1]
        return jnp.einsum('bhqk,bkhe->bqhe', a.astype(v.dtype), v)

    return sweep_query_blocks(block, (q,))


def neighbourhood_attention(q, k, v, k_ctx, v_ctx, rpb):
    B, L, H, d = q.shape
    rows = L // GRID_W
    wr = min(NA_WIN_ROWS_MAX, rows)
    wc = NA_WIN_COLS
    scale = d ** -0.5
    qg = q.reshape(B, rows, GRID_W, H, d)
    kg = k.reshape(B, rows, GRID_W, H, d)
    vg = v.reshape(B, rows, GRID_W, H, d)
    cols = jnp.arange(GRID_W)
    col_start = jnp.clip(cols - wc // 2, 0, GRID_W - wc)
    col_idx = col_start[:, None] + jnp.arange(wc)[None, :]
    col_bias = rpb[:, :, col_idx - cols[:, None] + (NA_WIN_COLS - 1)]

    def row_block(r):
        rs = jnp.clip(r - wr // 2, 0, rows - wr)
        q_row = lax.dynamic_index_in_dim(qg, r, axis=1, keepdims=False)
        k_win = lax.dynamic_slice_in_dim(kg, rs, wr, axis=1)[:, :, col_idx]
        v_win = lax.dynamic_slice_in_dim(vg, rs, wr, axis=1)[:, :, col_idx]
        row_off = rs + jnp.arange(wr) - r + (NA_WIN_ROWS_MAX - 1)
        bias = jnp.transpose(jnp.take(col_bias, row_off, axis=1), (0, 2, 1, 3))
        s_loc = jnp.einsum('bqhd,brqwhd->bhqrw', q_row, k_win).astype(jnp.float32) * scale + bias[None].astype(jnp.float32)
        s_ctx = jnp.einsum('bqhd,bkhd->bhqk', q_row, k_ctx).astype(jnp.float32) * scale
        s = jnp.concatenate([s_loc.reshape(B, H, GRID_W, wr * wc), s_ctx], axis=-1)
        p = jax.nn.softmax(s, axis=-1).astype(v.dtype)
        p_loc = p[..., :wr * wc].reshape(B, H, GRID_W, wr, wc)
        p_ctx = p[..., wr * wc:]
        return jnp.einsum('bhqrw,brqwhd->bqhd', p_loc, v_win) + jnp.einsum('bhqk,bkhd->bqhd', p_ctx, v_ctx)

    out = lax.map(row_block, jnp.arange(rows))
    return jnp.moveaxis(out, 0, 1).reshape(B, L, H, d)


def retention_chunkwise(q, k, v, log_gamma, s0, include_diag):
    B, H, L, _ = q.shape
    dv = v.shape[-1]
    C = RET_CHUNK
    n = L // C
    pos = jnp.arange(C, dtype=jnp.float32)
    rel = pos[:, None] - pos[None, :]
    mask = (rel >= 0) if include_diag else (rel > 0)
    decay_in = jnp.where(mask, jnp.exp(log_gamma[:, None, None] * jnp.where(mask, rel, 0.0)), 0.0)
    decay_q = jnp.exp(log_gamma[:, None] * (pos + 1.0))[..., None]
    decay_k = jnp.exp(log_gamma[:, None] * (C - 1.0 - pos))[..., None]
    decay_chunk = jnp.exp(log_gamma * C)[:, None, None]

    def to_chunks(t):
        return jnp.moveaxis(t.reshape(B, H, n, C, t.shape[-1]), 2, 0)

    def step(state, blk):
        qb, kb, vb = blk
        inner = jnp.einsum('bhij,bhjv->bhiv', jnp.einsum('bhid,bhjd->bhij', qb, kb) * decay_in, vb)
        cross = jnp.einsum('bhid,bhdv->bhiv', qb * decay_q, state)
        new_state = decay_chunk * state + jnp.einsum('bhjd,bhjv->bhdv', kb * decay_k, vb)
        return new_state, inner + cross

    _, out = lax.scan(step, s0, (to_chunks(q), to_chunks(k), to_chunks(v)))
    return jnp.moveaxis(out, 0, 2).reshape(B, H, L, dv)


def retention_final_state(k, v, log_gamma):
    L = k.shape[2]
    w = jnp.exp(log_gamma[:, None] * (L - 1.0 - jnp.arange(L, dtype=jnp.float32)))
    return jnp.einsum('hl,bhld,bhlv->bhdv', w, k, v)


def retention_bidir(q, k, v, q_c, k_c, v_c, decay_logit, need_ctx):
    def to_bhld(t):
        return jnp.moveaxis(t, 2, 1).astype(jnp.float32)

    def flip(t):
        return jnp.flip(t, axis=2)

    kscale = RET_QK_DIM ** -0.5
    q, v, q_c, v_c = to_bhld(q), to_bhld(v), to_bhld(q_c), to_bhld(v_c)
    k, k_c = to_bhld(k) * kscale, to_bhld(k_c) * kscale
    log_g = jax.nn.log_sigmoid(decay_logit.astype(jnp.float32))
    lg_f, lg_b = log_g[0], log_g[1]
    s_f = retention_final_state(k_c, v_c, lg_f)
    s_b = retention_final_state(flip(k_c), flip(v_c), lg_b)
    out = retention_chunkwise(q, k, v, lg_f, s_f, True) + flip(retention_chunkwise(flip(q), flip(k), flip(v), lg_b, s_b, False))
    out = jnp.moveaxis(out, 1, 2)
    if not need_ctx:
        return out, None
    zero = jnp.zeros_like(s_f)
    out_c = retention_chunkwise(q_c, k_c, v_c, lg_f, zero, True) + flip(retention_chunkwise(flip(q_c), flip(k_c), flip(v_c), lg_b, zero, False))
    return out, jnp.moveaxis(out_c, 1, 2)


def even_layer(h, hc, w_in, w_out, lam_params, subln, rpb, layer_idx, need_ctx):
    B, L, _ = h.shape
    Lc = hc.shape[1]
    widths = (DIFF_WIDTH,) * 3 + (NA_WIDTH,) * 3
    aq, ak, av, nq, nk, nv = split_cols(h @ w_in, widths)
    aqc, akc, avc, nqc, nkc, nvc = split_cols(hc @ w_in, widths)

    def dh(t):
        return t.reshape(t.shape[:2] + (DIFF_HEADS, 2, DIFF_HEAD_DIM))

    def dvh(t):
        return t.reshape(t.shape[:2] + (DIFF_HEADS, 2 * DIFF_HEAD_DIM))

    def nh(t):
        return t.reshape(t.shape[:2] + (NA_HEADS, NA_HEAD_DIM))

    ang_r, ang_c = axial_angles(L, DIFF_HEAD_DIM)
    lam_init = 0.8 - 0.6 * math.exp(-0.3 * layer_idx)
    lq1, lk1, lq2, lk2 = lam_params
    lam = jnp.exp(jnp.sum(lq1 * lk1)) - jnp.exp(jnp.sum(lq2 * lk2)) + lam_init

    k_all = jnp.concatenate([apply_axial_rope(dh(ak), ang_r, ang_c), dh(akc)], axis=1)
    v_all = jnp.concatenate([dvh(av), dvh(avc)], axis=1)
    a = diff_attention(apply_axial_rope(dh(aq), ang_r, ang_c), k_all, v_all, lam)
    a = (rms_norm(a, subln) * (1.0 - lam_init)).reshape(B, L, DIFF_WIDTH)
    nb = neighbourhood_attention(nh(nq), nh(nk), nh(nv), nh(nkc), nh(nvc), rpb).reshape(B, L, NA_WIDTH)
    y = jnp.concatenate([a, nb], axis=-1) @ w_out
    if not need_ctx:
        return y, None
    ac = diff_attention(dh(aqc), dh(akc), dvh(avc), lam)
    ac = (rms_norm(ac, subln) * (1.0 - lam_init)).reshape(B, Lc, DIFF_WIDTH)
    nc = softmax_attention(nh(nqc), nh(nkc), nh(nvc)).reshape(B, Lc, NA_WIDTH)
    yc = jnp.concatenate([ac, nc], axis=-1) @ w_out
    return y, yc


def odd_layer(h, hc, w_in, w_out, q_norm, w_uq, kv_norm, w_ukv, decay_logit, need_ctx):
    B, L, _ = h.shape
    Lc = hc.shape[1]
    widths = (MLA_Q_RANK, MLA_KV_RANK, MLA_ROPE_DIM, RET_QK_WIDTH, RET_QK_WIDTH, RET_V_WIDTH, RET_V_WIDTH)
    cq, ckv, kr, rq, rk, rv, rg = split_cols(h @ w_in, widths)
    cqc, ckvc, krc, rqc, rkc, rvc, rgc = split_cols(hc @ w_in, widths)

    def mla_up(cq_, ckv_):
        n = cq_.shape[:2]
        qh = (rms_norm(cq_, q_norm) @ w_uq).reshape(n + (MLA_HEADS, MLA_QK_DIM))
        kvh = (rms_norm(ckv_, kv_norm) @ w_ukv).reshape(n + (MLA_HEADS, MLA_NOPE_DIM + MLA_V_DIM))
        return qh[..., :MLA_NOPE_DIM], qh[..., MLA_NOPE_DIM:], kvh[..., :MLA_NOPE_DIM], kvh[..., MLA_NOPE_DIM:]

    def rh(t, d):
        return t.reshape(t.shape[:2] + (RET_HEADS, d))

    qn, qr, kn, vm = mla_up(cq, ckv)
    qnc, qrc, knc, vmc = mla_up(cqc, ckvc)
    ang_r, ang_c = axial_angles(L, MLA_ROPE_DIM)
    k_rope_all = jnp.concatenate([apply_axial_rope(kr, ang_r, ang_c), krc], axis=1)
    m = softmax_attention(qn, jnp.concatenate([kn, knc], axis=1), jnp.concatenate([vm, vmc], axis=1),
                          apply_axial_rope(qr, ang_r, ang_c), k_rope_all).reshape(B, L, MLA_WIDTH)
    r, r_c = retention_bidir(rh(rq, RET_QK_DIM), rh(rk, RET_QK_DIM), rh(rv, RET_V_DIM),
                             rh(rqc, RET_QK_DIM), rh(rkc, RET_QK_DIM), rh(rvc, RET_V_DIM), decay_logit, need_ctx)
    r = rms_norm(r.astype(h.dtype)).reshape(B, L, RET_V_WIDTH) * jax.nn.silu(rg)
    y = jnp.concatenate([m, r], axis=-1) @ w_out
    if not need_ctx:
        return y, None
    mc = softmax_attention(qnc, knc, vmc, qrc, krc).reshape(B, Lc, MLA_WIDTH)
    rc = rms_norm(r_c.astype(hc.dtype)).reshape(B, Lc, RET_V_WIDTH) * jax.nn.silu(rgc)
    yc = jnp.concatenate([mc, rc], axis=-1) @ w_out
    return y, yc


def expert_choice_ffn(h, w_router, w1, w3, w2):
    B, T, _ = h.shape
    cap = (EC_CAPACITY_FACTOR * T) // N_EXPERTS
    aff = jax.nn.softmax((h @ w_router).astype(jnp.float32), axis=-1)
    gate, idx = lax.top_k(jnp.swapaxes(aff, 1, 2), cap)
    bidx = jnp.arange(B)[:, None, None]
    xe = h[bidx, idx]
    hid = jax.nn.silu(jnp.einsum('becd,edf->becf', xe, w1)) * jnp.einsum('becd,edf->becf', xe, w3)
    ye = jnp.einsum('becf,efd->becd', hid, w2) * gate[..., None].astype(h.dtype)
    return jnp.zeros_like(h).at[bidx, idx].add(ye)


def setup_inputs(seed: int = 0) -> dict:
    key = jax.random.key(seed)
    ks = jax.random.split(key, 25)
    f32 = jnp.float32

    def nrm(k, shape, scale):
        return jax.random.normal(k, shape, f32) * scale

    def gain(k, shape):
        return 1.0 + 0.02 * jax.random.normal(k, shape, f32)

    decay_init = jnp.log(2.0 ** (5.0 + jnp.arange(RET_HEADS, dtype=f32)) - 1.0)
    return {
        'x': nrm(ks[0], (BATCH, SEQ, D_MODEL), 1.0),
        'c': nrm(ks[1], (BATCH, D_MODEL), 1.0),
        'ctx': nrm(ks[2], (BATCH, CTX_LEN, D_MODEL), 1.0),
        'c_ctx': nrm(ks[3], (D_MODEL,), 1.0),
        'ada_w': nrm(ks[4], (DEPTH, D_MODEL, 6 * D_MODEL), 0.5 * D_MODEL ** -0.5),
        'ada_b': nrm(ks[5], (DEPTH, 6 * D_MODEL), 0.01),
        'norm_mix': gain(ks[6], (DEPTH, D_MODEL)),
        'norm_ffn': gain(ks[7], (DEPTH, D_MODEL)),
        'final_norm': gain(ks[8], (D_MODEL,)),
        'even_w_in': nrm(ks[9], (N_EVEN, D_MODEL, EVEN_IN), D_MODEL ** -0.5),
        'even_w_out': nrm(ks[10], (N_EVEN, EVEN_OUT, D_MODEL), EVEN_OUT ** -0.5),
        'diff_lambda': nrm(ks[11], (N_EVEN, 4, DIFF_HEAD_DIM), 0.1),
        'diff_subln': gain(ks[12], (N_EVEN, 2 * DIFF_HEAD_DIM)),
        'na_rpb': nrm(ks[13], (N_EVEN, NA_HEADS, 2 * NA_WIN_ROWS_MAX - 1, 2 * NA_WIN_COLS - 1), 0.02),
        'odd_w_in': nrm(ks[14], (N_ODD, D_MODEL, ODD_IN), D_MODEL ** -0.5),
        'odd_w_out': nrm(ks[15], (N_ODD, ODD_OUT, D_MODEL), ODD_OUT ** -0.5),
        'mla_q_norm': gain(ks[16], (N_ODD, MLA_Q_RANK)),
        'mla_w_uq': nrm(ks[17], (N_ODD, MLA_Q_RANK, MLA_HEADS * MLA_QK_DIM), MLA_Q_RANK ** -0.5),
        'mla_kv_norm': gain(ks[18], (N_ODD, MLA_KV_RANK)),
        'mla_w_ukv': nrm(ks[19], (N_ODD, MLA_KV_RANK, MLA_HEADS * (MLA_NOPE_DIM + MLA_V_DIM)), MLA_KV_RANK ** -0.5),
        'ret_decay_logit': jnp.broadcast_to(decay_init, (N_ODD, 2, RET_HEADS)) + nrm(ks[20], (N_ODD, 2, RET_HEADS), 0.05),
        'moe_router': nrm(ks[21], (DEPTH, D_MODEL, N_EXPERTS), D_MODEL ** -0.5),
        'moe_w1': nrm(ks[22], (DEPTH, N_EXPERTS, D_MODEL, EXPERT_FF), D_MODEL ** -0.5),
        'moe_w3': nrm(ks[23], (DEPTH, N_EXPERTS, D_MODEL, EXPERT_FF), D_MODEL ** -0.5),
        'moe_w2': nrm(ks[24], (DEPTH, N_EXPERTS, EXPERT_FF, D_MODEL), EXPERT_FF ** -0.5),
    }


def reference(x, c, ctx, c_ctx, ada_w, ada_b, norm_mix, norm_ffn, final_norm, even_w_in, even_w_out,
              diff_lambda, diff_subln, na_rpb, odd_w_in, odd_w_out, mla_q_norm, mla_w_uq, mla_kv_norm,
              mla_w_ukv, ret_decay_logit, moe_router, moe_w1, moe_w3, moe_w2):
    xc = ctx
    silu_c = jax.nn.silu(c)
    silu_cc = jax.nn.silu(c_ctx)
    for l in range(DEPTH):
        need_ctx = l < DEPTH - 1
        mods = jnp.split(silu_c @ ada_w[l] + ada_b[l], 6, axis=-1)
        sh1, sc1, g1, sh2, sc2, g2 = [m[:, None, :] for m in mods]
        sh1c, sc1c, g1c, sh2c, sc2c, g2c = jnp.split(silu_cc @ ada_w[l] + ada_b[l], 6, axis=-1)
        h = rms_norm(x, norm_mix[l]) * (1.0 + sc1) + sh1
        hc = rms_norm(xc, norm_mix[l]) * (1.0 + sc1c) + sh1c
        i = l // 2
        if l % 2 == 0:
            y, yc = even_layer(h, hc, even_w_in[i], even_w_out[i], diff_lambda[i], diff_subln[i], na_rpb[i], l, need_ctx)
        else:
            y, yc = odd_layer(h, hc, odd_w_in[i], odd_w_out[i], mla_q_norm[i], mla_w_uq[i], mla_kv_norm[i],
                              mla_w_ukv[i], ret_decay_logit[i], need_ctx)
        x = x + g1 * y
        h2 = rms_norm(x, norm_ffn[l]) * (1.0 + sc2) + sh2
        x = x + g2 * expert_choice_ffn(h2, moe_router[l], moe_w1[l], moe_w3[l], moe_w2[l])
        if need_ctx:
            xc = xc + g1c * yc
            hc2 = rms_norm(xc, norm_ffn[l]) * (1.0 + sc2c) + sh2c
            xc = xc + g2c * expert_choice_ffn(hc2, moe_router[l], moe_w1[l], moe_w3[l], moe_w2[l])
    return rms_norm(x, final_norm)
```

```python
import functools
import math

import jax
import jax.numpy as jnp
import numpy as np
from jax import lax
from jax.experimental import pallas as pl
from jax.experimental.pallas import tpu as pltpu

GRID_W = 64
ROPE_BASE = 10000.0
RMS_EPS = 1e-6
DIFF_HEADS = 4
DIFF_HEAD_DIM = 64
NA_HEADS = 8
NA_HEAD_DIM = 64
NA_WIN_ROWS = 8
NA_WIN_COLS = 16
MLA_HEADS = 8
MLA_Q_RANK = 256
MLA_KV_RANK = 128
MLA_NOPE_DIM = 64
MLA_ROPE_DIM = 32
MLA_V_DIM = 64
RET_HEADS = 4
RET_QK_DIM = 64
RET_V_DIM = 128
RET_CHUNK = 128
N_EXPERTS = 16
EC_CAPACITY_FACTOR = 2

LANES = 128
VMEM_LIMIT = 56 * 1024 * 1024
NEG_BIG = -1e30
TOK_TILE = 256
NA_GROUP_ROWS = 4

BF16 = jnp.bfloat16
F32 = jnp.float32


def _cparams(*sem):
    return pltpu.CompilerParams(dimension_semantics=sem, vmem_limit_bytes=VMEM_LIMIT)


def _dot(a, b):
    return jnp.dot(a, b, preferred_element_type=F32)


def _dot_nt(a, b):
    return lax.dot_general(a, b, (((1,), (1,)), ((), ())), preferred_element_type=F32)


def _dot_tn(a, b):
    return lax.dot_general(a, b, (((0,), (0,)), ((), ())), preferred_element_type=F32)


def _rms(x):
    return x * lax.rsqrt(jnp.mean(x * x, axis=-1, keepdims=True) + RMS_EPS)


def _silu(x):
    return x * (1.0 / (1.0 + jnp.exp(-x)))


def _tile_lanes(t, n):
    return jnp.concatenate([t] * n, axis=-1)


def _mods_kernel(c_ref, w_ref, b_ref, o_ref):
    o_ref[0] = _dot(_silu(c_ref[...]), w_ref[0]) + b_ref[0]


def _mods(cond, ada_w, ada_b):
    depth, d, n = ada_w.shape
    r = cond.shape[0]
    tn = 1536
    return pl.pallas_call(
        _mods_kernel,
        grid=(depth, n // tn),
        in_specs=[pl.BlockSpec((r, d), lambda l, j: (0, 0)),
                  pl.BlockSpec((1, d, tn), lambda l, j: (l, 0, j)),
                  pl.BlockSpec((1, 1, tn), lambda l, j: (l, 0, j))],
        out_specs=pl.BlockSpec((1, r, tn), lambda l, j: (l, 0, j)),
        out_shape=jax.ShapeDtypeStruct((depth, r, n), F32),
        compiler_params=_cparams("arbitrary", "arbitrary"),
    )(cond, ada_w, ada_b.reshape(depth, 1, n))


def _normed(x, gain, mods, shift_row, scale_row):
    return _rms(x) * gain * (1.0 + mods[scale_row:scale_row + 1]) + mods[shift_row:shift_row + 1]


def _proj_even_kernel(x_ref, m_ref, g_ref, w_ref, cos_ref, sin_ref, o_ref, *, n_rope, n_out):
    h = _normed(x_ref[0], g_ref[...], m_ref[0], 0, 1)
    acc = _dot(h.astype(BF16), w_ref[...])
    reps = n_rope // LANES
    cos = _tile_lanes(cos_ref[...], reps)
    sin = _tile_lanes(sin_ref[...], reps)
    o_ref[0, :, :n_rope] = (acc[:, :n_rope] * cos + acc[:, n_out:] * sin).astype(BF16)
    o_ref[0, :, n_rope:] = acc[:, n_rope:n_out].astype(BF16)


def _proj_even(xa, mods, gain, w, cos, sin, n_lat_tiles, n_rope, n_out):
    b, t, d = xa.shape
    nb = mods.shape[0] - 1
    tm = TOK_TILE
    kern = functools.partial(_proj_even_kernel, n_rope=n_rope, n_out=n_out)
    return pl.pallas_call(
        kern,
        grid=(b, t // tm),
        in_specs=[pl.BlockSpec((1, tm, d), lambda i, j: (i, j, 0)),
                  pl.BlockSpec((1, 6, d), lambda i, j: (jnp.where(j >= n_lat_tiles, nb, i), 0, 0)),
                  pl.BlockSpec((1, d), lambda i, j: (0, 0)),
                  pl.BlockSpec(w.shape, lambda i, j: (0, 0)),
                  pl.BlockSpec((tm, LANES), lambda i, j: (j, 0)),
                  pl.BlockSpec((tm, LANES), lambda i, j: (j, 0))],
        out_specs=pl.BlockSpec((1, tm, n_out), lambda i, j: (i, j, 0)),
        out_shape=jax.ShapeDtypeStruct((b, t, n_out), BF16),
        compiler_params=_cparams("parallel", "arbitrary"),
    )(xa, mods, gain, w, cos, sin)


def _softmax_pv(s, v):
    m = jnp.max(s, axis=-1, keepdims=True)
    p = jnp.exp(s - m)
    l = jnp.sum(p, axis=-1, keepdims=True)
    return _dot(p.astype(BF16), v) / l


def _diff_attn_kernel(lam_ref, g_ref, q_ref, k_ref, v_ref, o_ref, *, n_lat, n_lat_blocks, lam_init):
    lp = lam_ref[...]
    s1 = jnp.sum(lp[0:1] * lp[1:2], axis=-1, keepdims=True)
    s2 = jnp.sum(lp[2:3] * lp[3:4], axis=-1, keepdims=True)
    lam = jnp.exp(s1) - jnp.exp(s2) + lam_init
    q = q_ref[0]
    lane = lax.broadcasted_iota(jnp.int32, (1, LANES), 1)
    q1 = jnp.where(lane < DIFF_HEAD_DIM, q, jnp.zeros_like(q))
    q2 = jnp.where(lane >= DIFF_HEAD_DIM, q, jnp.zeros_like(q))

    def attend(k, v):
        o = _softmax_pv(_dot_nt(q1, k), v) - lam * _softmax_pv(_dot_nt(q2, k), v)
        o_ref[0] = (_rms(o) * g_ref[...] * (1.0 - lam_init)).astype(BF16)

    qi = pl.program_id(2)

    @pl.when(qi < n_lat_blocks)
    def _():
        attend(k_ref[0], v_ref[0])

    @pl.when(qi >= n_lat_blocks)
    def _():
        attend(k_ref[0, n_lat:, :], v_ref[0, n_lat:, :])


def _diff_attn(qkv, lam_params, subln, n_lat, lam_init, tq):
    b, t, _ = qkv.shape
    nh = DIFF_HEADS
    kern = functools.partial(_diff_attn_kernel, n_lat=n_lat, n_lat_blocks=n_lat // tq, lam_init=lam_init)
    return pl.pallas_call(
        kern,
        grid=(b, nh, t // tq),
        in_specs=[pl.BlockSpec(lam_params.shape, lambda i, h, j: (0, 0)),
                  pl.BlockSpec((1, LANES), lambda i, h, j: (0, 0)),
                  pl.BlockSpec((1, tq, LANES), lambda i, h, j: (i, j, h)),
                  pl.BlockSpec((1, t, LANES), lambda i, h, j: (i, 0, nh + h)),
                  pl.BlockSpec((1, t, LANES), lambda i, h, j: (i, 0, 2 * nh + h))],
        out_specs=pl.BlockSpec((1, tq, LANES), lambda i, h, j: (i, j, h)),
        out_shape=jax.ShapeDtypeStruct((b, t, nh * LANES), BF16),
        compiler_params=_cparams("parallel", "parallel", "arbitrary"),
    )(lam_params, subln, qkv, qkv, qkv)


def _na_patterns(rows):
    g = NA_GROUP_ROWS
    span = g + NA_WIN_ROWS - 1
    nq, nk = g * GRID_W, span * GRID_W
    qi = np.arange(nq)[:, None]
    ki = np.arange(nk)[None, :]
    qcol, kcol = qi % GRID_W, ki % GRID_W
    cs = np.clip(qcol - NA_WIN_COLS // 2, 0, GRID_W - NA_WIN_COLS)
    col_ok = (kcol >= cs) & (kcol < cs + NA_WIN_COLS)
    idx_c = np.clip(kcol - qcol + NA_WIN_COLS - 1, 0, 2 * NA_WIN_COLS - 2) + 0 * qi
    seen, pat_of_group, idx_r, valid = {}, [], [], []
    for r0 in range(0, rows, g):
        ks = min(max(r0 - NA_WIN_ROWS // 2, 0), rows - span)
        qrow = r0 + qi // GRID_W
        krow = ks + ki // GRID_W
        rs = np.clip(qrow - NA_WIN_ROWS // 2, 0, rows - NA_WIN_ROWS)
        ok = (krow >= rs) & (krow < rs + NA_WIN_ROWS) & col_ok
        ir = np.clip(krow - qrow + NA_WIN_ROWS - 1, 0, 2 * NA_WIN_ROWS - 2) + 0 * kcol
        key = (ok.tobytes(), np.where(ok, ir, 0).tobytes())
        if key not in seen:
            seen[key] = len(idx_r)
            idx_r.append(ir)
            valid.append(ok)
        pat_of_group.append(seen[key])
    n_pat = len(idx_r)
    return tuple(pat_of_group), np.stack(idx_r), np.stack([idx_c] * n_pat), np.stack(valid)


def _na_kernel(q_ref, k_ref, v_ref, bias_ref, o_ref, *, n_lat, pat_of_group):
    g_rows = NA_GROUP_ROWS
    span = g_rows + NA_WIN_ROWS - 1
    rows = n_lat // GRID_W
    nq, nk = g_rows * GRID_W, span * GRID_W
    lane = lax.broadcasted_iota(jnp.int32, (1, LANES), 1)
    head_masks = (lane < NA_HEAD_DIM, lane >= NA_HEAD_DIM)
    kc = k_ref[0, n_lat:, :]
    vc = v_ref[0, n_lat:, :]

    def group(g, carry):
        r0 = g * g_rows
        ks = jnp.clip(r0 - NA_WIN_ROWS // 2, 0, rows - span)
        common = max(set(pat_of_group), key=pat_of_group.count)
        pat = jnp.int32(common)
        for gi, p in enumerate(pat_of_group):
            if p != common:
                pat = jnp.where(g == gi, p, pat)
        qg = q_ref[0, pl.ds(pl.multiple_of(g * nq, nq), nq), :]
        kw = k_ref[0, pl.ds(pl.multiple_of(ks * GRID_W, GRID_W), nk), :]
        vw = v_ref[0, pl.ds(pl.multiple_of(ks * GRID_W, GRID_W), nk), :]
        out = jnp.zeros((nq, LANES), F32)
        for hh in range(2):
            qh = jnp.where(head_masks[hh], qg, jnp.zeros_like(qg))
            s_loc = _dot_nt(qh, kw) + bias_ref[hh, pat]
            s_ctx = _dot_nt(qh, kc)
            m = jnp.maximum(jnp.max(s_loc, axis=-1, keepdims=True), jnp.max(s_ctx, axis=-1, keepdims=True))
            p_loc = jnp.exp(s_loc - m)
            p_ctx = jnp.exp(s_ctx - m)
            l = jnp.sum(p_loc, axis=-1, keepdims=True) + jnp.sum(p_ctx, axis=-1, keepdims=True)
            o = (_dot(p_loc.astype(BF16), vw) + _dot(p_ctx.astype(BF16), vc)) / l
            out = jnp.where(head_masks[hh], o, out)
        o_ref[0, pl.ds(pl.multiple_of(g * nq, nq), nq), :] = out.astype(BF16)
        return carry

    lax.fori_loop(0, rows // g_rows, group, 0)

    qc = q_ref[0, n_lat:, :]
    out = jnp.zeros(qc.shape, F32)
    for hh in range(2):
        qh = jnp.where(head_masks[hh], qc, jnp.zeros_like(qc))
        out = jnp.where(head_masks[hh], _softmax_pv(_dot_nt(qh, kc), vc), out)
    o_ref[0, n_lat:, :] = out.astype(BF16)


def _na_attn(qkv, bias, n_lat, pat_of_group, col0):
    b, t, _ = qkv.shape
    npairs = NA_HEADS // 2
    cb = col0 // LANES
    kern = functools.partial(_na_kernel, n_lat=n_lat, pat_of_group=pat_of_group)
    return pl.pallas_call(
        kern,
        grid=(b, npairs),
        in_specs=[pl.BlockSpec((1, t, LANES), lambda i, h: (i, 0, cb + h)),
                  pl.BlockSpec((1, t, LANES), lambda i, h: (i, 0, cb + npairs + h)),
                  pl.BlockSpec((1, t, LANES), lambda i, h: (i, 0, cb + 2 * npairs + h)),
                  pl.BlockSpec((2,) + bias.shape[1:], lambda i, h: (h, 0, 0, 0))],
        out_specs=pl.BlockSpec((1, t, LANES), lambda i, h: (i, 0, h)),
        out_shape=jax.ShapeDtypeStruct((b, t, npairs * LANES), BF16),
        compiler_params=_cparams("parallel", "arbitrary"),
    )(qkv, qkv, qkv, bias)


def _proj_out_kernel(x_ref, a_ref, b_ref, wa_ref, wb_ref, m_ref, g_ref, wr_ref, xo_ref, h_ref, aff_ref):
    mods = m_ref[0]
    y = _dot(a_ref[0], wa_ref[...]) + _dot(b_ref[0], wb_ref[...])
    x = x_ref[0] + mods[2:3] * y
    xo_ref[0] = x
    h2 = _normed(x, g_ref[...], mods, 3, 4)
    h_ref[0] = h2.astype(BF16)
    logits = jnp.dot(h2, wr_ref[...], preferred_element_type=F32, precision=lax.Precision.HIGHEST)
    lane = lax.broadcasted_iota(jnp.int32, logits.shape, 1)
    logits = jnp.where(lane < N_EXPERTS, logits, NEG_BIG)
    e = jnp.exp(logits - jnp.max(logits, axis=-1, keepdims=True))
    aff_ref[0] = e / jnp.sum(e, axis=-1, keepdims=True)


def _proj_out(xa, a, bb, w_out, mods, gain, w_router, n_tiles, n_lat_tiles):
    b, t, d = xa.shape
    nb = mods.shape[0] - 1
    tm = TOK_TILE
    wa, wb = w_out[:a.shape[-1]], w_out[a.shape[-1]:]
    tok = lambda i, j: (i, j, 0)
    const = lambda i, j: (0, 0)
    return pl.pallas_call(
        _proj_out_kernel,
        grid=(b, n_tiles),
        in_specs=[pl.BlockSpec((1, tm, d), tok),
                  pl.BlockSpec((1, tm, a.shape[-1]), tok),
                  pl.BlockSpec((1, tm, bb.shape[-1]), tok),
                  pl.BlockSpec(wa.shape, const),
                  pl.BlockSpec(wb.shape, const),
                  pl.BlockSpec((1, 6, d), lambda i, j: (jnp.where(j >= n_lat_tiles, nb, i), 0, 0)),
                  pl.BlockSpec((1, d), const),
                  pl.BlockSpec(w_router.shape, const)],
        out_specs=[pl.BlockSpec((1, tm, d), tok),
                   pl.BlockSpec((1, tm, d), tok),
                   pl.BlockSpec((1, tm, LANES), tok)],
        out_shape=[jax.ShapeDtypeStruct((b, n_tiles * tm, d), F32),
                   jax.ShapeDtypeStruct((b, n_tiles * tm, d), BF16),
                   jax.ShapeDtypeStruct((b, n_tiles * tm, LANES), F32)],
        compiler_params=_cparams("parallel", "arbitrary"),
    )(xa, a, bb, wa, wb, mods, gain, w_router)


def _ffn_kernel(x_ref, gate_ref, w1_ref, w3_ref, w2_ref, o_ref):
    x = x_ref[0]
    hid = _silu(_dot(x, w1_ref[0])) * _dot(x, w3_ref[0])
    o_ref[0] = _dot(hid.astype(BF16), w2_ref[0]) * gate_ref[0]


def _expert_ffn(xe, gate, w1, w3, w2):
    e, m, d = xe.shape
    f = w1.shape[-1]
    tm = min(m, 512)
    return pl.pallas_call(
        _ffn_kernel,
        grid=(e, m // tm),
        in_specs=[pl.BlockSpec((1, tm, d), lambda i, j: (i, j, 0)),
                  pl.BlockSpec((1, tm, 1), lambda i, j: (i, j, 0)),
                  pl.BlockSpec((1, d, f), lambda i, j: (i, 0, 0)),
                  pl.BlockSpec((1, d, f), lambda i, j: (i, 0, 0)),
                  pl.BlockSpec((1, f, d), lambda i, j: (i, 0, 0))],
        out_specs=pl.BlockSpec((1, tm, d), lambda i, j: (i, j, 0)),
        out_shape=jax.ShapeDtypeStruct((e, m, d), F32),
        compiler_params=_cparams("parallel", "arbitrary"),
    )(xe, gate, w1, w3, w2)


def _expert_choice(h2, aff, row0, n, w1, w3, w2):
    b, r, d = h2.shape
    e = N_EXPERTS
    cap = (EC_CAPACITY_FACTOR * n) // e
    gate, idx = lax.top_k(jnp.swapaxes(aff[:, row0:row0 + n, :e], 1, 2), cap)
    gate = jnp.swapaxes(gate, 0, 1).reshape(e, b * cap, 1)
    flat = idx + row0 + jnp.arange(b, dtype=idx.dtype)[:, None, None] * r
    flat = jnp.swapaxes(flat, 0, 1).reshape(e, b * cap)
    xe = jnp.take(h2.reshape(b * r, d), flat, axis=0)
    ye = _expert_ffn(xe, gate, w1, w3, w2)
    return flat.reshape(-1), ye.reshape(-1, d)


def _combine_kernel(x_ref, y_ref, m_ref, g_ref, o_ref, *, final):
    x = x_ref[0] + m_ref[0][5:6] * y_ref[0]
    if final:
        x = _rms(x) * g_ref[...]
    o_ref[0] = x


def _combine(x, y, mods, gain, n_lat_tiles, final):
    b, t, d = x.shape
    nb = mods.shape[0] - 1
    tm = TOK_TILE
    tok = lambda i, j: (i, j, 0)
    return pl.pallas_call(
        functools.partial(_combine_kernel, final=final),
        grid=(b, t // tm),
        in_specs=[pl.BlockSpec((1, tm, d), tok),
                  pl.BlockSpec((1, tm, d), tok),
                  pl.BlockSpec((1, 6, d), lambda i, j: (jnp.where(j >= n_lat_tiles, nb, i), 0, 0)),
                  pl.BlockSpec((1, d), lambda i, j: (0, 0))],
        out_specs=pl.BlockSpec((1, tm, d), tok),
        out_shape=jax.ShapeDtypeStruct((b, t, d), F32),
        compiler_params=_cparams("parallel", "arbitrary"),
    )(x, y, mods, gain)


def _proj_odd_kernel(x_ref, m_ref, g_ref, w_ref, qn_ref, kvn_ref, wq_ref, wkv_ref, cos_ref, sin_ref,
                     q_ref, k_ref, v_ref, rq_ref, rk_ref, rv_ref, rg_ref, *, q_scale):
    h = _normed(x_ref[0], g_ref[...], m_ref[0], 0, 1)
    acc = _dot(h.astype(BF16), w_ref[...])
    cos, sin = cos_ref[...], sin_ref[...]
    c0 = MLA_Q_RANK
    c1 = c0 + MLA_KV_RANK
    cq = acc[:, :c0]
    ckv = acc[:, c0:c1]
    kr = acc[:, c1:c1 + LANES] * cos + acc[:, c1 + LANES:c1 + 2 * LANES] * sin
    c2 = c1 + 2 * LANES
    nq = RET_HEADS * RET_QK_DIM
    nv = RET_HEADS * RET_V_DIM
    rq_ref[0] = acc[:, c2:c2 + nq].astype(BF16)
    rk_ref[0] = acc[:, c2 + nq:c2 + 2 * nq].astype(BF16)
    rv_ref[0] = acc[:, c2 + 2 * nq:c2 + 2 * nq + nv].astype(BF16)
    rg_ref[0] = acc[:, c2 + 2 * nq + nv:]

    nqk = MLA_HEADS * LANES
    qq = _dot((_rms(cq) * qn_ref[...]).astype(BF16), wq_ref[...])
    q = qq[:, :nqk] * _tile_lanes(cos, MLA_HEADS) + qq[:, nqk:] * _tile_lanes(sin, MLA_HEADS)
    q_ref[0] = (q * q_scale).astype(BF16)
    kv = _dot((_rms(ckv) * kvn_ref[...]).astype(BF16), wkv_ref[...])
    k_ref[0] = (kv[:, :nqk] + _tile_lanes(kr, MLA_HEADS)).astype(BF16)
    v_ref[0] = kv[:, nqk:].astype(BF16)


def _proj_odd(xa, mods, gain, w, q_norm, kv_norm, wq, wkv, cos, sin, n_lat_tiles):
    b, t, d = xa.shape
    nb = mods.shape[0] - 1
    tm = TOK_TILE
    tok = lambda i, j: (i, j, 0)
    const = lambda i, j: (0, 0)
    widths = (MLA_HEADS * LANES, MLA_HEADS * LANES, MLA_HEADS * MLA_V_DIM,
              RET_HEADS * RET_QK_DIM, RET_HEADS * RET_QK_DIM, RET_HEADS * RET_V_DIM, RET_HEADS * RET_V_DIM)
    dtypes = (BF16,) * 6 + (F32,)
    kern = functools.partial(_proj_odd_kernel, q_scale=float((MLA_NOPE_DIM + MLA_ROPE_DIM) ** -0.5))
    return pl.pallas_call(
        kern,
        grid=(b, t // tm),
        in_specs=[pl.BlockSpec((1, tm, d), tok),
                  pl.BlockSpec((1, 6, d), lambda i, j: (jnp.where(j >= n_lat_tiles, nb, i), 0, 0)),
                  pl.BlockSpec((1, d), const),
                  pl.BlockSpec(w.shape, const),
                  pl.BlockSpec(q_norm.shape, const),
                  pl.BlockSpec(kv_norm.shape, const),
                  pl.BlockSpec(wq.shape, const),
                  pl.BlockSpec(wkv.shape, const),
                  pl.BlockSpec((tm, LANES), lambda i, j: (j, 0)),
                  pl.BlockSpec((tm, LANES), lambda i, j: (j, 0))],
        out_specs=[pl.BlockSpec((1, tm, wd), tok) for wd in widths],
        out_shape=[jax.ShapeDtypeStruct((b, t, wd), dt) for wd, dt in zip(widths, dtypes)],
        compiler_params=_cparams("parallel", "arbitrary"),
    )(xa, mods, gain, w, q_norm, kv_norm, wq, wkv, cos, sin)


def _mla_kernel(q_ref, k_ref, v_ref, o_ref):
    lane = lax.broadcasted_iota(jnp.int32, (1, LANES), 1)
    v = v_ref[0]
    o0 = _softmax_pv(_dot_nt(q_ref[0, :, :LANES], k_ref[0, :, :LANES]), v)
    o1 = _softmax_pv(_dot_nt(q_ref[0, :, LANES:], k_ref[0, :, LANES:]), v)
    o_ref[0] = jnp.where(lane < MLA_V_DIM, o0, o1).astype(BF16)


def _mla_attn(q, k, v, n_lat, tq):
    b, t, _ = k.shape
    npairs = MLA_HEADS // 2
    return pl.pallas_call(
        _mla_kernel,
        grid=(b, npairs, n_lat // tq),
        in_specs=[pl.BlockSpec((1, tq, 2 * LANES), lambda i, h, j: (i, j, h)),
                  pl.BlockSpec((1, t, 2 * LANES), lambda i, h, j: (i, 0, h)),
                  pl.BlockSpec((1, t, LANES), lambda i, h, j: (i, 0, h))],
        out_specs=pl.BlockSpec((1, tq, LANES), lambda i, h, j: (i, j, h)),
        out_shape=jax.ShapeDtypeStruct((b, n_lat, npairs * LANES), BF16),
        compiler_params=_cparams("parallel", "parallel", "arbitrary"),
    )(q, k, v)


def _retention_kernel(dl_ref, q_ref, k_ref, v_ref, g_ref, o_ref, acc_ref, *, n_lat):
    c = RET_CHUNK
    t = k_ref.shape[1]
    n_chunks = n_lat // c
    n_ctx = (t - n_lat) // c
    hp = pl.program_id(1)
    lane = lax.broadcasted_iota(jnp.int32, (1, LANES), 1)
    pos_r = lax.broadcasted_iota(jnp.int32, (c, 1), 0).astype(F32)
    ii = lax.broadcasted_iota(jnp.int32, (c, c), 0)
    jj = lax.broadcasted_iota(jnp.int32, (c, c), 1)
    rel = (ii - jj).astype(F32)
    dl = dl_ref[...]
    hsel = lax.broadcasted_iota(jnp.int32, (1, RET_HEADS), 1)

    for hh in range(2):
        hmask = (lane >= hh * RET_QK_DIM) & (lane < (hh + 1) * RET_QK_DIM)
        v0 = hh * RET_V_DIM
        for direction in range(2):
            logit = jnp.sum(jnp.where(hsel == 2 * hp + hh, dl[direction:direction + 1], 0.0),
                            axis=-1, keepdims=True)
            lg = jnp.minimum(logit, 0.0) - jnp.log(1.0 + jnp.exp(-jnp.abs(logit)))
            if direction == 0:
                mask = rel >= 0
                d_in = jnp.where(mask, jnp.exp(lg * jnp.where(mask, rel, 0.0)), 0.0)
                d_q = jnp.exp(lg * (pos_r + 1.0))
                d_k = jnp.exp(lg * (c - 1.0 - pos_r))
            else:
                mask = rel < 0
                d_in = jnp.where(mask, jnp.exp(lg * jnp.where(mask, -rel, 0.0)), 0.0)
                d_q = jnp.exp(lg * (c - pos_r))
                d_k = jnp.exp(lg * pos_r)
            d_chunk = jnp.exp(lg * c)

            def kv_update(state, start):
                kb = k_ref[0, pl.ds(start, c), :].astype(F32)
                vb = v_ref[0, pl.ds(start, c), v0:v0 + RET_V_DIM]
                kd = jnp.where(hmask, kb * d_k, 0.0).astype(BF16)
                return d_chunk * state + _dot_tn(kd, vb)

            def ctx_step(i, state):
                ci = i if direction == 0 else n_ctx - 1 - i
                return kv_update(state, pl.multiple_of(n_lat + ci * c, c))

            def lat_step(i, state):
                ci = i if direction == 0 else n_chunks - 1 - i
                start = pl.multiple_of(ci * c, c)
                qb = q_ref[0, pl.ds(start, c), :]
                kb = k_ref[0, pl.ds(start, c), :]
                vb = v_ref[0, pl.ds(start, c), v0:v0 + RET_V_DIM]
                qm = jnp.where(hmask, qb, jnp.zeros_like(qb))
                a = _dot_nt(qm, kb) * d_in
                inner = _dot(a.astype(BF16), vb)
                qd = (qm.astype(F32) * d_q).astype(BF16)
                cross = _dot(qd, state.astype(BF16))
                if direction == 0:
                    acc_ref[pl.ds(start, c), :] = inner + cross
                else:
                    r = _rms(acc_ref[pl.ds(start, c), :] + inner + cross)
                    gate = _silu(g_ref[0, pl.ds(start, c), v0:v0 + RET_V_DIM])
                    o_ref[0, pl.ds(start, c), v0:v0 + RET_V_DIM] = (r * gate).astype(BF16)
                return kv_update(state, start)

            state = lax.fori_loop(0, n_ctx, ctx_step, jnp.zeros((LANES, RET_V_DIM), F32))
            lax.fori_loop(0, n_chunks, lat_step, state)


def _retention(rq, rk, rv, rg, decay_logit, n_lat):
    b, t, _ = rq.shape
    npairs = RET_HEADS // 2
    kern = functools.partial(_retention_kernel, n_lat=n_lat)
    return pl.pallas_call(
        kern,
        grid=(b, npairs),
        in_specs=[pl.BlockSpec(decay_logit.shape, lambda i, h: (0, 0)),
                  pl.BlockSpec((1, n_lat, LANES), lambda i, h: (i, 0, h)),
                  pl.BlockSpec((1, t, LANES), lambda i, h: (i, 0, h)),
                  pl.BlockSpec((1, t, 2 * RET_V_DIM), lambda i, h: (i, 0, h)),
                  pl.BlockSpec((1, n_lat, 2 * RET_V_DIM), lambda i, h: (i, 0, h))],
        out_specs=pl.BlockSpec((1, n_lat, 2 * RET_V_DIM), lambda i, h: (i, 0, h)),
        out_shape=jax.ShapeDtypeStruct((b, n_lat, RET_HEADS * RET_V_DIM), BF16),
        scratch_shapes=[pltpu.VMEM((n_lat, RET_V_DIM), F32)],
        compiler_params=_cparams("parallel", "arbitrary"),
    )(decay_logit, rq, rk, rv, rg)


def _rope_tables(n_lat, n_ctx, rot_dim, lane0):
    t = np.arange(n_lat)
    rows = (t // GRID_W).astype(np.float32)
    cols = (t % GRID_W).astype(np.float32)
    m = rot_dim // 4
    freqs = jnp.asarray(ROPE_BASE, F32) ** (-jnp.arange(m, dtype=F32) / m)
    ang_r = jnp.asarray(rows)[:, None] * freqs
    ang_c = jnp.asarray(cols)[:, None] * freqs
    cos = jnp.concatenate([jnp.cos(ang_r)] * 2 + [jnp.cos(ang_c)] * 2, axis=-1)
    sin = jnp.concatenate([-jnp.sin(ang_r), jnp.sin(ang_r), -jnp.sin(ang_c), jnp.sin(ang_c)], axis=-1)
    reps = (LANES - lane0) // rot_dim if lane0 == 0 else 1
    cos = jnp.concatenate([jnp.ones((n_lat, lane0), F32)] + [cos] * reps
                          + [jnp.ones((n_lat, LANES - lane0 - reps * rot_dim), F32)], axis=-1)
    sin = jnp.concatenate([jnp.zeros((n_lat, lane0), F32)] + [sin] * reps
                          + [jnp.zeros((n_lat, LANES - lane0 - reps * rot_dim), F32)], axis=-1)
    cos = jnp.concatenate([cos, jnp.ones((n_ctx, LANES), F32)], axis=0)
    sin = jnp.concatenate([sin, jnp.zeros((n_ctx, LANES), F32)], axis=0)
    return cos, sin


def _rot_partner(w, rot_dim):
    k, n = w.shape
    q = rot_dim // 4
    return w.reshape(k, n // rot_dim, 2, 2, q)[:, :, :, ::-1, :].reshape(k, n)


def _pad_cols(w, lane0, width=LANES):
    k, n = w.shape
    return jnp.concatenate([jnp.zeros((k, lane0), w.dtype), w, jnp.zeros((k, width - lane0 - n), w.dtype)], axis=-1)


def _even_weights(w_in):
    n_rope = 2 * DIFF_HEADS * 2 * DIFF_HEAD_DIM
    dw = DIFF_HEADS * 2 * DIFF_HEAD_DIM
    nw = NA_HEADS * NA_HEAD_DIM
    scale = jnp.concatenate([jnp.full((dw,), DIFF_HEAD_DIM ** -0.5, F32), jnp.ones((2 * dw,), F32),
                             jnp.full((nw,), NA_HEAD_DIM ** -0.5, F32), jnp.ones((2 * nw,), F32)])
    w = w_in * scale
    return jnp.concatenate([w, _rot_partner(w[:, :n_rope], DIFF_HEAD_DIM)], axis=-1).astype(BF16), n_rope


def _odd_weights(w_in, w_uq, w_ukv):
    c0 = MLA_Q_RANK
    c1 = c0 + MLA_KV_RANK
    c2 = c1 + MLA_ROPE_DIM
    nq = RET_HEADS * RET_QK_DIM
    kr = w_in[:, c1:c2]
    w = jnp.concatenate([w_in[:, :c1],
                         _pad_cols(kr, MLA_NOPE_DIM), _pad_cols(_rot_partner(kr, MLA_ROPE_DIM), MLA_NOPE_DIM),
                         w_in[:, c2:c2 + nq], w_in[:, c2 + nq:c2 + 2 * nq] * (RET_QK_DIM ** -0.5),
                         w_in[:, c2 + 2 * nq:]], axis=-1).astype(BF16)
    r = w_uq.shape[0]
    uq = w_uq.reshape(r, MLA_HEADS, MLA_NOPE_DIM + MLA_ROPE_DIM)
    pad = jnp.zeros((r, MLA_HEADS, LANES - MLA_NOPE_DIM - MLA_ROPE_DIM), F32)
    uq_rot = _rot_partner(uq[:, :, MLA_NOPE_DIM:].reshape(r, -1), MLA_ROPE_DIM).reshape(r, MLA_HEADS, MLA_ROPE_DIM)
    wq = jnp.concatenate([uq, pad], axis=-1).reshape(r, -1)
    wq_rot = jnp.concatenate([jnp.zeros_like(uq[:, :, :MLA_NOPE_DIM]), uq_rot, pad], axis=-1).reshape(r, -1)
    rk = w_ukv.shape[0]
    ukv = w_ukv.reshape(rk, MLA_HEADS, MLA_NOPE_DIM + MLA_V_DIM)
    wk = jnp.concatenate([ukv[:, :, :MLA_NOPE_DIM], jnp.zeros((rk, MLA_HEADS, LANES - MLA_NOPE_DIM), F32)],
                         axis=-1).reshape(rk, -1)
    wv = ukv[:, :, MLA_NOPE_DIM:].reshape(rk, -1)
    return w, jnp.concatenate([wq, wq_rot], axis=-1).astype(BF16), jnp.concatenate([wk, wv], axis=-1).astype(BF16)


def _na_bias(rpb, n_lat):
    pat_of_group, idx_r, idx_c, valid = _na_patterns(n_lat // GRID_W)
    bias = rpb[:, idx_r, idx_c]
    return jnp.where(valid[None], bias, NEG_BIG), pat_of_group


def _router_weights(w_router):
    return _pad_cols(w_router, 0)


def kernel(x, c, ctx, c_ctx, ada_w, ada_b, norm_mix, norm_ffn, final_norm, even_w_in, even_w_out,
           diff_lambda, diff_subln, na_rpb, odd_w_in, odd_w_out, mla_q_norm, mla_w_uq, mla_kv_norm,
           mla_w_ukv, ret_decay_logit, moe_router, moe_w1, moe_w3, moe_w2):
    b, n_lat, d = x.shape
    n_ctx = ctx.shape[1]
    depth = ada_w.shape[0]
    n_lat_tiles = n_lat // TOK_TILE
    n_tiles = (n_lat + n_ctx) // TOK_TILE

    cond = jnp.concatenate([c, c_ctx[None], jnp.zeros((7, d), F32)], axis=0)
    mods_all = _mods(cond, ada_w, ada_b)[:, :b + 1].reshape(depth, b + 1, 6, d)
    xa = jnp.concatenate([x, ctx], axis=1)

    for l in range(depth):
        mods = mods_all[l]
        need_ctx = l < depth - 1
        i = l // 2
        w1, w3, w2 = moe_w1[l].astype(BF16), moe_w3[l].astype(BF16), moe_w2[l].astype(BF16)
        gain_mix = norm_mix[l][None]
        if l % 2 == 0:
            w, n_rope = _even_weights(even_w_in[i])
            cos, sin = _rope_tables(n_lat, n_ctx, DIFF_HEAD_DIM, 0)
            qkv = _proj_even(xa, mods, gain_mix, w, cos, sin, n_lat_tiles, n_rope, even_w_in.shape[-1])
            lam_init = 0.8 - 0.6 * math.exp(-0.3 * l)
            mix_a = _diff_attn(qkv, diff_lambda[i], diff_subln[i][None], n_lat, lam_init, TOK_TILE)
            bias, offsets = _na_bias(na_rpb[i], n_lat)
            mix_b = _na_attn(qkv, bias, n_lat, offsets, 3 * DIFF_HEADS * 2 * DIFF_HEAD_DIM)
            w_out = even_w_out[i].astype(BF16)
        else:
            w, wq, wkv = _odd_weights(odd_w_in[i], mla_w_uq[i], mla_w_ukv[i])
            cos, sin = _rope_tables(n_lat, n_ctx, MLA_ROPE_DIM, MLA_NOPE_DIM)
            q, k, v, rq, rk, rv, rg = _proj_odd(xa, mods, gain_mix, w, mla_q_norm[i][None], mla_kv_norm[i][None],
                                                wq, wkv, cos, sin, n_lat_tiles)
            mix_a = _mla_attn(q, k, v, n_lat, TOK_TILE)
            mix_b = _retention(rq, rk, rv, rg, ret_decay_logit[i], n_lat)
            w_out = odd_w_out[i].astype(BF16)

        tiles = n_tiles if need_ctx else n_lat_tiles
        rows = tiles * TOK_TILE
        x_mid, h2, aff = _proj_out(xa, mix_a, mix_b, w_out, mods, norm_ffn[l][None],
                                   _router_weights(moe_router[l]), tiles, n_lat_tiles)
        flat, ye = _expert_choice(h2, aff, 0, n_lat, w1, w3, w2)
        moe = jnp.zeros((b * rows, d), F32).at[flat].add(ye)
        if need_ctx:
            flat_c, ye_c = _expert_choice(h2, aff, n_lat, n_ctx, w1, w3, w2)
            moe = moe.at[flat_c].add(ye_c)
        moe = moe.reshape(b, rows, d)
        final = l == depth - 1
        xa = _combine(x_mid, moe, mods, final_norm[None], n_lat_tiles, final)
    return xa[:, :n_lat]
```

```python
import functools
import math

import jax
import jax.numpy as jnp
import numpy as np
from jax import lax
from jax.experimental import pallas as pl
from jax.experimental.pallas import tpu as pltpu

GRID_W = 64
ROPE_BASE = 10000.0
RMS_EPS = 1e-6
DIFF_HEADS = 4
DIFF_HEAD_DIM = 64
NA_HEADS = 8
NA_HEAD_DIM = 64
NA_WIN_ROWS = 8
NA_WIN_COLS = 16
MLA_HEADS = 8
MLA_Q_RANK = 256
MLA_KV_RANK = 128
MLA_NOPE_DIM = 64
MLA_ROPE_DIM = 32
MLA_V_DIM = 64
RET_HEADS = 4
RET_QK_DIM = 64
RET_V_DIM = 128
RET_CHUNK = 128
N_EXPERTS = 16
EC_CAPACITY_FACTOR = 2

LANES = 128
VMEM_LIMIT = 56 * 1024 * 1024
NEG_BIG = -1e30
TOK_TILE = 256
NA_GROUP_ROWS = 4
PAIR_CHUNK = 256

BF16 = jnp.bfloat16
F32 = jnp.float32


def _cparams(*sem):
    return pltpu.CompilerParams(dimension_semantics=sem, vmem_limit_bytes=VMEM_LIMIT)


def _dot(a, b):
    return jnp.dot(a, b, preferred_element_type=F32)


def _dot_nt(a, b):
    return lax.dot_general(a, b, (((1,), (1,)), ((), ())), preferred_element_type=F32)


def _dot_tn(a, b):
    return lax.dot_general(a, b, (((0,), (0,)), ((), ())), preferred_element_type=F32)


def _rms(x):
    return x * lax.rsqrt(jnp.mean(x * x, axis=-1, keepdims=True) + RMS_EPS)


def _silu(x):
    return x * (1.0 / (1.0 + jnp.exp(-x)))


def _tile_lanes(t, n):
    return jnp.concatenate([t] * n, axis=-1)


def _mods_kernel(c_ref, w_ref, b_ref, o_ref):
    o_ref[0] = _dot(_silu(c_ref[...]), w_ref[0]) + b_ref[0]


def _mods(cond, ada_w, ada_b):
    depth, d, n = ada_w.shape
    r = cond.shape[0]
    tn = 1536
    return pl.pallas_call(
        _mods_kernel,
        grid=(depth, n // tn),
        in_specs=[pl.BlockSpec((r, d), lambda l, j: (0, 0)),
                  pl.BlockSpec((1, d, tn), lambda l, j: (l, 0, j)),
                  pl.BlockSpec((1, 1, tn), lambda l, j: (l, 0, j))],
        out_specs=pl.BlockSpec((1, r, tn), lambda l, j: (l, 0, j)),
        out_shape=jax.ShapeDtypeStruct((depth, r, n), F32),
        compiler_params=_cparams("arbitrary", "arbitrary"),
    )(cond, ada_w, ada_b.reshape(depth, 1, n))


def _normed(x, gain, mods, shift_row, scale_row):
    return _rms(x) * gain * (1.0 + mods[scale_row:scale_row + 1]) + mods[shift_row:shift_row + 1]


def _proj_even_kernel(x_ref, m_ref, g_ref, w_ref, cos_ref, sin_ref, o_ref, *, n_rope, n_out):
    h = _normed(x_ref[0], g_ref[...], m_ref[0], 0, 1)
    acc = _dot(h.astype(BF16), w_ref[...])
    reps = n_rope // LANES
    cos = _tile_lanes(cos_ref[...], reps)
    sin = _tile_lanes(sin_ref[...], reps)
    o_ref[0, :, :n_rope] = (acc[:, :n_rope] * cos + acc[:, n_out:] * sin).astype(BF16)
    o_ref[0, :, n_rope:] = acc[:, n_rope:n_out].astype(BF16)


def _proj_even(xa, mods, gain, w, cos, sin, n_lat_tiles, n_rope, n_out):
    b, t, d = xa.shape
    nb = mods.shape[0] - 1
    tm = TOK_TILE
    kern = functools.partial(_proj_even_kernel, n_rope=n_rope, n_out=n_out)
    return pl.pallas_call(
        kern,
        grid=(b, t // tm),
        in_specs=[pl.BlockSpec((1, tm, d), lambda i, j: (i, j, 0)),
                  pl.BlockSpec((1, 6, d), lambda i, j: (jnp.where(j >= n_lat_tiles, nb, i), 0, 0)),
                  pl.BlockSpec((1, d), lambda i, j: (0, 0)),
                  pl.BlockSpec(w.shape, lambda i, j: (0, 0)),
                  pl.BlockSpec((tm, LANES), lambda i, j: (j, 0)),
                  pl.BlockSpec((tm, LANES), lambda i, j: (j, 0))],
        out_specs=pl.BlockSpec((1, tm, n_out), lambda i, j: (i, j, 0)),
        out_shape=jax.ShapeDtypeStruct((b, t, n_out), BF16),
        compiler_params=_cparams("parallel", "arbitrary"),
    )(xa, mods, gain, w, cos, sin)


def _softmax_pv(s, v):
    m = jnp.max(s, axis=-1, keepdims=True)
    p = jnp.exp(s - m)
    l = jnp.sum(p, axis=-1, keepdims=True)
    return _dot(p.astype(BF16), v) / l


def _diff_attn_kernel(lam_ref, g_ref, q_ref, k_ref, v_ref, o_ref, *, n_lat, n_lat_blocks, lam_init):
    lp = lam_ref[...]
    s1 = jnp.sum(lp[0:1] * lp[1:2], axis=-1, keepdims=True)
    s2 = jnp.sum(lp[2:3] * lp[3:4], axis=-1, keepdims=True)
    lam = jnp.exp(s1) - jnp.exp(s2) + lam_init
    q = q_ref[0]
    lane = lax.broadcasted_iota(jnp.int32, (1, LANES), 1)
    q1 = jnp.where(lane < DIFF_HEAD_DIM, q, jnp.zeros_like(q))
    q2 = jnp.where(lane >= DIFF_HEAD_DIM, q, jnp.zeros_like(q))

    def attend(k, v):
        o = _softmax_pv(_dot_nt(q1, k), v) - lam * _softmax_pv(_dot_nt(q2, k), v)
        o_ref[0] = (_rms(o) * g_ref[...] * (1.0 - lam_init)).astype(BF16)

    qi = pl.program_id(2)

    @pl.when(qi < n_lat_blocks)
    def _():
        attend(k_ref[0], v_ref[0])

    @pl.when(qi >= n_lat_blocks)
    def _():
        attend(k_ref[0, n_lat:, :], v_ref[0, n_lat:, :])


def _diff_attn(qkv, lam_params, subln, n_lat, lam_init, tq):
    b, t, _ = qkv.shape
    nh = DIFF_HEADS
    kern = functools.partial(_diff_attn_kernel, n_lat=n_lat, n_lat_blocks=n_lat // tq, lam_init=lam_init)
    return pl.pallas_call(
        kern,
        grid=(b, nh, t // tq),
        in_specs=[pl.BlockSpec(lam_params.shape, lambda i, h, j: (0, 0)),
                  pl.BlockSpec((1, LANES), lambda i, h, j: (0, 0)),
                  pl.BlockSpec((1, tq, LANES), lambda i, h, j: (i, j, h)),
                  pl.BlockSpec((1, t, LANES), lambda i, h, j: (i, 0, nh + h)),
                  pl.BlockSpec((1, t, LANES), lambda i, h, j: (i, 0, 2 * nh + h))],
        out_specs=pl.BlockSpec((1, tq, LANES), lambda i, h, j: (i, j, h)),
        out_shape=jax.ShapeDtypeStruct((b, t, nh * LANES), BF16),
        compiler_params=_cparams("parallel", "parallel", "arbitrary"),
    )(lam_params, subln, qkv, qkv, qkv)


def _na_patterns(rows):
    g = NA_GROUP_ROWS
    span = g + NA_WIN_ROWS - 1
    nq, nk = g * GRID_W, span * GRID_W
    qi = np.arange(nq)[:, None]
    ki = np.arange(nk)[None, :]
    qcol, kcol = qi % GRID_W, ki % GRID_W
    cs = np.clip(qcol - NA_WIN_COLS // 2, 0, GRID_W - NA_WIN_COLS)
    col_ok = (kcol >= cs) & (kcol < cs + NA_WIN_COLS)
    idx_c = np.clip(kcol - qcol + NA_WIN_COLS - 1, 0, 2 * NA_WIN_COLS - 2) + 0 * qi
    seen, pat_of_group, idx_r, valid = {}, [], [], []
    for r0 in range(0, rows, g):
        ks = min(max(r0 - NA_WIN_ROWS // 2, 0), rows - span)
        qrow = r0 + qi // GRID_W
        krow = ks + ki // GRID_W
        rs = np.clip(qrow - NA_WIN_ROWS // 2, 0, rows - NA_WIN_ROWS)
        ok = (krow >= rs) & (krow < rs + NA_WIN_ROWS) & col_ok
        ir = np.clip(krow - qrow + NA_WIN_ROWS - 1, 0, 2 * NA_WIN_ROWS - 2) + 0 * kcol
        key = (ok.tobytes(), np.where(ok, ir, 0).tobytes())
        if key not in seen:
            seen[key] = len(idx_r)
            idx_r.append(ir)
            valid.append(ok)
        pat_of_group.append(seen[key])
    n_pat = len(idx_r)
    return tuple(pat_of_group), np.stack(idx_r), np.stack([idx_c] * n_pat), np.stack(valid)


def _na_kernel(q_ref, k_ref, v_ref, bias_ref, o_ref, *, n_lat, pat_of_group):
    g_rows = NA_GROUP_ROWS
    span = g_rows + NA_WIN_ROWS - 1
    rows = n_lat // GRID_W
    nq, nk = g_rows * GRID_W, span * GRID_W
    lane = lax.broadcasted_iota(jnp.int32, (1, LANES), 1)
    head_masks = (lane < NA_HEAD_DIM, lane >= NA_HEAD_DIM)
    kc = k_ref[0, n_lat:, :]
    vc = v_ref[0, n_lat:, :]

    def group(g, carry):
        r0 = g * g_rows
        ks = jnp.clip(r0 - NA_WIN_ROWS // 2, 0, rows - span)
        common = max(set(pat_of_group), key=pat_of_group.count)
        pat = jnp.int32(common)
        for gi, p in enumerate(pat_of_group):
            if p != common:
                pat = jnp.where(g == gi, p, pat)
        qg = q_ref[0, pl.ds(pl.multiple_of(g * nq, nq), nq), :]
        kw = k_ref[0, pl.ds(pl.multiple_of(ks * GRID_W, GRID_W), nk), :]
        vw = v_ref[0, pl.ds(pl.multiple_of(ks * GRID_W, GRID_W), nk), :]
        out = jnp.zeros((nq, LANES), F32)
        for hh in range(2):
            qh = jnp.where(head_masks[hh], qg, jnp.zeros_like(qg))
            s_loc = _dot_nt(qh, kw) + bias_ref[hh, pat]
            s_ctx = _dot_nt(qh, kc)
            m = jnp.maximum(jnp.max(s_loc, axis=-1, keepdims=True), jnp.max(s_ctx, axis=-1, keepdims=True))
            p_loc = jnp.exp(s_loc - m)
            p_ctx = jnp.exp(s_ctx - m)
            l = jnp.sum(p_loc, axis=-1, keepdims=True) + jnp.sum(p_ctx, axis=-1, keepdims=True)
            o = (_dot(p_loc.astype(BF16), vw) + _dot(p_ctx.astype(BF16), vc)) / l
            out = jnp.where(head_masks[hh], o, out)
        o_ref[0, pl.ds(pl.multiple_of(g * nq, nq), nq), :] = out.astype(BF16)
        return carry

    lax.fori_loop(0, rows // g_rows, group, 0)

    qc = q_ref[0, n_lat:, :]
    out = jnp.zeros(qc.shape, F32)
    for hh in range(2):
        qh = jnp.where(head_masks[hh], qc, jnp.zeros_like(qc))
        out = jnp.where(head_masks[hh], _softmax_pv(_dot_nt(qh, kc), vc), out)
    o_ref[0, n_lat:, :] = out.astype(BF16)


def _na_attn(qkv, bias, n_lat, pat_of_group, col0):
    b, t, _ = qkv.shape
    npairs = NA_HEADS // 2
    cb = col0 // LANES
    kern = functools.partial(_na_kernel, n_lat=n_lat, pat_of_group=pat_of_group)
    return pl.pallas_call(
        kern,
        grid=(b, npairs),
        in_specs=[pl.BlockSpec((1, t, LANES), lambda i, h: (i, 0, cb + h)),
                  pl.BlockSpec((1, t, LANES), lambda i, h: (i, 0, cb + npairs + h)),
                  pl.BlockSpec((1, t, LANES), lambda i, h: (i, 0, cb + 2 * npairs + h)),
                  pl.BlockSpec((2,) + bias.shape[1:], lambda i, h: (h, 0, 0, 0))],
        out_specs=pl.BlockSpec((1, t, LANES), lambda i, h: (i, 0, h)),
        out_shape=jax.ShapeDtypeStruct((b, t, npairs * LANES), BF16),
        compiler_params=_cparams("parallel", "arbitrary"),
    )(qkv, qkv, qkv, bias)


def _proj_out_kernel(x_ref, a_ref, b_ref, wa_ref, wb_ref, m_ref, g_ref, wr_ref, xo_ref, h_ref, aff_ref):
    mods = m_ref[0]
    y = _dot(a_ref[0], wa_ref[...]) + _dot(b_ref[0], wb_ref[...])
    x = x_ref[0] + mods[2:3] * y
    xo_ref[0] = x
    h2 = _normed(x, g_ref[...], mods, 3, 4)
    h_hi = h2.astype(BF16)
    h_ref[0] = h_hi
    h_lo = (h2 - h_hi.astype(F32)).astype(BF16)
    wr = wr_ref[...]
    w_hi = wr.astype(BF16)
    w_lo = (wr - w_hi.astype(F32)).astype(BF16)
    logits = _dot(h_hi, w_hi) + (_dot(h_hi, w_lo) + _dot(h_lo, w_hi))
    lane = lax.broadcasted_iota(jnp.int32, logits.shape, 1)
    logits = jnp.where(lane < N_EXPERTS, logits, NEG_BIG)
    e = jnp.exp(logits - jnp.max(logits, axis=-1, keepdims=True))
    aff_ref[0] = e / jnp.sum(e, axis=-1, keepdims=True)


def _proj_out(xa, a, bb, w_out, mods, gain, w_router, n_tiles, n_lat_tiles):
    b, t, d = xa.shape
    nb = mods.shape[0] - 1
    tm = TOK_TILE
    wa, wb = w_out[:a.shape[-1]], w_out[a.shape[-1]:]
    tok = lambda i, j: (i, j, 0)
    const = lambda i, j: (0, 0)
    return pl.pallas_call(
        _proj_out_kernel,
        grid=(b, n_tiles),
        in_specs=[pl.BlockSpec((1, tm, d), tok),
                  pl.BlockSpec((1, tm, a.shape[-1]), tok),
                  pl.BlockSpec((1, tm, bb.shape[-1]), tok),
                  pl.BlockSpec(wa.shape, const),
                  pl.BlockSpec(wb.shape, const),
                  pl.BlockSpec((1, 6, d), lambda i, j: (jnp.where(j >= n_lat_tiles, nb, i), 0, 0)),
                  pl.BlockSpec((1, d), const),
                  pl.BlockSpec(w_router.shape, const)],
        out_specs=[pl.BlockSpec((1, tm, d), tok),
                   pl.BlockSpec((1, tm, d), tok),
                   pl.BlockSpec((1, tm, LANES), tok)],
        out_shape=[jax.ShapeDtypeStruct((b, n_tiles * tm, d), F32),
                   jax.ShapeDtypeStruct((b, n_tiles * tm, d), BF16),
                   jax.ShapeDtypeStruct((b, n_tiles * tm, LANES), F32)],
        compiler_params=_cparams("parallel", "arbitrary"),
    )(xa, a, bb, wa, wb, mods, gain, w_router)


def _ffn_kernel(x_ref, gate_ref, w1_ref, w3_ref, w2_ref, o_ref, w1b, w3b, w2b):
    @pl.when(pl.program_id(1) == 0)
    def _():
        w1b[...] = w1_ref[0].astype(BF16)
        w3b[...] = w3_ref[0].astype(BF16)
        w2b[...] = w2_ref[0].astype(BF16)

    x = x_ref[0]
    hid = _silu(_dot(x, w1b[...])) * _dot(x, w3b[...])
    o_ref[0] = (_dot(hid.astype(BF16), w2b[...]) * gate_ref[0]).astype(BF16)


def _expert_ffn(xe, gate, w1, w3, w2, layer):
    e, m, d = xe.shape
    f = w1.shape[-1]
    tm = next(c for c in (512, 256, 128, 64, 32) if m % c == 0)
    wspec = lambda r, c: pl.BlockSpec((None, 1, r, c), lambda i, j: (layer, i, 0, 0))
    return pl.pallas_call(
        _ffn_kernel,
        grid=(e, m // tm),
        in_specs=[pl.BlockSpec((1, tm, d), lambda i, j: (i, j, 0)),
                  pl.BlockSpec((1, tm, 1), lambda i, j: (i, j, 0)),
                  wspec(d, f), wspec(d, f), wspec(f, d)],
        out_specs=pl.BlockSpec((1, tm, d), lambda i, j: (i, j, 0)),
        out_shape=jax.ShapeDtypeStruct((e, m, d), BF16),
        scratch_shapes=[pltpu.VMEM((d, f), BF16), pltpu.VMEM((d, f), BF16), pltpu.VMEM((f, d), BF16)],
        compiler_params=_cparams("parallel", "arbitrary"),
    )(xe, gate, w1, w3, w2)


def _route(aff, segments):
    b, r, _ = aff.shape
    e = N_EXPERTS
    gates, flats, toks = [], [], []
    for row0, n in segments:
        cap = (EC_CAPACITY_FACTOR * n) // e
        gate, idx = lax.top_k(jnp.swapaxes(aff[:, row0:row0 + n, :e], 1, 2), cap)
        tok = idx + row0
        gates.append(jnp.swapaxes(gate, 0, 1).reshape(e, b * cap))
        flats.append(jnp.swapaxes(tok + jnp.arange(b, dtype=idx.dtype)[:, None, None] * r, 0, 1).reshape(e, b * cap))
        toks.append(tok)
    return jnp.concatenate(gates, axis=1)[..., None], jnp.concatenate(flats, axis=1), toks


def _combine_kernel(lo_ref, x_ref, tok_ref, y_ref, m_ref, g_ref, o_ref, *, final):
    i, j = pl.program_id(0), pl.program_id(1)
    tm = x_ref.shape[1]
    n_tiles = pl.num_programs(1)
    lo = lo_ref[i * (n_tiles + 1) + j]
    hi = lo_ref[i * (n_tiles + 1) + j + 1]
    rows = j * tm + lax.broadcasted_iota(jnp.int32, (tm, 1), 0)
    o_ref[0] = jnp.zeros(o_ref.shape[1:], F32)

    def chunk(c, carry):
        sel = (tok_ref[0, pl.ds(c, 1), :] == rows).astype(BF16)
        o_ref[0] += _dot(sel, y_ref[0, pl.ds(pl.multiple_of(c * PAIR_CHUNK, PAIR_CHUNK), PAIR_CHUNK), :])
        return carry

    lax.fori_loop(lo // PAIR_CHUNK, (hi + PAIR_CHUNK - 1) // PAIR_CHUNK, chunk, 0)
    x = x_ref[0] + m_ref[0][5:6] * o_ref[0]
    if final:
        x = _rms(x) * g_ref[...]
    o_ref[0] = x


def _combine(x, tok_sorted, y_sorted, lo, mods, gain, n_lat_tiles, final):
    b, t, d = x.shape
    p = tok_sorted.shape[1]
    nb = mods.shape[0] - 1
    tm = TOK_TILE
    tok = lambda i, j, lo_ref: (i, j, 0)
    whole = lambda i, j, lo_ref: (i, 0, 0)
    grid_spec = pltpu.PrefetchScalarGridSpec(
        num_scalar_prefetch=1,
        grid=(b, t // tm),
        in_specs=[pl.BlockSpec((1, tm, d), tok),
                  pl.BlockSpec((1, p // PAIR_CHUNK, PAIR_CHUNK), whole),
                  pl.BlockSpec((1, p, d), whole, pipeline_mode=pl.Buffered(1)),
                  pl.BlockSpec((1, 6, d), lambda i, j, lo_ref: (jnp.where(j >= n_lat_tiles, nb, i), 0, 0)),
                  pl.BlockSpec((1, d), lambda i, j, lo_ref: (0, 0))],
        out_specs=pl.BlockSpec((1, tm, d), tok))
    return pl.pallas_call(
        functools.partial(_combine_kernel, final=final),
        grid_spec=grid_spec,
        out_shape=jax.ShapeDtypeStruct((b, t, d), F32),
        compiler_params=_cparams("parallel", "arbitrary"),
    )(lo, x, tok_sorted.reshape(b, p // PAIR_CHUNK, PAIR_CHUNK), y_sorted, mods, gain)


def _moe(x_mid, h2, aff, segments, mods, gain, n_lat_tiles, final, w1, w3, w2, layer):
    b, r, d = h2.shape
    gate, flat, toks = _route(aff, segments)
    e, m = flat.shape
    xe = jnp.take(h2.reshape(b * r, d), flat, axis=0)
    ye = _expert_ffn(xe, gate, w1, w3, w2, layer)
    srcs, off = [], 0
    for tok in toks:
        cap = tok.shape[-1]
        src = (jnp.arange(e, dtype=jnp.int32)[None, :, None] * m + off
               + jnp.arange(b, dtype=jnp.int32)[:, None, None] * cap + jnp.arange(cap, dtype=jnp.int32))
        srcs.append(src.reshape(b, e * cap))
        off += b * cap
    tok_all = jnp.concatenate([tok.reshape(b, -1) for tok in toks], axis=1)
    tok_sorted, src_sorted = lax.sort_key_val(tok_all, jnp.concatenate(srcs, axis=1), dimension=1)
    y_sorted = jnp.take(ye.reshape(e * m, d), src_sorted, axis=0)
    starts = jnp.arange(r // TOK_TILE + 1, dtype=jnp.int32) * TOK_TILE
    lo = jnp.sum((tok_sorted[:, None, :] < starts[None, :, None]).astype(jnp.int32), axis=-1)
    return _combine(x_mid, tok_sorted, y_sorted, lo.reshape(-1), mods, gain, n_lat_tiles, final)


def _proj_odd_kernel(x_ref, m_ref, g_ref, w_ref, qn_ref, kvn_ref, wq_ref, wkv_ref, cos_ref, sin_ref,
                     q_ref, k_ref, v_ref, rq_ref, rk_ref, rv_ref, rg_ref, *, q_scale):
    h = _normed(x_ref[0], g_ref[...], m_ref[0], 0, 1)
    acc = _dot(h.astype(BF16), w_ref[...])
    cos, sin = cos_ref[...], sin_ref[...]
    c0 = MLA_Q_RANK
    c1 = c0 + MLA_KV_RANK
    cq = acc[:, :c0]
    ckv = acc[:, c0:c1]
    kr = acc[:, c1:c1 + LANES] * cos + acc[:, c1 + LANES:c1 + 2 * LANES] * sin
    c2 = c1 + 2 * LANES
    nq = RET_HEADS * RET_QK_DIM
    nv = RET_HEADS * RET_V_DIM
    rq_ref[0] = acc[:, c2:c2 + nq].astype(BF16)
    rk_ref[0] = acc[:, c2 + nq:c2 + 2 * nq].astype(BF16)
    rv_ref[0] = acc[:, c2 + 2 * nq:c2 + 2 * nq + nv].astype(BF16)
    rg_ref[0] = acc[:, c2 + 2 * nq + nv:]

    nqk = MLA_HEADS * LANES
    qq = _dot((_rms(cq) * qn_ref[...]).astype(BF16), wq_ref[...])
    q = qq[:, :nqk] * _tile_lanes(cos, MLA_HEADS) + qq[:, nqk:] * _tile_lanes(sin, MLA_HEADS)
    q_ref[0] = (q * q_scale).astype(BF16)
    kv = _dot((_rms(ckv) * kvn_ref[...]).astype(BF16), wkv_ref[...])
    k_ref[0] = (kv[:, :nqk] + _tile_lanes(kr, MLA_HEADS)).astype(BF16)
    v_ref[0] = kv[:, nqk:].astype(BF16)


def _proj_odd(xa, mods, gain, w, q_norm, kv_norm, wq, wkv, cos, sin, n_lat_tiles):
    b, t, d = xa.shape
    nb = mods.shape[0] - 1
    tm = TOK_TILE
    tok = lambda i, j: (i, j, 0)
    const = lambda i, j: (0, 0)
    widths = (MLA_HEADS * LANES, MLA_HEADS * LANES, MLA_HEADS * MLA_V_DIM,
              RET_HEADS * RET_QK_DIM, RET_HEADS * RET_QK_DIM, RET_HEADS * RET_V_DIM, RET_HEADS * RET_V_DIM)
    dtypes = (BF16,) * 6 + (F32,)
    kern = functools.partial(_proj_odd_kernel, q_scale=float((MLA_NOPE_DIM + MLA_ROPE_DIM) ** -0.5))
    return pl.pallas_call(
        kern,
        grid=(b, t // tm),
        in_specs=[pl.BlockSpec((1, tm, d), tok),
                  pl.BlockSpec((1, 6, d), lambda i, j: (jnp.where(j >= n_lat_tiles, nb, i), 0, 0)),
                  pl.BlockSpec((1, d), const),
                  pl.BlockSpec(w.shape, const),
                  pl.BlockSpec(q_norm.shape, const),
                  pl.BlockSpec(kv_norm.shape, const),
                  pl.BlockSpec(wq.shape, const),
                  pl.BlockSpec(wkv.shape, const),
                  pl.BlockSpec((tm, LANES), lambda i, j: (j, 0)),
                  pl.BlockSpec((tm, LANES), lambda i, j: (j, 0))],
        out_specs=[pl.BlockSpec((1, tm, wd), tok) for wd in widths],
        out_shape=[jax.ShapeDtypeStruct((b, t, wd), dt) for wd, dt in zip(widths, dtypes)],
        compiler_params=_cparams("parallel", "arbitrary"),
    )(xa, mods, gain, w, q_norm, kv_norm, wq, wkv, cos, sin)


def _mla_kernel(q_ref, k_ref, v_ref, o_ref):
    lane = lax.broadcasted_iota(jnp.int32, (1, LANES), 1)
    v = v_ref[0]
    o0 = _softmax_pv(_dot_nt(q_ref[0, :, :LANES], k_ref[0, :, :LANES]), v)
    o1 = _softmax_pv(_dot_nt(q_ref[0, :, LANES:], k_ref[0, :, LANES:]), v)
    o_ref[0] = jnp.where(lane < MLA_V_DIM, o0, o1).astype(BF16)


def _mla_attn(q, k, v, n_lat, tq):
    b, t, _ = k.shape
    npairs = MLA_HEADS // 2
    return pl.pallas_call(
        _mla_kernel,
        grid=(b, npairs, n_lat // tq),
        in_specs=[pl.BlockSpec((1, tq, 2 * LANES), lambda i, h, j: (i, j, h)),
                  pl.BlockSpec((1, t, 2 * LANES), lambda i, h, j: (i, 0, h)),
                  pl.BlockSpec((1, t, LANES), lambda i, h, j: (i, 0, h))],
        out_specs=pl.BlockSpec((1, tq, LANES), lambda i, h, j: (i, j, h)),
        out_shape=jax.ShapeDtypeStruct((b, n_lat, npairs * LANES), BF16),
        compiler_params=_cparams("parallel", "parallel", "arbitrary"),
    )(q, k, v)


def _retention_kernel(dl_ref, q_ref, k_ref, v_ref, g_ref, o_ref, acc_ref, *, n_lat):
    c = RET_CHUNK
    t = k_ref.shape[1]
    n_chunks = n_lat // c
    n_ctx = (t - n_lat) // c
    hp = pl.program_id(1)
    lane = lax.broadcasted_iota(jnp.int32, (1, LANES), 1)
    pos_r = lax.broadcasted_iota(jnp.int32, (c, 1), 0).astype(F32)
    ii = lax.broadcasted_iota(jnp.int32, (c, c), 0)
    jj = lax.broadcasted_iota(jnp.int32, (c, c), 1)
    rel = (ii - jj).astype(F32)
    dl = dl_ref[...]
    hsel = lax.broadcasted_iota(jnp.int32, (1, RET_HEADS), 1)

    for hh in range(2):
        hmask = (lane >= hh * RET_QK_DIM) & (lane < (hh + 1) * RET_QK_DIM)
        v0 = hh * RET_V_DIM
        for direction in range(2):
            logit = jnp.sum(jnp.where(hsel == 2 * hp + hh, dl[direction:direction + 1], 0.0),
                            axis=-1, keepdims=True)
            lg = jnp.minimum(logit, 0.0) - jnp.log(1.0 + jnp.exp(-jnp.abs(logit)))
            if direction == 0:
                mask = rel >= 0
                d_in = jnp.where(mask, jnp.exp(lg * jnp.where(mask, rel, 0.0)), 0.0)
                d_q = jnp.exp(lg * (pos_r + 1.0))
                d_k = jnp.exp(lg * (c - 1.0 - pos_r))
            else:
                mask = rel < 0
                d_in = jnp.where(mask, jnp.exp(lg * jnp.where(mask, -rel, 0.0)), 0.0)
                d_q = jnp.exp(lg * (c - pos_r))
                d_k = jnp.exp(lg * pos_r)
            d_chunk = jnp.exp(lg * c)

            def kv_update(state, start):
                kb = k_ref[0, pl.ds(start, c), :].astype(F32)
                vb = v_ref[0, pl.ds(start, c), v0:v0 + RET_V_DIM]
                kd = jnp.where(hmask, kb * d_k, 0.0).astype(BF16)
                return d_chunk * state + _dot_tn(kd, vb)

            def ctx_step(i, state):
                ci = i if direction == 0 else n_ctx - 1 - i
                return kv_update(state, pl.multiple_of(n_lat + ci * c, c))

            def lat_step(i, state):
                ci = i if direction == 0 else n_chunks - 1 - i
                start = pl.multiple_of(ci * c, c)
                qb = q_ref[0, pl.ds(start, c), :]
                kb = k_ref[0, pl.ds(start, c), :]
                vb = v_ref[0, pl.ds(start, c), v0:v0 + RET_V_DIM]
                qm = jnp.where(hmask, qb, jnp.zeros_like(qb))
                a = _dot_nt(qm, kb) * d_in
                inner = _dot(a.astype(BF16), vb)
                qd = (qm.astype(F32) * d_q).astype(BF16)
                cross = _dot(qd, state.astype(BF16))
                if direction == 0:
                    acc_ref[pl.ds(start, c), :] = inner + cross
                else:
                    r = _rms(acc_ref[pl.ds(start, c), :] + inner + cross)
                    gate = _silu(g_ref[0, pl.ds(start, c), v0:v0 + RET_V_DIM])
                    o_ref[0, pl.ds(start, c), v0:v0 + RET_V_DIM] = (r * gate).astype(BF16)
                return kv_update(state, start)

            state = lax.fori_loop(0, n_ctx, ctx_step, jnp.zeros((LANES, RET_V_DIM), F32))
            lax.fori_loop(0, n_chunks, lat_step, state)


def _retention(rq, rk, rv, rg, decay_logit, n_lat):
    b, t, _ = rq.shape
    npairs = RET_HEADS // 2
    kern = functools.partial(_retention_kernel, n_lat=n_lat)
    return pl.pallas_call(
        kern,
        grid=(b, npairs),
        in_specs=[pl.BlockSpec(decay_logit.shape, lambda i, h: (0, 0)),
                  pl.BlockSpec((1, n_lat, LANES), lambda i, h: (i, 0, h)),
                  pl.BlockSpec((1, t, LANES), lambda i, h: (i, 0, h)),
                  pl.BlockSpec((1, t, 2 * RET_V_DIM), lambda i, h: (i, 0, h)),
                  pl.BlockSpec((1, n_lat, 2 * RET_V_DIM), lambda i, h: (i, 0, h))],
        out_specs=pl.BlockSpec((1, n_lat, 2 * RET_V_DIM), lambda i, h: (i, 0, h)),
        out_shape=jax.ShapeDtypeStruct((b, n_lat, RET_HEADS * RET_V_DIM), BF16),
        scratch_shapes=[pltpu.VMEM((n_lat, RET_V_DIM), F32)],
        compiler_params=_cparams("parallel", "arbitrary"),
    )(decay_logit, rq, rk, rv, rg)


def _rope_tables(n_lat, n_ctx, rot_dim, lane0):
    t = np.arange(n_lat)
    rows = (t // GRID_W).astype(np.float32)
    cols = (t % GRID_W).astype(np.float32)
    m = rot_dim // 4
    freqs = jnp.asarray(ROPE_BASE, F32) ** (-jnp.arange(m, dtype=F32) / m)
    ang_r = jnp.asarray(rows)[:, None] * freqs
    ang_c = jnp.asarray(cols)[:, None] * freqs
    cos = jnp.concatenate([jnp.cos(ang_r)] * 2 + [jnp.cos(ang_c)] * 2, axis=-1)
    sin = jnp.concatenate([-jnp.sin(ang_r), jnp.sin(ang_r), -jnp.sin(ang_c), jnp.sin(ang_c)], axis=-1)
    reps = (LANES - lane0) // rot_dim if lane0 == 0 else 1
    cos = jnp.concatenate([jnp.ones((n_lat, lane0), F32)] + [cos] * reps
                          + [jnp.ones((n_lat, LANES - lane0 - reps * rot_dim), F32)], axis=-1)
    sin = jnp.concatenate([jnp.zeros((n_lat, lane0), F32)] + [sin] * reps
                          + [jnp.zeros((n_lat, LANES - lane0 - reps * rot_dim), F32)], axis=-1)
    cos = jnp.concatenate([cos, jnp.ones((n_ctx, LANES), F32)], axis=0)
    sin = jnp.concatenate([sin, jnp.zeros((n_ctx, LANES), F32)], axis=0)
    return cos, sin


def _rot_partner(w, rot_dim):
    k, n = w.shape
    q = rot_dim // 4
    return w.reshape(k, n // rot_dim, 2, 2, q)[:, :, :, ::-1, :].reshape(k, n)


def _pad_cols(w, lane0, width=LANES):
    k, n = w.shape
    return jnp.concatenate([jnp.zeros((k, lane0), w.dtype), w, jnp.zeros((k, width - lane0 - n), w.dtype)], axis=-1)


def _even_weights(w_in):
    n_rope = 2 * DIFF_HEADS * 2 * DIFF_HEAD_DIM
    dw = DIFF_HEADS * 2 * DIFF_HEAD_DIM
    nw = NA_HEADS * NA_HEAD_DIM
    scale = jnp.concatenate([jnp.full((dw,), DIFF_HEAD_DIM ** -0.5, F32), jnp.ones((2 * dw,), F32),
                             jnp.full((nw,), NA_HEAD_DIM ** -0.5, F32), jnp.ones((2 * nw,), F32)])
    w = w_in * scale
    return jnp.concatenate([w, _rot_partner(w[:, :n_rope], DIFF_HEAD_DIM)], axis=-1).astype(BF16), n_rope


def _odd_weights(w_in, w_uq, w_ukv):
    c0 = MLA_Q_RANK
    c1 = c0 + MLA_KV_RANK
    c2 = c1 + MLA_ROPE_DIM
    nq = RET_HEADS * RET_QK_DIM
    kr = w_in[:, c1:c2]
    w = jnp.concatenate([w_in[:, :c1],
                         _pad_cols(kr, MLA_NOPE_DIM), _pad_cols(_rot_partner(kr, MLA_ROPE_DIM), MLA_NOPE_DIM),
                         w_in[:, c2:c2 + nq], w_in[:, c2 + nq:c2 + 2 * nq] * (RET_QK_DIM ** -0.5),
                         w_in[:, c2 + 2 * nq:]], axis=-1).astype(BF16)
    r = w_uq.shape[0]
    uq = w_uq.reshape(r, MLA_HEADS, MLA_NOPE_DIM + MLA_ROPE_DIM)
    pad = jnp.zeros((r, MLA_HEADS, LANES - MLA_NOPE_DIM - MLA_ROPE_DIM), F32)
    uq_rot = _rot_partner(uq[:, :, MLA_NOPE_DIM:].reshape(r, -1), MLA_ROPE_DIM).reshape(r, MLA_HEADS, MLA_ROPE_DIM)
    wq = jnp.concatenate([uq, pad], axis=-1).reshape(r, -1)
    wq_rot = jnp.concatenate([jnp.zeros_like(uq[:, :, :MLA_NOPE_DIM]), uq_rot, pad], axis=-1).reshape(r, -1)
    rk = w_ukv.shape[0]
    ukv = w_ukv.reshape(rk, MLA_HEADS, MLA_NOPE_DIM + MLA_V_DIM)
    wk = jnp.concatenate([ukv[:, :, :MLA_NOPE_DIM], jnp.zeros((rk, MLA_HEADS, LANES - MLA_NOPE_DIM), F32)],
                         axis=-1).reshape(rk, -1)
    wv = ukv[:, :, MLA_NOPE_DIM:].reshape(rk, -1)
    return w, jnp.concatenate([wq, wq_rot], axis=-1).astype(BF16), jnp.concatenate([wk, wv], axis=-1).astype(BF16)


def _na_bias(rpb, n_lat):
    pat_of_group, idx_r, idx_c, valid = _na_patterns(n_lat // GRID_W)
    g = NA_GROUP_ROWS
    span = g + NA_WIN_ROWS - 1
    n_pat = idx_r.shape[0]
    col_sel = (idx_c[0, :GRID_W, :GRID_W, None] == np.arange(2 * NA_WIN_COLS - 1)).astype(np.float32)
    row_idx = idx_r.reshape(n_pat, g, GRID_W, span, GRID_W)[:, :, 0, :, 0]
    row_sel = (row_idx[..., None] == np.arange(2 * NA_WIN_ROWS - 1)).astype(np.float32)
    cols = jnp.einsum('hrc,qkc->hrqk', rpb, col_sel, precision=lax.Precision.HIGHEST)
    bias = jnp.einsum('pijr,hrqk->hpiqjk', row_sel, cols, precision=lax.Precision.HIGHEST)
    bias = bias.reshape(rpb.shape[0], n_pat, g * GRID_W, span * GRID_W)
    return jnp.where(valid[None], bias, NEG_BIG), pat_of_group


def _router_weights(w_router):
    return _pad_cols(w_router, 0)


def kernel(x, c, ctx, c_ctx, ada_w, ada_b, norm_mix, norm_ffn, final_norm, even_w_in, even_w_out,
           diff_lambda, diff_subln, na_rpb, odd_w_in, odd_w_out, mla_q_norm, mla_w_uq, mla_kv_norm,
           mla_w_ukv, ret_decay_logit, moe_router, moe_w1, moe_w3, moe_w2):
    b, n_lat, d = x.shape
    n_ctx = ctx.shape[1]
    depth = ada_w.shape[0]
    n_lat_tiles = n_lat // TOK_TILE
    n_tiles = (n_lat + n_ctx) // TOK_TILE

    cond = jnp.concatenate([c, c_ctx[None], jnp.zeros((7, d), F32)], axis=0)
    mods_all = _mods(cond, ada_w, ada_b)[:, :b + 1].reshape(depth, b + 1, 6, d)
    xa = jnp.concatenate([x, ctx], axis=1)

    for l in range(depth):
        mods = mods_all[l]
        need_ctx = l < depth - 1
        i = l // 2
        gain_mix = norm_mix[l][None]
        if l % 2 == 0:
            w, n_rope = _even_weights(even_w_in[i])
            cos, sin = _rope_tables(n_lat, n_ctx, DIFF_HEAD_DIM, 0)
            qkv = _proj_even(xa, mods, gain_mix, w, cos, sin, n_lat_tiles, n_rope, even_w_in.shape[-1])
            lam_init = 0.8 - 0.6 * math.exp(-0.3 * l)
            mix_a = _diff_attn(qkv, diff_lambda[i], diff_subln[i][None], n_lat, lam_init, TOK_TILE)
            bias, offsets = _na_bias(na_rpb[i], n_lat)
            mix_b = _na_attn(qkv, bias, n_lat, offsets, 3 * DIFF_HEADS * 2 * DIFF_HEAD_DIM)
            w_out = even_w_out[i].astype(BF16)
        else:
            w, wq, wkv = _odd_weights(odd_w_in[i], mla_w_uq[i], mla_w_ukv[i])
            cos, sin = _rope_tables(n_lat, n_ctx, MLA_ROPE_DIM, MLA_NOPE_DIM)
            q, k, v, rq, rk, rv, rg = _proj_odd(xa, mods, gain_mix, w, mla_q_norm[i][None], mla_kv_norm[i][None],
                                                wq, wkv, cos, sin, n_lat_tiles)
            mix_a = _mla_attn(q, k, v, n_lat, TOK_TILE)
            mix_b = _retention(rq, rk, rv, rg, ret_decay_logit[i], n_lat)
            w_out = odd_w_out[i].astype(BF16)

        tiles = n_tiles if need_ctx else n_lat_tiles
        x_mid, h2, aff = _proj_out(xa, mix_a, mix_b, w_out, mods, norm_ffn[l][None],
                                   _router_weights(moe_router[l]), tiles, n_lat_tiles)
        segments = ((0, n_lat), (n_lat, n_ctx)) if need_ctx else ((0, n_lat),)
        xa = _moe(x_mid, h2, aff, segments, mods, final_norm[None], n_lat_tiles, l == depth - 1,
                  moe_w1, moe_w3, moe_w2, l)
    return xa
```

```python
import functools
import math

import jax
import jax.numpy as jnp
import numpy as np
from jax import lax
from jax.experimental import pallas as pl
from jax.experimental.pallas import tpu as pltpu

GRID_W = 64
ROPE_BASE = 10000.0
RMS_EPS = 1e-6
DIFF_HEADS = 4
DIFF_HEAD_DIM = 64
NA_HEADS = 8
NA_HEAD_DIM = 64
NA_WIN_ROWS = 8
NA_WIN_COLS = 16
MLA_HEADS = 8
MLA_Q_RANK = 256
MLA_KV_RANK = 128
MLA_NOPE_DIM = 64
MLA_ROPE_DIM = 32
MLA_V_DIM = 64
RET_HEADS = 4
RET_QK_DIM = 64
RET_V_DIM = 128
RET_CHUNK = 128
N_EXPERTS = 16
EC_CAPACITY_FACTOR = 2

LANES = 128
VMEM_LIMIT = 56 * 1024 * 1024
NEG_BIG = -1e30
TOK_TILE = 256
NA_GROUP_ROWS = 4
PAIR_CHUNK = 256
TRANSPOSE_CHUNK = 256
ONES_ROWS = 16
LOG2E = math.log2(math.e)

BF16 = jnp.bfloat16
F32 = jnp.float32


def _cparams(*sem):
    return pltpu.CompilerParams(dimension_semantics=sem, vmem_limit_bytes=VMEM_LIMIT)


def _dot(a, b):
    return jnp.dot(a, b, preferred_element_type=F32)


def _dot_nt(a, b):
    return lax.dot_general(a, b, (((1,), (1,)), ((), ())), preferred_element_type=F32)


def _dot_tn(a, b):
    return lax.dot_general(a, b, (((0,), (0,)), ((), ())), preferred_element_type=F32)


def _rms(x):
    return x * lax.rsqrt(jnp.mean(x * x, axis=-1, keepdims=True) + RMS_EPS)


def _silu(x):
    return x * (1.0 / (1.0 + jnp.exp(-x)))


def _tile_lanes(t, n):
    return jnp.concatenate([t] * n, axis=-1)


def _mods_kernel(c_ref, w_ref, b_ref, o_ref):
    o_ref[0] = _dot(_silu(c_ref[...]), w_ref[0]) + b_ref[0]


def _mods(cond, ada_w, ada_b):
    depth, d, n = ada_w.shape
    r = cond.shape[0]
    tn = 1536
    return pl.pallas_call(
        _mods_kernel,
        grid=(depth, n // tn),
        in_specs=[pl.BlockSpec((r, d), lambda l, j: (0, 0)),
                  pl.BlockSpec((1, d, tn), lambda l, j: (l, 0, j)),
                  pl.BlockSpec((1, 1, tn), lambda l, j: (l, 0, j))],
        out_specs=pl.BlockSpec((1, r, tn), lambda l, j: (l, 0, j)),
        out_shape=jax.ShapeDtypeStruct((depth, r, n), F32),
        compiler_params=_cparams("arbitrary", "arbitrary"),
    )(cond, ada_w, ada_b.reshape(depth, 1, n))


def _normed(x, gain, mods, shift_row, scale_row):
    return _rms(x) * gain * (1.0 + mods[scale_row:scale_row + 1]) + mods[shift_row:shift_row + 1]


def _proj_even_kernel(x_ref, m_ref, g_ref, w_ref, cos_ref, sin_ref, o_ref, *, n_rope, n_out):
    h = _normed(x_ref[0], g_ref[...], m_ref[0], 0, 1)
    acc = _dot(h.astype(BF16), w_ref[...])
    reps = n_rope // LANES
    cos = _tile_lanes(cos_ref[...], reps)
    sin = _tile_lanes(sin_ref[...], reps)
    roped = acc[:, :n_rope] * cos + acc[:, n_out:] * sin
    n_q = n_rope // 2
    o_ref[0, :, :n_q] = (roped[:, :n_q] * LOG2E).astype(BF16)
    o_ref[0, :, n_q:n_rope] = roped[:, n_q:].astype(BF16)
    o_ref[0, :, n_rope:] = acc[:, n_rope:n_out].astype(BF16)


def _proj_even(xa, mods, gain, w, cos, sin, n_lat_tiles, n_rope, n_out):
    b, t, d = xa.shape
    nb = mods.shape[0] - 1
    tm = TOK_TILE
    kern = functools.partial(_proj_even_kernel, n_rope=n_rope, n_out=n_out)
    return pl.pallas_call(
        kern,
        grid=(b, t // tm),
        in_specs=[pl.BlockSpec((1, tm, d), lambda i, j: (i, j, 0)),
                  pl.BlockSpec((1, 6, d), lambda i, j: (jnp.where(j >= n_lat_tiles, nb, i), 0, 0)),
                  pl.BlockSpec((1, d), lambda i, j: (0, 0)),
                  pl.BlockSpec(w.shape, lambda i, j: (0, 0)),
                  pl.BlockSpec((tm, LANES), lambda i, j: (j, 0)),
                  pl.BlockSpec((tm, LANES), lambda i, j: (j, 0))],
        out_specs=pl.BlockSpec((1, tm, n_out), lambda i, j: (i, j, 0)),
        out_shape=jax.ShapeDtypeStruct((b, t, n_out), BF16),
        compiler_params=_cparams("parallel", "arbitrary"),
    )(xa, mods, gain, w, cos, sin)


def _softmax_pv(s, v):
    m = jnp.max(s, axis=-1, keepdims=True)
    p = jnp.exp(s - m)
    l = jnp.sum(p, axis=-1, keepdims=True)
    return _dot(p.astype(BF16), v) / l


def _store_transposed(vt_ref, row0, v_ref, col0, width):
    t = v_ref.shape[1]
    for st in range(0, t, TRANSPOSE_CHUNK):
        blk = v_ref[0, st:st + TRANSPOSE_CHUNK, :].astype(F32).T
        vt_ref[row0:row0 + width, st:st + TRANSPOSE_CHUNK] = blk[col0:col0 + width].astype(BF16)


def _softmax_pv_t(k, q, vt_ext, n_v):
    s = _dot_nt(k, q)
    p = jnp.exp2(s - jnp.max(s, axis=0, keepdims=True)).astype(BF16)
    ot = _dot(vt_ext, p)
    return ot[:n_v] / ot[n_v:n_v + 1]


def _diff_attn_kernel(lam_ref, g_ref, q_ref, k_ref, v_ref, o_ref, vt_ref, *, n_lat, n_lat_blocks, lam_init):
    qi = pl.program_id(2)
    nv = 2 * DIFF_HEAD_DIM

    @pl.when(qi == 0)
    def _():
        _store_transposed(vt_ref, 0, v_ref, 0, nv)
        vt_ref[nv:, :] = jnp.ones((ONES_ROWS, vt_ref.shape[1]), BF16)

    lp = lam_ref[...]
    s1 = jnp.sum(lp[0:1] * lp[1:2], axis=-1, keepdims=True)
    s2 = jnp.sum(lp[2:3] * lp[3:4], axis=-1, keepdims=True)
    lam = jnp.exp(s1) - jnp.exp(s2) + lam_init
    q = q_ref[0]
    lane = lax.broadcasted_iota(jnp.int32, (1, LANES), 1)
    q1 = jnp.where(lane < DIFF_HEAD_DIM, q, jnp.zeros_like(q))
    q2 = jnp.where(lane >= DIFF_HEAD_DIM, q, jnp.zeros_like(q))

    def attend(k, vt):
        ot = _softmax_pv_t(k, q1, vt, nv) - lam * _softmax_pv_t(k, q2, vt, nv)
        r = ot * lax.rsqrt(jnp.mean(ot * ot, axis=0, keepdims=True) + RMS_EPS) * g_ref[...] * (1.0 - lam_init)
        o_ref[0] = r.T.astype(BF16)

    @pl.when(qi < n_lat_blocks)
    def _():
        attend(k_ref[0], vt_ref[...])

    @pl.when(qi >= n_lat_blocks)
    def _():
        attend(k_ref[0, n_lat:, :], vt_ref[:, n_lat:])


def _diff_attn(qkv, lam_params, subln, n_lat, lam_init, tq):
    b, t, _ = qkv.shape
    nh = DIFF_HEADS
    kern = functools.partial(_diff_attn_kernel, n_lat=n_lat, n_lat_blocks=n_lat // tq, lam_init=lam_init)
    return pl.pallas_call(
        kern,
        grid=(b, nh, t // tq),
        in_specs=[pl.BlockSpec(lam_params.shape, lambda i, h, j: (0, 0)),
                  pl.BlockSpec((LANES, 1), lambda i, h, j: (0, 0)),
                  pl.BlockSpec((1, tq, LANES), lambda i, h, j: (i, j, h)),
                  pl.BlockSpec((1, t, LANES), lambda i, h, j: (i, 0, nh + h)),
                  pl.BlockSpec((1, t, LANES), lambda i, h, j: (i, 0, 2 * nh + h))],
        out_specs=pl.BlockSpec((1, tq, LANES), lambda i, h, j: (i, j, h)),
        out_shape=jax.ShapeDtypeStruct((b, t, nh * LANES), BF16),
        scratch_shapes=[pltpu.VMEM((LANES + ONES_ROWS, t), BF16)],
        compiler_params=_cparams("parallel", "parallel", "arbitrary"),
    )(lam_params, subln.reshape(LANES, 1), qkv, qkv, qkv)


def _na_patterns(rows):
    g = NA_GROUP_ROWS
    span = g + NA_WIN_ROWS - 1
    nq, nk = g * GRID_W, span * GRID_W
    qi = np.arange(nq)[:, None]
    ki = np.arange(nk)[None, :]
    qcol, kcol = qi % GRID_W, ki % GRID_W
    cs = np.clip(qcol - NA_WIN_COLS // 2, 0, GRID_W - NA_WIN_COLS)
    col_ok = (kcol >= cs) & (kcol < cs + NA_WIN_COLS)
    idx_c = np.clip(kcol - qcol + NA_WIN_COLS - 1, 0, 2 * NA_WIN_COLS - 2) + 0 * qi
    seen, pat_of_group, idx_r, valid = {}, [], [], []
    for r0 in range(0, rows, g):
        ks = min(max(r0 - NA_WIN_ROWS // 2, 0), rows - span)
        qrow = r0 + qi // GRID_W
        krow = ks + ki // GRID_W
        rs = np.clip(qrow - NA_WIN_ROWS // 2, 0, rows - NA_WIN_ROWS)
        ok = (krow >= rs) & (krow < rs + NA_WIN_ROWS) & col_ok
        ir = np.clip(krow - qrow + NA_WIN_ROWS - 1, 0, 2 * NA_WIN_ROWS - 2) + 0 * kcol
        key = (ok.tobytes(), np.where(ok, ir, 0).tobytes())
        if key not in seen:
            seen[key] = len(idx_r)
            idx_r.append(ir)
            valid.append(ok)
        pat_of_group.append(seen[key])
    n_pat = len(idx_r)
    return tuple(pat_of_group), np.stack(idx_r), np.stack([idx_c] * n_pat), np.stack(valid)


def _na_kernel(q_ref, k_ref, v_ref, bias_ref, o_ref, *, n_lat, pat_of_group):
    g_rows = NA_GROUP_ROWS
    span = g_rows + NA_WIN_ROWS - 1
    rows = n_lat // GRID_W
    nq, nk = g_rows * GRID_W, span * GRID_W
    lane = lax.broadcasted_iota(jnp.int32, (1, LANES), 1)
    head_masks = (lane < NA_HEAD_DIM, lane >= NA_HEAD_DIM)
    kc = k_ref[0, n_lat:, :]
    vc = v_ref[0, n_lat:, :]

    def group(g, carry):
        r0 = g * g_rows
        ks = jnp.clip(r0 - NA_WIN_ROWS // 2, 0, rows - span)
        common = max(set(pat_of_group), key=pat_of_group.count)
        pat = jnp.int32(common)
        for gi, p in enumerate(pat_of_group):
            if p != common:
                pat = jnp.where(g == gi, p, pat)
        qg = q_ref[0, pl.ds(pl.multiple_of(g * nq, nq), nq), :]
        kw = k_ref[0, pl.ds(pl.multiple_of(ks * GRID_W, GRID_W), nk), :]
        vw = v_ref[0, pl.ds(pl.multiple_of(ks * GRID_W, GRID_W), nk), :]
        out = jnp.zeros((nq, LANES), F32)
        for hh in range(2):
            qh = jnp.where(head_masks[hh], qg, jnp.zeros_like(qg))
            s_loc = _dot_nt(qh, kw) + bias_ref[hh, pat]
            s_ctx = _dot_nt(qh, kc)
            m = jnp.maximum(jnp.max(s_loc, axis=-1, keepdims=True), jnp.max(s_ctx, axis=-1, keepdims=True))
            p_loc = jnp.exp(s_loc - m)
            p_ctx = jnp.exp(s_ctx - m)
            l = jnp.sum(p_loc, axis=-1, keepdims=True) + jnp.sum(p_ctx, axis=-1, keepdims=True)
            o = (_dot(p_loc.astype(BF16), vw) + _dot(p_ctx.astype(BF16), vc)) / l
            out = jnp.where(head_masks[hh], o, out)
        o_ref[0, pl.ds(pl.multiple_of(g * nq, nq), nq), :] = out.astype(BF16)
        return carry

    lax.fori_loop(0, rows // g_rows, group, 0)

    qc = q_ref[0, n_lat:, :]
    out = jnp.zeros(qc.shape, F32)
    for hh in range(2):
        qh = jnp.where(head_masks[hh], qc, jnp.zeros_like(qc))
        out = jnp.where(head_masks[hh], _softmax_pv(_dot_nt(qh, kc), vc), out)
    o_ref[0, n_lat:, :] = out.astype(BF16)


def _na_attn(qkv, bias, n_lat, pat_of_group, col0):
    b, t, _ = qkv.shape
    npairs = NA_HEADS // 2
    cb = col0 // LANES
    kern = functools.partial(_na_kernel, n_lat=n_lat, pat_of_group=pat_of_group)
    return pl.pallas_call(
        kern,
        grid=(b, npairs),
        in_specs=[pl.BlockSpec((1, t, LANES), lambda i, h: (i, 0, cb + h)),
                  pl.BlockSpec((1, t, LANES), lambda i, h: (i, 0, cb + npairs + h)),
                  pl.BlockSpec((1, t, LANES), lambda i, h: (i, 0, cb + 2 * npairs + h)),
                  pl.BlockSpec((2,) + bias.shape[1:], lambda i, h: (h, 0, 0, 0))],
        out_specs=pl.BlockSpec((1, t, LANES), lambda i, h: (i, 0, h)),
        out_shape=jax.ShapeDtypeStruct((b, t, npairs * LANES), BF16),
        compiler_params=_cparams("parallel", "arbitrary"),
    )(qkv, qkv, qkv, bias)


def _proj_out_kernel(x_ref, a_ref, b_ref, wa_ref, wb_ref, m_ref, g_ref, wr_ref, xo_ref, h_ref, aff_ref):
    mods = m_ref[0]
    y = _dot(a_ref[0], wa_ref[...]) + _dot(b_ref[0], wb_ref[...])
    x = x_ref[0] + mods[2:3] * y
    xo_ref[0] = x
    h2 = _normed(x, g_ref[...], mods, 3, 4)
    h_hi = h2.astype(BF16)
    h_ref[0] = h_hi
    h_lo = (h2 - h_hi.astype(F32)).astype(BF16)
    wr = wr_ref[...]
    w_hi = wr.astype(BF16)
    w_lo = (wr - w_hi.astype(F32)).astype(BF16)
    logits = _dot(h_hi, w_hi) + (_dot(h_hi, w_lo) + _dot(h_lo, w_hi))
    lane = lax.broadcasted_iota(jnp.int32, logits.shape, 1)
    logits = jnp.where(lane < N_EXPERTS, logits, NEG_BIG)
    e = jnp.exp(logits - jnp.max(logits, axis=-1, keepdims=True))
    aff_ref[0] = e / jnp.sum(e, axis=-1, keepdims=True)


def _proj_out(xa, a, bb, w_out, mods, gain, w_router, n_tiles, n_lat_tiles):
    b, t, d = xa.shape
    nb = mods.shape[0] - 1
    tm = TOK_TILE
    wa, wb = w_out[:a.shape[-1]], w_out[a.shape[-1]:]
    tok = lambda i, j: (i, j, 0)
    const = lambda i, j: (0, 0)
    return pl.pallas_call(
        _proj_out_kernel,
        grid=(b, n_tiles),
        in_specs=[pl.BlockSpec((1, tm, d), tok),
                  pl.BlockSpec((1, tm, a.shape[-1]), tok),
                  pl.BlockSpec((1, tm, bb.shape[-1]), tok),
                  pl.BlockSpec(wa.shape, const),
                  pl.BlockSpec(wb.shape, const),
                  pl.BlockSpec((1, 6, d), lambda i, j: (jnp.where(j >= n_lat_tiles, nb, i), 0, 0)),
                  pl.BlockSpec((1, d), const),
                  pl.BlockSpec(w_router.shape, const)],
        out_specs=[pl.BlockSpec((1, tm, d), tok),
                   pl.BlockSpec((1, tm, d), tok),
                   pl.BlockSpec((1, tm, LANES), tok)],
        out_shape=[jax.ShapeDtypeStruct((b, n_tiles * tm, d), F32),
                   jax.ShapeDtypeStruct((b, n_tiles * tm, d), BF16),
                   jax.ShapeDtypeStruct((b, n_tiles * tm, LANES), F32)],
        compiler_params=_cparams("parallel", "arbitrary"),
    )(xa, a, bb, wa, wb, mods, gain, w_router)


def _ffn_kernel(x_ref, gate_ref, w1_ref, w3_ref, w2_ref, o_ref, w1b, w3b, w2b):
    @pl.when(pl.program_id(1) == 0)
    def _():
        w1b[...] = w1_ref[0].astype(BF16)
        w3b[...] = w3_ref[0].astype(BF16)
        w2b[...] = w2_ref[0].astype(BF16)

    x = x_ref[0]
    hid = _silu(_dot(x, w1b[...])) * _dot(x, w3b[...])
    o_ref[0] = (_dot(hid.astype(BF16), w2b[...]) * gate_ref[0]).astype(BF16)


def _expert_ffn(xe, gate, w1, w3, w2, layer):
    e, m, d = xe.shape
    f = w1.shape[-1]
    tm = next(c for c in (512, 256, 128, 64, 32) if m % c == 0)
    wspec = lambda r, c: pl.BlockSpec((None, 1, r, c), lambda i, j: (layer, i, 0, 0))
    return pl.pallas_call(
        _ffn_kernel,
        grid=(e, m // tm),
        in_specs=[pl.BlockSpec((1, tm, d), lambda i, j: (i, j, 0)),
                  pl.BlockSpec((1, tm, 1), lambda i, j: (i, j, 0)),
                  wspec(d, f), wspec(d, f), wspec(f, d)],
        out_specs=pl.BlockSpec((1, tm, d), lambda i, j: (i, j, 0)),
        out_shape=jax.ShapeDtypeStruct((e, m, d), BF16),
        scratch_shapes=[pltpu.VMEM((d, f), BF16), pltpu.VMEM((d, f), BF16), pltpu.VMEM((f, d), BF16)],
        compiler_params=_cparams("parallel", "arbitrary"),
    )(xe, gate, w1, w3, w2)


def _route(aff, segments):
    b, r, _ = aff.shape
    e = N_EXPERTS
    gates, flats, toks = [], [], []
    for row0, n in segments:
        cap = (EC_CAPACITY_FACTOR * n) // e
        gate, idx = lax.top_k(jnp.swapaxes(aff[:, row0:row0 + n, :e], 1, 2), cap)
        tok = idx + row0
        gates.append(jnp.swapaxes(gate, 0, 1).reshape(e, b * cap))
        flats.append(jnp.swapaxes(tok + jnp.arange(b, dtype=idx.dtype)[:, None, None] * r, 0, 1).reshape(e, b * cap))
        toks.append(tok)
    return jnp.concatenate(gates, axis=1)[..., None], jnp.concatenate(flats, axis=1), toks


def _combine_kernel(lo_ref, x_ref, tok_ref, y_ref, m_ref, g_ref, o_ref, *, final):
    i, j = pl.program_id(0), pl.program_id(1)
    tm = x_ref.shape[1]
    n_tiles = pl.num_programs(1)
    lo = lo_ref[i * (n_tiles + 1) + j]
    hi = lo_ref[i * (n_tiles + 1) + j + 1]
    rows = j * tm + lax.broadcasted_iota(jnp.int32, (tm, 1), 0)
    o_ref[0] = jnp.zeros(o_ref.shape[1:], F32)

    def chunk(c, carry):
        sel = (tok_ref[0, pl.ds(c, 1), :] == rows).astype(BF16)
        o_ref[0] += _dot(sel, y_ref[0, pl.ds(pl.multiple_of(c * PAIR_CHUNK, PAIR_CHUNK), PAIR_CHUNK), :])
        return carry

    lax.fori_loop(lo // PAIR_CHUNK, (hi + PAIR_CHUNK - 1) // PAIR_CHUNK, chunk, 0)
    x = x_ref[0] + m_ref[0][5:6] * o_ref[0]
    if final:
        x = _rms(x) * g_ref[...]
    o_ref[0] = x


def _combine(x, tok_sorted, y_sorted, lo, mods, gain, n_lat_tiles, final):
    b, t, d = x.shape
    p = tok_sorted.shape[1]
    nb = mods.shape[0] - 1
    tm = TOK_TILE
    tok = lambda i, j, lo_ref: (i, j, 0)
    whole = lambda i, j, lo_ref: (i, 0, 0)
    grid_spec = pltpu.PrefetchScalarGridSpec(
        num_scalar_prefetch=1,
        grid=(b, t // tm),
        in_specs=[pl.BlockSpec((1, tm, d), tok),
                  pl.BlockSpec((1, p // PAIR_CHUNK, PAIR_CHUNK), whole),
                  pl.BlockSpec((1, p, d), whole, pipeline_mode=pl.Buffered(1)),
                  pl.BlockSpec((1, 6, d), lambda i, j, lo_ref: (jnp.where(j >= n_lat_tiles, nb, i), 0, 0)),
                  pl.BlockSpec((1, d), lambda i, j, lo_ref: (0, 0))],
        out_specs=pl.BlockSpec((1, tm, d), tok))
    return pl.pallas_call(
        functools.partial(_combine_kernel, final=final),
        grid_spec=grid_spec,
        out_shape=jax.ShapeDtypeStruct((b, t, d), F32),
        compiler_params=_cparams("parallel", "arbitrary"),
    )(lo, x, tok_sorted.reshape(b, p // PAIR_CHUNK, PAIR_CHUNK), y_sorted, mods, gain)


def _moe(x_mid, h2, aff, segments, mods, gain, n_lat_tiles, final, w1, w3, w2, layer):
    b, r, d = h2.shape
    gate, flat, toks = _route(aff, segments)
    e, m = flat.shape
    xe = h2.reshape(b * r, d).at[flat].get(mode="promise_in_bounds")
    ye = _expert_ffn(xe, gate, w1, w3, w2, layer)
    srcs, off = [], 0
    for tok in toks:
        cap = tok.shape[-1]
        src = (jnp.arange(e, dtype=jnp.int32)[None, :, None] * m + off
               + jnp.arange(b, dtype=jnp.int32)[:, None, None] * cap + jnp.arange(cap, dtype=jnp.int32))
        srcs.append(src.reshape(b, e * cap))
        off += b * cap
    tok_all = jnp.concatenate([tok.reshape(b, -1) for tok in toks], axis=1)
    tok_sorted, src_sorted = lax.sort_key_val(tok_all, jnp.concatenate(srcs, axis=1), dimension=1)
    y_sorted = ye.reshape(e * m, d).at[src_sorted].get(mode="promise_in_bounds")
    starts = jnp.arange(r // TOK_TILE + 1, dtype=jnp.int32) * TOK_TILE
    lo = jnp.sum((tok_sorted[:, None, :] < starts[None, :, None]).astype(jnp.int32), axis=-1)
    return _combine(x_mid, tok_sorted, y_sorted, lo.reshape(-1), mods, gain, n_lat_tiles, final)


def _proj_odd_kernel(x_ref, m_ref, g_ref, w_ref, qn_ref, kvn_ref, wq_ref, wkv_ref, cos_ref, sin_ref,
                     q_ref, k_ref, v_ref, rq_ref, rk_ref, rv_ref, rg_ref, *, q_scale):
    h = _normed(x_ref[0], g_ref[...], m_ref[0], 0, 1)
    acc = _dot(h.astype(BF16), w_ref[...])
    cos, sin = cos_ref[...], sin_ref[...]
    c0 = MLA_Q_RANK
    c1 = c0 + MLA_KV_RANK
    cq = acc[:, :c0]
    ckv = acc[:, c0:c1]
    kr = acc[:, c1:c1 + LANES] * cos + acc[:, c1 + LANES:c1 + 2 * LANES] * sin
    c2 = c1 + 2 * LANES
    nq = RET_HEADS * RET_QK_DIM
    nv = RET_HEADS * RET_V_DIM
    rq_ref[0] = acc[:, c2:c2 + nq].astype(BF16)
    rk_ref[0] = acc[:, c2 + nq:c2 + 2 * nq].astype(BF16)
    rv_ref[0] = acc[:, c2 + 2 * nq:c2 + 2 * nq + nv].astype(BF16)
    rg_ref[0] = acc[:, c2 + 2 * nq + nv:]

    nqk = MLA_HEADS * LANES
    qq = _dot((_rms(cq) * qn_ref[...]).astype(BF16), wq_ref[...])
    q = qq[:, :nqk] * _tile_lanes(cos, MLA_HEADS) + qq[:, nqk:] * _tile_lanes(sin, MLA_HEADS)
    q_ref[0] = (q * q_scale).astype(BF16)
    kv = _dot((_rms(ckv) * kvn_ref[...]).astype(BF16), wkv_ref[...])
    k_ref[0] = (kv[:, :nqk] + _tile_lanes(kr, MLA_HEADS)).astype(BF16)
    v_ref[0] = kv[:, nqk:].astype(BF16)


def _proj_odd(xa, mods, gain, w, q_norm, kv_norm, wq, wkv, cos, sin, n_lat_tiles):
    b, t, d = xa.shape
    nb = mods.shape[0] - 1
    tm = TOK_TILE
    tok = lambda i, j: (i, j, 0)
    const = lambda i, j: (0, 0)
    widths = (MLA_HEADS * LANES, MLA_HEADS * LANES, MLA_HEADS * MLA_V_DIM,
              RET_HEADS * RET_QK_DIM, RET_HEADS * RET_QK_DIM, RET_HEADS * RET_V_DIM, RET_HEADS * RET_V_DIM)
    dtypes = (BF16,) * 6 + (F32,)
    kern = functools.partial(_proj_odd_kernel, q_scale=float((MLA_NOPE_DIM + MLA_ROPE_DIM) ** -0.5) * LOG2E)
    return pl.pallas_call(
        kern,
        grid=(b, t // tm),
        in_specs=[pl.BlockSpec((1, tm, d), tok),
                  pl.BlockSpec((1, 6, d), lambda i, j: (jnp.where(j >= n_lat_tiles, nb, i), 0, 0)),
                  pl.BlockSpec((1, d), const),
                  pl.BlockSpec(w.shape, const),
                  pl.BlockSpec(q_norm.shape, const),
                  pl.BlockSpec(kv_norm.shape, const),
                  pl.BlockSpec(wq.shape, const),
                  pl.BlockSpec(wkv.shape, const),
                  pl.BlockSpec((tm, LANES), lambda i, j: (j, 0)),
                  pl.BlockSpec((tm, LANES), lambda i, j: (j, 0))],
        out_specs=[pl.BlockSpec((1, tm, wd), tok) for wd in widths],
        out_shape=[jax.ShapeDtypeStruct((b, t, wd), dt) for wd, dt in zip(widths, dtypes)],
        compiler_params=_cparams("parallel", "arbitrary"),
    )(xa, mods, gain, w, q_norm, kv_norm, wq, wkv, cos, sin)


def _mla_kernel(q_ref, k_ref, v_ref, o_ref, vt_ref):
    nv = MLA_V_DIM
    blk = nv + ONES_ROWS

    @pl.when(pl.program_id(2) == 0)
    def _():
        for hh in range(2):
            _store_transposed(vt_ref, hh * blk, v_ref, hh * nv, nv)
            vt_ref[hh * blk + nv:(hh + 1) * blk, :] = jnp.ones((ONES_ROWS, vt_ref.shape[1]), BF16)

    outs = [_softmax_pv_t(k_ref[0, :, hh * LANES:(hh + 1) * LANES], q_ref[0, :, hh * LANES:(hh + 1) * LANES],
                          vt_ref[hh * blk:(hh + 1) * blk, :], nv) for hh in range(2)]
    o_ref[0] = jnp.concatenate(outs, axis=0).T.astype(BF16)


def _mla_attn(q, k, v, n_lat, tq):
    b, t, _ = k.shape
    npairs = MLA_HEADS // 2
    return pl.pallas_call(
        _mla_kernel,
        grid=(b, npairs, n_lat // tq),
        in_specs=[pl.BlockSpec((1, tq, 2 * LANES), lambda i, h, j: (i, j, h)),
                  pl.BlockSpec((1, t, 2 * LANES), lambda i, h, j: (i, 0, h)),
                  pl.BlockSpec((1, t, LANES), lambda i, h, j: (i, 0, h))],
        out_specs=pl.BlockSpec((1, tq, LANES), lambda i, h, j: (i, j, h)),
        out_shape=jax.ShapeDtypeStruct((b, n_lat, npairs * LANES), BF16),
        scratch_shapes=[pltpu.VMEM((2 * (MLA_V_DIM + ONES_ROWS), t), BF16)],
        compiler_params=_cparams("parallel", "parallel", "arbitrary"),
    )(q, k, v)


def _retention_kernel(dl_ref, q_ref, k_ref, v_ref, g_ref, o_ref, acc_ref, *, n_lat):
    c = RET_CHUNK
    t = k_ref.shape[1]
    n_chunks = n_lat // c
    n_ctx = (t - n_lat) // c
    hp = pl.program_id(1)
    lane = lax.broadcasted_iota(jnp.int32, (1, LANES), 1)
    pos_r = lax.broadcasted_iota(jnp.int32, (c, 1), 0).astype(F32)
    ii = lax.broadcasted_iota(jnp.int32, (c, c), 0)
    jj = lax.broadcasted_iota(jnp.int32, (c, c), 1)
    rel = (ii - jj).astype(F32)
    dl = dl_ref[...]
    hsel = lax.broadcasted_iota(jnp.int32, (1, RET_HEADS), 1)

    for hh in range(2):
        hmask = (lane >= hh * RET_QK_DIM) & (lane < (hh + 1) * RET_QK_DIM)
        v0 = hh * RET_V_DIM
        for direction in range(2):
            logit = jnp.sum(jnp.where(hsel == 2 * hp + hh, dl[direction:direction + 1], 0.0),
                            axis=-1, keepdims=True)
            lg = jnp.minimum(logit, 0.0) - jnp.log(1.0 + jnp.exp(-jnp.abs(logit)))
            if direction == 0:
                mask = rel >= 0
                d_in = jnp.where(mask, jnp.exp(lg * jnp.where(mask, rel, 0.0)), 0.0)
                d_q = jnp.exp(lg * (pos_r + 1.0))
                d_k = jnp.exp(lg * (c - 1.0 - pos_r))
            else:
                mask = rel < 0
                d_in = jnp.where(mask, jnp.exp(lg * jnp.where(mask, -rel, 0.0)), 0.0)
                d_q = jnp.exp(lg * (c - pos_r))
                d_k = jnp.exp(lg * pos_r)
            d_chunk = jnp.exp(lg * c)

            def kv_update(state, start):
                kb = k_ref[0, pl.ds(start, c), :].astype(F32)
                vb = v_ref[0, pl.ds(start, c), v0:v0 + RET_V_DIM]
                kd = jnp.where(hmask, kb * d_k, 0.0).astype(BF16)
                return d_chunk * state + _dot_tn(kd, vb)

            def ctx_step(i, state):
                ci = i if direction == 0 else n_ctx - 1 - i
                return kv_update(state, pl.multiple_of(n_lat + ci * c, c))

            def lat_step(i, state):
                ci = i if direction == 0 else n_chunks - 1 - i
                start = pl.multiple_of(ci * c, c)
                qb = q_ref[0, pl.ds(start, c), :]
                kb = k_ref[0, pl.ds(start, c), :]
                vb = v_ref[0, pl.ds(start, c), v0:v0 + RET_V_DIM]
                qm = jnp.where(hmask, qb, jnp.zeros_like(qb))
                a = _dot_nt(qm, kb) * d_in
                inner = _dot(a.astype(BF16), vb)
                qd = (qm.astype(F32) * d_q).astype(BF16)
                cross = _dot(qd, state.astype(BF16))
                if direction == 0:
                    acc_ref[pl.ds(start, c), :] = inner + cross
                else:
                    r = _rms(acc_ref[pl.ds(start, c), :] + inner + cross)
                    gate = _silu(g_ref[0, pl.ds(start, c), v0:v0 + RET_V_DIM])
                    o_ref[0, pl.ds(start, c), v0:v0 + RET_V_DIM] = (r * gate).astype(BF16)
                return kv_update(state, start)

            state = lax.fori_loop(0, n_ctx, ctx_step, jnp.zeros((LANES, RET_V_DIM), F32))
            lax.fori_loop(0, n_chunks, lat_step, state)


def _retention(rq, rk, rv, rg, decay_logit, n_lat):
    b, t, _ = rq.shape
    npairs = RET_HEADS // 2
    kern = functools.partial(_retention_kernel, n_lat=n_lat)
    return pl.pallas_call(
        kern,
        grid=(b, npairs),
        in_specs=[pl.BlockSpec(decay_logit.shape, lambda i, h: (0, 0)),
                  pl.BlockSpec((1, n_lat, LANES), lambda i, h: (i, 0, h)),
                  pl.BlockSpec((1, t, LANES), lambda i, h: (i, 0, h)),
                  pl.BlockSpec((1, t, 2 * RET_V_DIM), lambda i, h: (i, 0, h)),
                  pl.BlockSpec((1, n_lat, 2 * RET_V_DIM), lambda i, h: (i, 0, h))],
        out_specs=pl.BlockSpec((1, n_lat, 2 * RET_V_DIM), lambda i, h: (i, 0, h)),
        out_shape=jax.ShapeDtypeStruct((b, n_lat, RET_HEADS * RET_V_DIM), BF16),
        scratch_shapes=[pltpu.VMEM((n_lat, RET_V_DIM), F32)],
        compiler_params=_cparams("parallel", "arbitrary"),
    )(decay_logit, rq, rk, rv, rg)


def _rope_tables(n_lat, n_ctx, rot_dim, lane0):
    t = np.arange(n_lat)
    rows = (t // GRID_W).astype(np.float32)
    cols = (t % GRID_W).astype(np.float32)
    m = rot_dim // 4
    freqs = jnp.asarray(ROPE_BASE, F32) ** (-jnp.arange(m, dtype=F32) / m)
    ang_r = jnp.asarray(rows)[:, None] * freqs
    ang_c = jnp.asarray(cols)[:, None] * freqs
    cos = jnp.concatenate([jnp.cos(ang_r)] * 2 + [jnp.cos(ang_c)] * 2, axis=-1)
    sin = jnp.concatenate([-jnp.sin(ang_r), jnp.sin(ang_r), -jnp.sin(ang_c), jnp.sin(ang_c)], axis=-1)
    reps = (LANES - lane0) // rot_dim if lane0 == 0 else 1
    cos = jnp.concatenate([jnp.ones((n_lat, lane0), F32)] + [cos] * reps
                          + [jnp.ones((n_lat, LANES - lane0 - reps * rot_dim), F32)], axis=-1)
    sin = jnp.concatenate([jnp.zeros((n_lat, lane0), F32)] + [sin] * reps
                          + [jnp.zeros((n_lat, LANES - lane0 - reps * rot_dim), F32)], axis=-1)
    cos = jnp.concatenate([cos, jnp.ones((n_ctx, LANES), F32)], axis=0)
    sin = jnp.concatenate([sin, jnp.zeros((n_ctx, LANES), F32)], axis=0)
    return cos, sin


def _rot_partner(w, rot_dim):
    k, n = w.shape
    q = rot_dim // 4
    return w.reshape(k, n // rot_dim, 2, 2, q)[:, :, :, ::-1, :].reshape(k, n)


def _pad_cols(w, lane0, width=LANES):
    k, n = w.shape
    return jnp.concatenate([jnp.zeros((k, lane0), w.dtype), w, jnp.zeros((k, width - lane0 - n), w.dtype)], axis=-1)


def _even_weights(w_in):
    n_rope = 2 * DIFF_HEADS * 2 * DIFF_HEAD_DIM
    dw = DIFF_HEADS * 2 * DIFF_HEAD_DIM
    nw = NA_HEADS * NA_HEAD_DIM
    scale = jnp.concatenate([jnp.full((dw,), DIFF_HEAD_DIM ** -0.5, F32), jnp.ones((2 * dw,), F32),
                             jnp.full((nw,), NA_HEAD_DIM ** -0.5, F32), jnp.ones((2 * nw,), F32)])
    w = w_in * scale
    return jnp.concatenate([w, _rot_partner(w[:, :n_rope], DIFF_HEAD_DIM)], axis=-1).astype(BF16), n_rope


def _odd_weights(w_in, w_uq, w_ukv):
    c0 = MLA_Q_RANK
    c1 = c0 + MLA_KV_RANK
    c2 = c1 + MLA_ROPE_DIM
    nq = RET_HEADS * RET_QK_DIM
    kr = w_in[:, c1:c2]
    w = jnp.concatenate([w_in[:, :c1],
                         _pad_cols(kr, MLA_NOPE_DIM), _pad_cols(_rot_partner(kr, MLA_ROPE_DIM), MLA_NOPE_DIM),
                         w_in[:, c2:c2 + nq], w_in[:, c2 + nq:c2 + 2 * nq] * (RET_QK_DIM ** -0.5),
                         w_in[:, c2 + 2 * nq:]], axis=-1).astype(BF16)
    r = w_uq.shape[0]
    uq = w_uq.reshape(r, MLA_HEADS, MLA_NOPE_DIM + MLA_ROPE_DIM)
    pad = jnp.zeros((r, MLA_HEADS, LANES - MLA_NOPE_DIM - MLA_ROPE_DIM), F32)
    uq_rot = _rot_partner(uq[:, :, MLA_NOPE_DIM:].reshape(r, -1), MLA_ROPE_DIM).reshape(r, MLA_HEADS, MLA_ROPE_DIM)
    wq = jnp.concatenate([uq, pad], axis=-1).reshape(r, -1)
    wq_rot = jnp.concatenate([jnp.zeros_like(uq[:, :, :MLA_NOPE_DIM]), uq_rot, pad], axis=-1).reshape(r, -1)
    rk = w_ukv.shape[0]
    ukv = w_ukv.reshape(rk, MLA_HEADS, MLA_NOPE_DIM + MLA_V_DIM)
    wk = jnp.concatenate([ukv[:, :, :MLA_NOPE_DIM], jnp.zeros((rk, MLA_HEADS, LANES - MLA_NOPE_DIM), F32)],
                         axis=-1).reshape(rk, -1)
    wv = ukv[:, :, MLA_NOPE_DIM:].reshape(rk, -1)
    return w, jnp.concatenate([wq, wq_rot], axis=-1).astype(BF16), jnp.concatenate([wk, wv], axis=-1).astype(BF16)


def _na_bias(rpb, n_lat):
    pat_of_group, idx_r, idx_c, valid = _na_patterns(n_lat // GRID_W)
    g = NA_GROUP_ROWS
    span = g + NA_WIN_ROWS - 1
    n_pat = idx_r.shape[0]
    col_sel = (idx_c[0, :GRID_W, :GRID_W, None] == np.arange(2 * NA_WIN_COLS - 1)).astype(np.float32)
    row_idx = idx_r.reshape(n_pat, g, GRID_W, span, GRID_W)[:, :, 0, :, 0]
    row_sel = (row_idx[..., None] == np.arange(2 * NA_WIN_ROWS - 1)).astype(np.float32)
    cols = jnp.einsum('hrc,qkc->hrqk', rpb, col_sel, precision=lax.Precision.HIGHEST)
    bias = jnp.einsum('pijr,hrqk->hpiqjk', row_sel, cols, precision=lax.Precision.HIGHEST)
    bias = bias.reshape(rpb.shape[0], n_pat, g * GRID_W, span * GRID_W)
    return jnp.where(valid[None], bias, NEG_BIG), pat_of_group


def _router_weights(w_router):
    return _pad_cols(w_router, 0)


def kernel(x, c, ctx, c_ctx, ada_w, ada_b, norm_mix, norm_ffn, final_norm, even_w_in, even_w_out,
           diff_lambda, diff_subln, na_rpb, odd_w_in, odd_w_out, mla_q_norm, mla_w_uq, mla_kv_norm,
           mla_w_ukv, ret_decay_logit, moe_router, moe_w1, moe_w3, moe_w2):
    b, n_lat, d = x.shape
    n_ctx = ctx.shape[1]
    depth = ada_w.shape[0]
    n_lat_tiles = n_lat // TOK_TILE
    n_tiles = (n_lat + n_ctx) // TOK_TILE

    cond = jnp.concatenate([c, c_ctx[None], jnp.zeros((7, d), F32)], axis=0)
    mods_all = _mods(cond, ada_w, ada_b)[:, :b + 1].reshape(depth, b + 1, 6, d)
    xa = jnp.concatenate([x, ctx], axis=1)

    for l in range(depth):
        mods = mods_all[l]
        need_ctx = l < depth - 1
        i = l // 2
        gain_mix = norm_mix[l][None]
        if l % 2 == 0:
            w, n_rope = _even_weights(even_w_in[i])
            cos, sin = _rope_tables(n_lat, n_ctx, DIFF_HEAD_DIM, 0)
            qkv = _proj_even(xa, mods, gain_mix, w, cos, sin, n_lat_tiles, n_rope, even_w_in.shape[-1])
            lam_init = 0.8 - 0.6 * math.exp(-0.3 * l)
            mix_a = _diff_attn(qkv, diff_lambda[i], diff_subln[i][None], n_lat, lam_init, TOK_TILE)
            bias, offsets = _na_bias(na_rpb[i], n_lat)
            mix_b = _na_attn(qkv, bias, n_lat, offsets, 3 * DIFF_HEADS * 2 * DIFF_HEAD_DIM)
            w_out = even_w_out[i].astype(BF16)
        else:
            w, wq, wkv = _odd_weights(odd_w_in[i], mla_w_uq[i], mla_w_ukv[i])
            cos, sin = _rope_tables(n_lat, n_ctx, MLA_ROPE_DIM, MLA_NOPE_DIM)
            q, k, v, rq, rk, rv, rg = _proj_odd(xa, mods, gain_mix, w, mla_q_norm[i][None], mla_kv_norm[i][None],
                                                wq, wkv, cos, sin, n_lat_tiles)
            mix_a = _mla_attn(q, k, v, n_lat, TOK_TILE)
            mix_b = _retention(rq, rk, rv, rg, ret_decay_logit[i], n_lat)
            w_out = odd_w_out[i].astype(BF16)

        tiles = n_tiles if need_ctx else n_lat_tiles
        x_mid, h2, aff = _proj_out(xa, mix_a, mix_b, w_out, mods, norm_ffn[l][None],
                                   _router_weights(moe_router[l]), tiles, n_lat_tiles)
        segments = ((0, n_lat), (n_lat, n_ctx)) if need_ctx else ((0, n_lat),)
        xa = _moe(x_mid, h2, aff, segments, mods, final_norm[None], n_lat_tiles, l == depth - 1,
                  moe_w1, moe_w3, moe_w2, l)
    return xa
```

```python
import functools
import math

import jax
import jax.numpy as jnp
import numpy as np
from jax import lax
from jax.experimental import pallas as pl
from jax.experimental.pallas import tpu as pltpu

GRID_W = 64
ROPE_BASE = 10000.0
RMS_EPS = 1e-6
DIFF_HEADS = 4
DIFF_HEAD_DIM = 64
NA_HEADS = 8
NA_HEAD_DIM = 64
NA_WIN_ROWS = 8
NA_WIN_COLS = 16
MLA_HEADS = 8
MLA_Q_RANK = 256
MLA_KV_RANK = 128
MLA_NOPE_DIM = 64
MLA_ROPE_DIM = 32
MLA_V_DIM = 64
RET_HEADS = 4
RET_QK_DIM = 64
RET_V_DIM = 128
RET_CHUNK = 128
N_EXPERTS = 16
EC_CAPACITY_FACTOR = 2

LANES = 128
VMEM_LIMIT = 56 * 1024 * 1024
NEG_BIG = -1e30
TOK_TILE = 256
NA_GROUP_ROWS = 4
PAIR_CHUNK = 256
LOG2E = math.log2(math.e)

BF16 = jnp.bfloat16
F32 = jnp.float32


def _cparams(*sem):
    return pltpu.CompilerParams(dimension_semantics=sem, vmem_limit_bytes=VMEM_LIMIT)


def _dot(a, b):
    return jnp.dot(a, b, preferred_element_type=F32)


def _dot_nt(a, b):
    return lax.dot_general(a, b, (((1,), (1,)), ((), ())), preferred_element_type=F32)


def _dot_tn(a, b):
    return lax.dot_general(a, b, (((0,), (0,)), ((), ())), preferred_element_type=F32)


def _rms(x):
    return x * lax.rsqrt(jnp.mean(x * x, axis=-1, keepdims=True) + RMS_EPS)


def _silu(x):
    return x * (1.0 / (1.0 + jnp.exp(-x)))


def _tile_lanes(t, n):
    return jnp.concatenate([t] * n, axis=-1)


def _mods_kernel(c_ref, w_ref, b_ref, o_ref):
    o_ref[0] = _dot(_silu(c_ref[...]), w_ref[0]) + b_ref[0]


def _mods(cond, ada_w, ada_b):
    depth, d, n = ada_w.shape
    r = cond.shape[0]
    tn = 1536
    return pl.pallas_call(
        _mods_kernel,
        grid=(depth, n // tn),
        in_specs=[pl.BlockSpec((r, d), lambda l, j: (0, 0)),
                  pl.BlockSpec((1, d, tn), lambda l, j: (l, 0, j)),
                  pl.BlockSpec((1, 1, tn), lambda l, j: (l, 0, j))],
        out_specs=pl.BlockSpec((1, r, tn), lambda l, j: (l, 0, j)),
        out_shape=jax.ShapeDtypeStruct((depth, r, n), F32),
        compiler_params=_cparams("arbitrary", "arbitrary"),
    )(cond, ada_w, ada_b.reshape(depth, 1, n))


def _normed(x, gain, mods, shift_row, scale_row):
    return _rms(x) * gain * (1.0 + mods[scale_row:scale_row + 1]) + mods[shift_row:shift_row + 1]


def _proj_even_kernel(x_ref, m_ref, g_ref, w_ref, cos_ref, sin_ref, o_ref, *, n_rope, n_out):
    na_q0 = n_rope + n_rope // 2
    na_q1 = na_q0 + NA_HEADS * NA_HEAD_DIM
    h = _normed(x_ref[0], g_ref[...], m_ref[0], 0, 1)
    acc = _dot(h.astype(BF16), w_ref[...])
    reps = n_rope // LANES
    cos = _tile_lanes(cos_ref[...], reps)
    sin = _tile_lanes(sin_ref[...], reps)
    roped = acc[:, :n_rope] * cos + acc[:, n_out:] * sin
    n_q = n_rope // 2
    o_ref[0, :, :n_q] = (roped[:, :n_q] * LOG2E).astype(BF16)
    o_ref[0, :, n_q:n_rope] = roped[:, n_q:].astype(BF16)
    o_ref[0, :, n_rope:na_q0] = acc[:, n_rope:na_q0].astype(BF16)
    o_ref[0, :, na_q0:na_q1] = (acc[:, na_q0:na_q1] * LOG2E).astype(BF16)
    o_ref[0, :, na_q1:] = acc[:, na_q1:n_out].astype(BF16)


def _proj_even(xa, mods, gain, w, cos, sin, n_lat_tiles, n_rope, n_out):
    b, t, d = xa.shape
    nb = mods.shape[0] - 1
    tm = TOK_TILE
    kern = functools.partial(_proj_even_kernel, n_rope=n_rope, n_out=n_out)
    return pl.pallas_call(
        kern,
        grid=(b, t // tm),
        in_specs=[pl.BlockSpec((1, tm, d), lambda i, j: (i, j, 0)),
                  pl.BlockSpec((1, 6, d), lambda i, j: (jnp.where(j >= n_lat_tiles, nb, i), 0, 0)),
                  pl.BlockSpec((1, d), lambda i, j: (0, 0)),
                  pl.BlockSpec(w.shape, lambda i, j: (0, 0)),
                  pl.BlockSpec((tm, LANES), lambda i, j: (j, 0)),
                  pl.BlockSpec((tm, LANES), lambda i, j: (j, 0))],
        out_specs=pl.BlockSpec((1, tm, n_out), lambda i, j: (i, j, 0)),
        out_shape=jax.ShapeDtypeStruct((b, t, n_out), BF16),
        compiler_params=_cparams("parallel", "arbitrary"),
    )(xa, mods, gain, w, cos, sin)


def _fill_v_ones(vx_ref, v_ref):
    vx_ref[:, :LANES] = v_ref[0]
    vx_ref[:, LANES:] = jnp.ones((vx_ref.shape[0], LANES), BF16)


def _weights(s, m):
    return jnp.exp2(s - m).astype(BF16)


def _rowmax(*ss):
    m = jnp.max(ss[0], axis=-1, keepdims=True)
    for s in ss[1:]:
        m = jnp.maximum(m, jnp.max(s, axis=-1, keepdims=True))
    return m


def _normalised(ox):
    return ox[:, :LANES] / ox[:, LANES:LANES + 1]


def _scores(q, segs):
    return [_dot_nt(q, k) if bias is None else _dot_nt(q, k) + bias for k, bias in segs]


def _pv(ps, vxs):
    o = _dot(ps[0], vxs[0])
    for p, vx in zip(ps[1:], vxs[1:]):
        o = o + _dot(p, vx)
    return o


def _two_stream_attention(qa, segs_a, qb, segs_b, vxs):
    sa = _scores(qa, segs_a)
    ma = _rowmax(*sa)
    sb = _scores(qb, segs_b)
    pa = [_weights(s, ma) for s in sa]
    mb = _rowmax(*sb)
    oa = _pv(pa, vxs)
    pb = [_weights(s, mb) for s in sb]
    ob = _pv(pb, vxs)
    return _normalised(oa), _normalised(ob)


def _diff_attn_kernel(lam_ref, g_ref, q_ref, k_ref, v_ref, o_ref, vx_ref, *, n_lat, n_lat_blocks, lam_init):
    qi = pl.program_id(2)

    @pl.when(qi == 0)
    def _():
        _fill_v_ones(vx_ref, v_ref)

    lp = lam_ref[...]
    s1 = jnp.sum(lp[0:1] * lp[1:2], axis=-1, keepdims=True)
    s2 = jnp.sum(lp[2:3] * lp[3:4], axis=-1, keepdims=True)
    lam = jnp.exp(s1) - jnp.exp(s2) + lam_init
    q = q_ref[0]
    lane = lax.broadcasted_iota(jnp.int32, (1, LANES), 1)
    q1 = jnp.where(lane < DIFF_HEAD_DIM, q, jnp.zeros_like(q))
    q2 = jnp.where(lane >= DIFF_HEAD_DIM, q, jnp.zeros_like(q))

    def attend(k, vx):
        o1, o2 = _two_stream_attention(q1, [(k, None)], q2, [(k, None)], [vx])
        o = o1 - lam * o2
        o_ref[0] = (_rms(o) * g_ref[...] * (1.0 - lam_init)).astype(BF16)

    @pl.when(qi < n_lat_blocks)
    def _():
        attend(k_ref[0], vx_ref[...])

    @pl.when(qi >= n_lat_blocks)
    def _():
        attend(k_ref[0, n_lat:, :], vx_ref[n_lat:, :])


def _diff_attn(qkv, lam_params, subln, n_lat, lam_init, tq):
    b, t, _ = qkv.shape
    nh = DIFF_HEADS
    kern = functools.partial(_diff_attn_kernel, n_lat=n_lat, n_lat_blocks=n_lat // tq, lam_init=lam_init)
    return pl.pallas_call(
        kern,
        grid=(b, nh, t // tq),
        in_specs=[pl.BlockSpec(lam_params.shape, lambda i, h, j: (0, 0)),
                  pl.BlockSpec((1, LANES), lambda i, h, j: (0, 0)),
                  pl.BlockSpec((1, tq, LANES), lambda i, h, j: (i, j, h)),
                  pl.BlockSpec((1, t, LANES), lambda i, h, j: (i, 0, nh + h)),
                  pl.BlockSpec((1, t, LANES), lambda i, h, j: (i, 0, 2 * nh + h))],
        out_specs=pl.BlockSpec((1, tq, LANES), lambda i, h, j: (i, j, h)),
        out_shape=jax.ShapeDtypeStruct((b, t, nh * LANES), BF16),
        scratch_shapes=[pltpu.VMEM((t, 2 * LANES), BF16)],
        compiler_params=_cparams("parallel", "parallel", "arbitrary"),
    )(lam_params, subln, qkv, qkv, qkv)


def _na_patterns(rows):
    g = NA_GROUP_ROWS
    span = g + NA_WIN_ROWS - 1
    nq, nk = g * GRID_W, span * GRID_W
    qi = np.arange(nq)[:, None]
    ki = np.arange(nk)[None, :]
    qcol, kcol = qi % GRID_W, ki % GRID_W
    cs = np.clip(qcol - NA_WIN_COLS // 2, 0, GRID_W - NA_WIN_COLS)
    col_ok = (kcol >= cs) & (kcol < cs + NA_WIN_COLS)
    idx_c = np.clip(kcol - qcol + NA_WIN_COLS - 1, 0, 2 * NA_WIN_COLS - 2) + 0 * qi
    seen, pat_of_group, idx_r, valid = {}, [], [], []
    for r0 in range(0, rows, g):
        ks = min(max(r0 - NA_WIN_ROWS // 2, 0), rows - span)
        qrow = r0 + qi // GRID_W
        krow = ks + ki // GRID_W
        rs = np.clip(qrow - NA_WIN_ROWS // 2, 0, rows - NA_WIN_ROWS)
        ok = (krow >= rs) & (krow < rs + NA_WIN_ROWS) & col_ok
        ir = np.clip(krow - qrow + NA_WIN_ROWS - 1, 0, 2 * NA_WIN_ROWS - 2) + 0 * kcol
        key = (ok.tobytes(), np.where(ok, ir, 0).tobytes())
        if key not in seen:
            seen[key] = len(idx_r)
            idx_r.append(ir)
            valid.append(ok)
        pat_of_group.append(seen[key])
    n_pat = len(idx_r)
    return tuple(pat_of_group), np.stack(idx_r), np.stack([idx_c] * n_pat), np.stack(valid)


def _na_kernel(q_ref, k_ref, v_ref, bias_ref, o_ref, vx_ref, *, n_lat, pat_of_group):
    g_rows = NA_GROUP_ROWS
    span = g_rows + NA_WIN_ROWS - 1
    rows = n_lat // GRID_W
    nq, nk = g_rows * GRID_W, span * GRID_W
    lane = lax.broadcasted_iota(jnp.int32, (1, LANES), 1)
    head_masks = (lane < NA_HEAD_DIM, lane >= NA_HEAD_DIM)
    _fill_v_ones(vx_ref, v_ref)
    kc = k_ref[0, n_lat:, :]
    vxc = vx_ref[n_lat:, :]

    def both_heads(q, segs, vxs):
        qs = [jnp.where(hm, q, jnp.zeros_like(q)) for hm in head_masks]
        o0, o1 = _two_stream_attention(qs[0], [(k, None if bi is None else bias_ref[0, bi]) for k, bi in segs],
                                       qs[1], [(k, None if bi is None else bias_ref[1, bi]) for k, bi in segs], vxs)
        return jnp.where(head_masks[0], o0, o1).astype(BF16)

    def group(g, carry):
        r0 = g * g_rows
        ks = jnp.clip(r0 - NA_WIN_ROWS // 2, 0, rows - span)
        common = max(set(pat_of_group), key=pat_of_group.count)
        pat = jnp.int32(common)
        for gi, p in enumerate(pat_of_group):
            if p != common:
                pat = jnp.where(g == gi, p, pat)
        qg = q_ref[0, pl.ds(pl.multiple_of(g * nq, nq), nq), :]
        kw = k_ref[0, pl.ds(pl.multiple_of(ks * GRID_W, GRID_W), nk), :]
        vxw = vx_ref[pl.ds(pl.multiple_of(ks * GRID_W, GRID_W), nk), :]
        o_ref[0, pl.ds(pl.multiple_of(g * nq, nq), nq), :] = both_heads(qg, [(kw, pat), (kc, None)], [vxw, vxc])
        return carry

    lax.fori_loop(0, rows // g_rows, group, 0)

    o_ref[0, n_lat:, :] = both_heads(q_ref[0, n_lat:, :], [(kc, None)], [vxc])


def _na_attn(qkv, bias, n_lat, pat_of_group, col0):
    b, t, _ = qkv.shape
    npairs = NA_HEADS // 2
    cb = col0 // LANES
    kern = functools.partial(_na_kernel, n_lat=n_lat, pat_of_group=pat_of_group)
    return pl.pallas_call(
        kern,
        grid=(b, npairs),
        in_specs=[pl.BlockSpec((1, t, LANES), lambda i, h: (i, 0, cb + h)),
                  pl.BlockSpec((1, t, LANES), lambda i, h: (i, 0, cb + npairs + h)),
                  pl.BlockSpec((1, t, LANES), lambda i, h: (i, 0, cb + 2 * npairs + h)),
                  pl.BlockSpec((2,) + bias.shape[1:], lambda i, h: (h, 0, 0, 0))],
        out_specs=pl.BlockSpec((1, t, LANES), lambda i, h: (i, 0, h)),
        out_shape=jax.ShapeDtypeStruct((b, t, npairs * LANES), BF16),
        scratch_shapes=[pltpu.VMEM((t, 2 * LANES), BF16)],
        compiler_params=_cparams("parallel", "arbitrary"),
    )(qkv, qkv, qkv, bias)


def _proj_out_kernel(x_ref, a_ref, b_ref, wa_ref, wb_ref, m_ref, g_ref, wr_ref, xo_ref, h_ref, aff_ref):
    mods = m_ref[0]
    y = _dot(a_ref[0], wa_ref[...]) + _dot(b_ref[0], wb_ref[...])
    x = x_ref[0] + mods[2:3] * y
    xo_ref[0] = x
    h2 = _normed(x, g_ref[...], mods, 3, 4)
    h_hi = h2.astype(BF16)
    h_ref[0] = h_hi
    h_lo = (h2 - h_hi.astype(F32)).astype(BF16)
    wr = wr_ref[...]
    w_hi = wr.astype(BF16)
    w_lo = (wr - w_hi.astype(F32)).astype(BF16)
    logits = _dot(h_hi, w_hi) + (_dot(h_hi, w_lo) + _dot(h_lo, w_hi))
    lane = lax.broadcasted_iota(jnp.int32, logits.shape, 1)
    logits = jnp.where(lane < N_EXPERTS, logits, NEG_BIG)
    e = jnp.exp(logits - jnp.max(logits, axis=-1, keepdims=True))
    aff_ref[0] = e / jnp.sum(e, axis=-1, keepdims=True)


def _proj_out(xa, a, bb, w_out, mods, gain, w_router, n_tiles, n_lat_tiles):
    b, t, d = xa.shape
    nb = mods.shape[0] - 1
    tm = TOK_TILE
    wa, wb = w_out[:a.shape[-1]], w_out[a.shape[-1]:]
    tok = lambda i, j: (i, j, 0)
    const = lambda i, j: (0, 0)
    return pl.pallas_call(
        _proj_out_kernel,
        grid=(b, n_tiles),
        in_specs=[pl.BlockSpec((1, tm, d), tok),
                  pl.BlockSpec((1, tm, a.shape[-1]), tok),
                  pl.BlockSpec((1, tm, bb.shape[-1]), tok),
                  pl.BlockSpec(wa.shape, const),
                  pl.BlockSpec(wb.shape, const),
                  pl.BlockSpec((1, 6, d), lambda i, j: (jnp.where(j >= n_lat_tiles, nb, i), 0, 0)),
                  pl.BlockSpec((1, d), const),
                  pl.BlockSpec(w_router.shape, const)],
        out_specs=[pl.BlockSpec((1, tm, d), tok),
                   pl.BlockSpec((1, tm, d), tok),
                   pl.BlockSpec((1, tm, LANES), tok)],
        out_shape=[jax.ShapeDtypeStruct((b, n_tiles * tm, d), F32),
                   jax.ShapeDtypeStruct((b, n_tiles * tm, d), BF16),
                   jax.ShapeDtypeStruct((b, n_tiles * tm, LANES), F32)],
        compiler_params=_cparams("parallel", "arbitrary"),
    )(xa, a, bb, wa, wb, mods, gain, w_router)


def _ffn_kernel(x_ref, gate_ref, w1_ref, w3_ref, w2_ref, o_ref, w1b, w3b, w2b):
    @pl.when(pl.program_id(1) == 0)
    def _():
        w1b[...] = w1_ref[0].astype(BF16)
        w3b[...] = w3_ref[0].astype(BF16)
        w2b[...] = w2_ref[0].astype(BF16)

    x = x_ref[0]
    hid = _silu(_dot(x, w1b[...])) * _dot(x, w3b[...])
    o_ref[0] = (_dot(hid.astype(BF16), w2b[...]) * gate_ref[0]).astype(BF16)


def _expert_ffn(xe, gate, w1, w3, w2, layer):
    e, m, d = xe.shape
    f = w1.shape[-1]
    tm = next(c for c in (512, 256, 128, 64, 32) if m % c == 0)
    wspec = lambda r, c: pl.BlockSpec((None, 1, r, c), lambda i, j: (layer, i, 0, 0))
    return pl.pallas_call(
        _ffn_kernel,
        grid=(e, m // tm),
        in_specs=[pl.BlockSpec((1, tm, d), lambda i, j: (i, j, 0)),
                  pl.BlockSpec((1, tm, 1), lambda i, j: (i, j, 0)),
                  wspec(d, f), wspec(d, f), wspec(f, d)],
        out_specs=pl.BlockSpec((1, tm, d), lambda i, j: (i, j, 0)),
        out_shape=jax.ShapeDtypeStruct((e, m, d), BF16),
        scratch_shapes=[pltpu.VMEM((d, f), BF16), pltpu.VMEM((d, f), BF16), pltpu.VMEM((f, d), BF16)],
        compiler_params=_cparams("parallel", "arbitrary"),
    )(xe, gate, w1, w3, w2)


def _route(aff, segments):
    b, r, _ = aff.shape
    e = N_EXPERTS
    gates, flats, toks = [], [], []
    for row0, n in segments:
        cap = (EC_CAPACITY_FACTOR * n) // e
        gate, idx = lax.top_k(jnp.swapaxes(aff[:, row0:row0 + n, :e], 1, 2), cap)
        tok = idx + row0
        gates.append(jnp.swapaxes(gate, 0, 1).reshape(e, b * cap))
        flats.append(jnp.swapaxes(tok + jnp.arange(b, dtype=idx.dtype)[:, None, None] * r, 0, 1).reshape(e, b * cap))
        toks.append(tok)
    return jnp.concatenate(gates, axis=1)[..., None], jnp.concatenate(flats, axis=1), toks


def _combine_kernel(lo_ref, x_ref, tok_ref, y_ref, m_ref, g_ref, o_ref, *, final):
    i, j = pl.program_id(0), pl.program_id(1)
    tm = x_ref.shape[1]
    n_tiles = pl.num_programs(1)
    lo = lo_ref[i * (n_tiles + 1) + j]
    hi = lo_ref[i * (n_tiles + 1) + j + 1]
    rows = j * tm + lax.broadcasted_iota(jnp.int32, (tm, 1), 0)
    o_ref[0] = jnp.zeros(o_ref.shape[1:], F32)

    def chunk(c, carry):
        sel = (tok_ref[0, pl.ds(c, 1), :] == rows).astype(BF16)
        o_ref[0] += _dot(sel, y_ref[0, pl.ds(pl.multiple_of(c * PAIR_CHUNK, PAIR_CHUNK), PAIR_CHUNK), :])
        return carry

    lax.fori_loop(lo // PAIR_CHUNK, (hi + PAIR_CHUNK - 1) // PAIR_CHUNK, chunk, 0)
    x = x_ref[0] + m_ref[0][5:6] * o_ref[0]
    if final:
        x = _rms(x) * g_ref[...]
    o_ref[0] = x


def _combine(x, tok_sorted, y_sorted, lo, mods, gain, n_lat_tiles, final):
    b, t, d = x.shape
    p = tok_sorted.shape[1]
    nb = mods.shape[0] - 1
    tm = TOK_TILE
    tok = lambda i, j, lo_ref: (i, j, 0)
    whole = lambda i, j, lo_ref: (i, 0, 0)
    grid_spec = pltpu.PrefetchScalarGridSpec(
        num_scalar_prefetch=1,
        grid=(b, t // tm),
        in_specs=[pl.BlockSpec((1, tm, d), tok),
                  pl.BlockSpec((1, p // PAIR_CHUNK, PAIR_CHUNK), whole),
                  pl.BlockSpec((1, p, d), whole, pipeline_mode=pl.Buffered(1)),
                  pl.BlockSpec((1, 6, d), lambda i, j, lo_ref: (jnp.where(j >= n_lat_tiles, nb, i), 0, 0)),
                  pl.BlockSpec((1, d), lambda i, j, lo_ref: (0, 0))],
        out_specs=pl.BlockSpec((1, tm, d), tok))
    return pl.pallas_call(
        functools.partial(_combine_kernel, final=final),
        grid_spec=grid_spec,
        out_shape=jax.ShapeDtypeStruct((b, t, d), F32),
        compiler_params=_cparams("parallel", "arbitrary"),
    )(lo, x, tok_sorted.reshape(b, p // PAIR_CHUNK, PAIR_CHUNK), y_sorted, mods, gain)


def _moe(x_mid, h2, aff, segments, mods, gain, n_lat_tiles, final, w1, w3, w2, layer):
    b, r, d = h2.shape
    gate, flat, toks = _route(aff, segments)
    e, m = flat.shape
    xe = h2.reshape(b * r, d).at[flat].get(mode="promise_in_bounds")
    ye = _expert_ffn(xe, gate, w1, w3, w2, layer)
    srcs, off = [], 0
    for tok in toks:
        cap = tok.shape[-1]
        src = (jnp.arange(e, dtype=jnp.int32)[None, :, None] * m + off
               + jnp.arange(b, dtype=jnp.int32)[:, None, None] * cap + jnp.arange(cap, dtype=jnp.int32))
        srcs.append(src.reshape(b, e * cap))
        off += b * cap
    tok_all = jnp.concatenate([tok.reshape(b, -1) for tok in toks], axis=1)
    tok_sorted, src_sorted = lax.sort_key_val(tok_all, jnp.concatenate(srcs, axis=1), dimension=1)
    y_sorted = ye.reshape(e * m, d).at[src_sorted].get(mode="promise_in_bounds")
    starts = jnp.arange(r // TOK_TILE + 1, dtype=jnp.int32) * TOK_TILE
    lo = jnp.sum((tok_sorted[:, None, :] < starts[None, :, None]).astype(jnp.int32), axis=-1)
    return _combine(x_mid, tok_sorted, y_sorted, lo.reshape(-1), mods, gain, n_lat_tiles, final)


def _proj_odd_kernel(x_ref, m_ref, g_ref, w_ref, qn_ref, kvn_ref, wq_ref, wkv_ref, cos_ref, sin_ref,
                     q_ref, k_ref, v_ref, rq_ref, rk_ref, rv_ref, rg_ref, *, q_scale):
    h = _normed(x_ref[0], g_ref[...], m_ref[0], 0, 1)
    acc = _dot(h.astype(BF16), w_ref[...])
    cos, sin = cos_ref[...], sin_ref[...]
    c0 = MLA_Q_RANK
    c1 = c0 + MLA_KV_RANK
    cq = acc[:, :c0]
    ckv = acc[:, c0:c1]
    kr = acc[:, c1:c1 + LANES] * cos + acc[:, c1 + LANES:c1 + 2 * LANES] * sin
    c2 = c1 + 2 * LANES
    nq = RET_HEADS * RET_QK_DIM
    nv = RET_HEADS * RET_V_DIM
    rq_ref[0] = acc[:, c2:c2 + nq].astype(BF16)
    rk_ref[0] = acc[:, c2 + nq:c2 + 2 * nq].astype(BF16)
    rv_ref[0] = acc[:, c2 + 2 * nq:c2 + 2 * nq + nv].astype(BF16)
    rg_ref[0] = acc[:, c2 + 2 * nq + nv:]

    nqk = MLA_HEADS * LANES
    qq = _dot((_rms(cq) * qn_ref[...]).astype(BF16), wq_ref[...])
    q = qq[:, :nqk] * _tile_lanes(cos, MLA_HEADS) + qq[:, nqk:] * _tile_lanes(sin, MLA_HEADS)
    q_ref[0] = (q * q_scale).astype(BF16)
    kv = _dot((_rms(ckv) * kvn_ref[...]).astype(BF16), wkv_ref[...])
    k_ref[0] = (kv[:, :nqk] + _tile_lanes(kr, MLA_HEADS)).astype(BF16)
    v_ref[0] = kv[:, nqk:].astype(BF16)


def _proj_odd(xa, mods, gain, w, q_norm, kv_norm, wq, wkv, cos, sin, n_lat_tiles):
    b, t, d = xa.shape
    nb = mods.shape[0] - 1
    tm = TOK_TILE
    tok = lambda i, j: (i, j, 0)
    const = lambda i, j: (0, 0)
    widths = (MLA_HEADS * LANES, MLA_HEADS * LANES, MLA_HEADS * MLA_V_DIM,
              RET_HEADS * RET_QK_DIM, RET_HEADS * RET_QK_DIM, RET_HEADS * RET_V_DIM, RET_HEADS * RET_V_DIM)
    dtypes = (BF16,) * 6 + (F32,)
    kern = functools.partial(_proj_odd_kernel, q_scale=float((MLA_NOPE_DIM + MLA_ROPE_DIM) ** -0.5) * LOG2E)
    return pl.pallas_call(
        kern,
        grid=(b, t // tm),
        in_specs=[pl.BlockSpec((1, tm, d), tok),
                  pl.BlockSpec((1, 6, d), lambda i, j: (jnp.where(j >= n_lat_tiles, nb, i), 0, 0)),
                  pl.BlockSpec((1, d), const),
                  pl.BlockSpec(w.shape, const),
                  pl.BlockSpec(q_norm.shape, const),
                  pl.BlockSpec(kv_norm.shape, const),
                  pl.BlockSpec(wq.shape, const),
                  pl.BlockSpec(wkv.shape, const),
                  pl.BlockSpec((tm, LANES), lambda i, j: (j, 0)),
                  pl.BlockSpec((tm, LANES), lambda i, j: (j, 0))],
        out_specs=[pl.BlockSpec((1, tm, wd), tok) for wd in widths],
        out_shape=[jax.ShapeDtypeStruct((b, t, wd), dt) for wd, dt in zip(widths, dtypes)],
        compiler_params=_cparams("parallel", "arbitrary"),
    )(xa, mods, gain, w, q_norm, kv_norm, wq, wkv, cos, sin)


def _mla_kernel(q_ref, k_ref, v_ref, o_ref, vx_ref):
    @pl.when(pl.program_id(2) == 0)
    def _():
        _fill_v_ones(vx_ref, v_ref)

    o0, o1 = _two_stream_attention(q_ref[0, :, :LANES], [(k_ref[0, :, :LANES], None)],
                                   q_ref[0, :, LANES:], [(k_ref[0, :, LANES:], None)], [vx_ref[...]])
    lane = lax.broadcasted_iota(jnp.int32, (1, LANES), 1)
    o_ref[0] = jnp.where(lane < MLA_V_DIM, o0, o1).astype(BF16)


def _mla_attn(q, k, v, n_lat, tq):
    b, t, _ = k.shape
    npairs = MLA_HEADS // 2
    return pl.pallas_call(
        _mla_kernel,
        grid=(b, npairs, n_lat // tq),
        in_specs=[pl.BlockSpec((1, tq, 2 * LANES), lambda i, h, j: (i, j, h)),
                  pl.BlockSpec((1, t, 2 * LANES), lambda i, h, j: (i, 0, h)),
                  pl.BlockSpec((1, t, LANES), lambda i, h, j: (i, 0, h))],
        out_specs=pl.BlockSpec((1, tq, LANES), lambda i, h, j: (i, j, h)),
        out_shape=jax.ShapeDtypeStruct((b, n_lat, npairs * LANES), BF16),
        scratch_shapes=[pltpu.VMEM((t, 2 * LANES), BF16)],
        compiler_params=_cparams("parallel", "parallel", "arbitrary"),
    )(q, k, v)


def _retention_kernel(dl_ref, q_ref, k_ref, v_ref, g_ref, o_ref, f_ref, b_ref, st_ref, *, n_lat):
    c = RET_CHUNK
    t = k_ref.shape[1]
    n_chunks = n_lat // c
    n_ctx = (t - n_lat) // c
    hp = pl.program_id(1)
    lane = lax.broadcasted_iota(jnp.int32, (1, LANES), 1)
    pos_r = lax.broadcasted_iota(jnp.int32, (c, 1), 0).astype(F32)
    ii = lax.broadcasted_iota(jnp.int32, (c, c), 0)
    jj = lax.broadcasted_iota(jnp.int32, (c, c), 1)
    rel = (ii - jj).astype(F32)
    dl = dl_ref[...]
    hsel = lax.broadcasted_iota(jnp.int32, (1, RET_HEADS), 1)

    chains = []
    for hh in range(2):
        hmask = (lane >= hh * RET_QK_DIM) & (lane < (hh + 1) * RET_QK_DIM)
        for direction in range(2):
            logit = jnp.sum(jnp.where(hsel == 2 * hp + hh, dl[direction:direction + 1], 0.0),
                            axis=-1, keepdims=True)
            lg = jnp.minimum(logit, 0.0) - jnp.log(1.0 + jnp.exp(-jnp.abs(logit)))
            if direction == 0:
                mask = rel >= 0
                d_in = jnp.where(mask, jnp.exp(lg * jnp.where(mask, rel, 0.0)), 0.0)
                d_q = jnp.exp(lg * (pos_r + 1.0))
                d_k = jnp.exp(lg * (c - 1.0 - pos_r))
            else:
                mask = rel < 0
                d_in = jnp.where(mask, jnp.exp(lg * jnp.where(mask, -rel, 0.0)), 0.0)
                d_q = jnp.exp(lg * (c - pos_r))
                d_k = jnp.exp(lg * pos_r)
            chains.append(dict(idx=2 * hh + direction, v0=hh * RET_V_DIM, direction=direction, hmask=hmask,
                               d_in=d_in, d_q=d_q, d_k=d_k, d_chunk=jnp.exp(lg * c)))

    st_ref[...] = jnp.zeros(st_ref.shape, F32)

    def kv_update(ch, start):
        kb = k_ref[0, pl.ds(start, c), :].astype(F32)
        vb = v_ref[0, pl.ds(start, c), ch["v0"]:ch["v0"] + RET_V_DIM]
        kd = jnp.where(ch["hmask"], kb * ch["d_k"], 0.0).astype(BF16)
        st_ref[ch["idx"]] = ch["d_chunk"] * st_ref[ch["idx"]] + _dot_tn(kd, vb)

    for i in range(n_ctx):
        for ch in chains:
            ci = i if ch["direction"] == 0 else n_ctx - 1 - i
            kv_update(ch, n_lat + ci * c)

    def lat_step(i, carry):
        starts = [pl.multiple_of((i if ch["direction"] == 0 else n_chunks - 1 - i) * c, c) for ch in chains]
        qms, avs, crs = [], [], []
        for ch, start in zip(chains, starts):
            qb = q_ref[0, pl.ds(start, c), :]
            qm = jnp.where(ch["hmask"], qb, jnp.zeros_like(qb))
            qms.append(qm)
            avs.append(_dot_nt(qm, k_ref[0, pl.ds(start, c), :]))
        for ch, qm in zip(chains, qms):
            qd = (qm.astype(F32) * ch["d_q"]).astype(BF16)
            crs.append(_dot(qd, st_ref[ch["idx"]].astype(BF16)))
        for ch, start, a, cross in zip(chains, starts, avs, crs):
            vb = v_ref[0, pl.ds(start, c), ch["v0"]:ch["v0"] + RET_V_DIM]
            res = _dot((a * ch["d_in"]).astype(BF16), vb) + cross
            dst = f_ref if ch["direction"] == 0 else b_ref
            dst[pl.ds(start, c), ch["v0"]:ch["v0"] + RET_V_DIM] = res
        for ch, start in zip(chains, starts):
            kv_update(ch, start)
        return carry

    lax.fori_loop(0, n_chunks, lat_step, 0)

    def finish(i, carry):
        start = pl.multiple_of(i * c, c)
        for hh in range(2):
            v0 = hh * RET_V_DIM
            r = _rms(f_ref[pl.ds(start, c), v0:v0 + RET_V_DIM] + b_ref[pl.ds(start, c), v0:v0 + RET_V_DIM])
            gate = _silu(g_ref[0, pl.ds(start, c), v0:v0 + RET_V_DIM])
            o_ref[0, pl.ds(start, c), v0:v0 + RET_V_DIM] = (r * gate).astype(BF16)
        return carry

    lax.fori_loop(0, n_chunks, finish, 0)


def _retention(rq, rk, rv, rg, decay_logit, n_lat):
    b, t, _ = rq.shape
    npairs = RET_HEADS // 2
    kern = functools.partial(_retention_kernel, n_lat=n_lat)
    return pl.pallas_call(
        kern,
        grid=(b, npairs),
        in_specs=[pl.BlockSpec(decay_logit.shape, lambda i, h: (0, 0)),
                  pl.BlockSpec((1, n_lat, LANES), lambda i, h: (i, 0, h)),
                  pl.BlockSpec((1, t, LANES), lambda i, h: (i, 0, h)),
                  pl.BlockSpec((1, t, 2 * RET_V_DIM), lambda i, h: (i, 0, h)),
                  pl.BlockSpec((1, n_lat, 2 * RET_V_DIM), lambda i, h: (i, 0, h))],
        out_specs=pl.BlockSpec((1, n_lat, 2 * RET_V_DIM), lambda i, h: (i, 0, h)),
        out_shape=jax.ShapeDtypeStruct((b, n_lat, RET_HEADS * RET_V_DIM), BF16),
        scratch_shapes=[pltpu.VMEM((n_lat, 2 * RET_V_DIM), F32), pltpu.VMEM((n_lat, 2 * RET_V_DIM), F32),
                        pltpu.VMEM((4, LANES, RET_V_DIM), F32)],
        compiler_params=_cparams("parallel", "arbitrary"),
    )(decay_logit, rq, rk, rv, rg)


def _rope_tables(n_lat, n_ctx, rot_dim, lane0):
    t = np.arange(n_lat)
    rows = (t // GRID_W).astype(np.float32)
    cols = (t % GRID_W).astype(np.float32)
    m = rot_dim // 4
    freqs = jnp.asarray(ROPE_BASE, F32) ** (-jnp.arange(m, dtype=F32) / m)
    ang_r = jnp.asarray(rows)[:, None] * freqs
    ang_c = jnp.asarray(cols)[:, None] * freqs
    cos = jnp.concatenate([jnp.cos(ang_r)] * 2 + [jnp.cos(ang_c)] * 2, axis=-1)
    sin = jnp.concatenate([-jnp.sin(ang_r), jnp.sin(ang_r), -jnp.sin(ang_c), jnp.sin(ang_c)], axis=-1)
    reps = (LANES - lane0) // rot_dim if lane0 == 0 else 1
    cos = jnp.concatenate([jnp.ones((n_lat, lane0), F32)] + [cos] * reps
                          + [jnp.ones((n_lat, LANES - lane0 - reps * rot_dim), F32)], axis=-1)
    sin = jnp.concatenate([jnp.zeros((n_lat, lane0), F32)] + [sin] * reps
                          + [jnp.zeros((n_lat, LANES - lane0 - reps * rot_dim), F32)], axis=-1)
    cos = jnp.concatenate([cos, jnp.ones((n_ctx, LANES), F32)], axis=0)
    sin = jnp.concatenate([sin, jnp.zeros((n_ctx, LANES), F32)], axis=0)
    return cos, sin


def _rot_partner(w, rot_dim):
    k, n = w.shape
    q = rot_dim // 4
    return w.reshape(k, n // rot_dim, 2, 2, q)[:, :, :, ::-1, :].reshape(k, n)


def _pad_cols(w, lane0, width=LANES):
    k, n = w.shape
    return jnp.concatenate([jnp.zeros((k, lane0), w.dtype), w, jnp.zeros((k, width - lane0 - n), w.dtype)], axis=-1)


def _even_weights(w_in):
    n_rope = 2 * DIFF_HEADS * 2 * DIFF_HEAD_DIM
    dw = DIFF_HEADS * 2 * DIFF_HEAD_DIM
    nw = NA_HEADS * NA_HEAD_DIM
    scale = jnp.concatenate([jnp.full((dw,), DIFF_HEAD_DIM ** -0.5, F32), jnp.ones((2 * dw,), F32),
                             jnp.full((nw,), NA_HEAD_DIM ** -0.5, F32), jnp.ones((2 * nw,), F32)])
    w = w_in * scale
    return jnp.concatenate([w, _rot_partner(w[:, :n_rope], DIFF_HEAD_DIM)], axis=-1).astype(BF16), n_rope


def _odd_weights(w_in, w_uq, w_ukv):
    c0 = MLA_Q_RANK
    c1 = c0 + MLA_KV_RANK
    c2 = c1 + MLA_ROPE_DIM
    nq = RET_HEADS * RET_QK_DIM
    kr = w_in[:, c1:c2]
    w = jnp.concatenate([w_in[:, :c1],
                         _pad_cols(kr, MLA_NOPE_DIM), _pad_cols(_rot_partner(kr, MLA_ROPE_DIM), MLA_NOPE_DIM),
                         w_in[:, c2:c2 + nq], w_in[:, c2 + nq:c2 + 2 * nq] * (RET_QK_DIM ** -0.5),
                         w_in[:, c2 + 2 * nq:]], axis=-1).astype(BF16)
    r = w_uq.shape[0]
    uq = w_uq.reshape(r, MLA_HEADS, MLA_NOPE_DIM + MLA_ROPE_DIM)
    pad = jnp.zeros((r, MLA_HEADS, LANES - MLA_NOPE_DIM - MLA_ROPE_DIM), F32)
    uq_rot = _rot_partner(uq[:, :, MLA_NOPE_DIM:].reshape(r, -1), MLA_ROPE_DIM).reshape(r, MLA_HEADS, MLA_ROPE_DIM)
    wq = jnp.concatenate([uq, pad], axis=-1).reshape(r, -1)
    wq_rot = jnp.concatenate([jnp.zeros_like(uq[:, :, :MLA_NOPE_DIM]), uq_rot, pad], axis=-1).reshape(r, -1)
    rk = w_ukv.shape[0]
    ukv = w_ukv.reshape(rk, MLA_HEADS, MLA_NOPE_DIM + MLA_V_DIM)
    wk = jnp.concatenate([ukv[:, :, :MLA_NOPE_DIM], jnp.zeros((rk, MLA_HEADS, LANES - MLA_NOPE_DIM), F32)],
                         axis=-1).reshape(rk, -1)
    wv = ukv[:, :, MLA_NOPE_DIM:].reshape(rk, -1)
    return w, jnp.concatenate([wq, wq_rot], axis=-1).astype(BF16), jnp.concatenate([wk, wv], axis=-1).astype(BF16)


def _na_bias(rpb, n_lat):
    pat_of_group, idx_r, idx_c, valid = _na_patterns(n_lat // GRID_W)
    g = NA_GROUP_ROWS
    span = g + NA_WIN_ROWS - 1
    n_pat = idx_r.shape[0]
    col_sel = (idx_c[0, :GRID_W, :GRID_W, None] == np.arange(2 * NA_WIN_COLS - 1)).astype(np.float32)
    row_idx = idx_r.reshape(n_pat, g, GRID_W, span, GRID_W)[:, :, 0, :, 0]
    row_sel = (row_idx[..., None] == np.arange(2 * NA_WIN_ROWS - 1)).astype(np.float32)
    cols = jnp.einsum('hrc,qkc->hrqk', rpb, col_sel, precision=lax.Precision.HIGHEST)
    bias = jnp.einsum('pijr,hrqk->hpiqjk', row_sel, cols, precision=lax.Precision.HIGHEST)
    bias = bias.reshape(rpb.shape[0], n_pat, g * GRID_W, span * GRID_W)
    return jnp.where(valid[None], bias * LOG2E, NEG_BIG), pat_of_group


def _router_weights(w_router):
    return _pad_cols(w_router, 0)


def kernel(x, c, ctx, c_ctx, ada_w, ada_b, norm_mix, norm_ffn, final_norm, even_w_in, even_w_out,
           diff_lambda, diff_subln, na_rpb, odd_w_in, odd_w_out, mla_q_norm, mla_w_uq, mla_kv_norm,
           mla_w_ukv, ret_decay_logit, moe_router, moe_w1, moe_w3, moe_w2):
    b, n_lat, d = x.shape
    n_ctx = ctx.shape[1]
    depth = ada_w.shape[0]
    n_lat_tiles = n_lat // TOK_TILE
    n_tiles = (n_lat + n_ctx) // TOK_TILE

    cond = jnp.concatenate([c, c_ctx[None], jnp.zeros((7, d), F32)], axis=0)
    mods_all = _mods(cond, ada_w, ada_b)[:, :b + 1].reshape(depth, b + 1, 6, d)
    xa = jnp.concatenate([x, ctx], axis=1)

    for l in range(depth):
        mods = mods_all[l]
        need_ctx = l < depth - 1
        i = l // 2
        gain_mix = norm_mix[l][None]
        if l % 2 == 0:
            w, n_rope = _even_weights(even_w_in[i])
            cos, sin = _rope_tables(n_lat, n_ctx, DIFF_HEAD_DIM, 0)
            qkv = _proj_even(xa, mods, gain_mix, w, cos, sin, n_lat_tiles, n_rope, even_w_in.shape[-1])
            lam_init = 0.8 - 0.6 * math.exp(-0.3 * l)
            mix_a = _diff_attn(qkv, diff_lambda[i], diff_subln[i][None], n_lat, lam_init, TOK_TILE)
            bias, offsets = _na_bias(na_rpb[i], n_lat)
            mix_b = _na_attn(qkv, bias, n_lat, offsets, 3 * DIFF_HEADS * 2 * DIFF_HEAD_DIM)
            w_out = even_w_out[i].astype(BF16)
        else:
            w, wq, wkv = _odd_weights(odd_w_in[i], mla_w_uq[i], mla_w_ukv[i])
            cos, sin = _rope_tables(n_lat, n_ctx, MLA_ROPE_DIM, MLA_NOPE_DIM)
            q, k, v, rq, rk, rv, rg = _proj_odd(xa, mods, gain_mix, w, mla_q_norm[i][None], mla_kv_norm[i][None],
                                                wq, wkv, cos, sin, n_lat_tiles)
            mix_a = _mla_attn(q, k, v, n_lat, TOK_TILE)
            mix_b = _retention(rq, rk, rv, rg, ret_decay_logit[i], n_lat)
            w_out = odd_w_out[i].astype(BF16)

        tiles = n_tiles if need_ctx else n_lat_tiles
        x_mid, h2, aff = _proj_out(xa, mix_a, mix_b, w_out, mods, norm_ffn[l][None],
                                   _router_weights(moe_router[l]), tiles, n_lat_tiles)
        segments = ((0, n_lat), (n_lat, n_ctx)) if need_ctx else ((0, n_lat),)
        xa = _moe(x_mid, h2, aff, segments, mods, final_norm[None], n_lat_tiles, l == depth - 1,
                  moe_w1, moe_w3, moe_w2, l)
    return xa
```

```python
import functools
import math

import jax
import jax.numpy as jnp
import numpy as np
from jax import lax
from jax.experimental import pallas as pl
from jax.experimental.pallas import tpu as pltpu

GRID_W = 64
ROPE_BASE = 10000.0
RMS_EPS = 1e-6
DIFF_HEADS = 4
DIFF_HEAD_DIM = 64
NA_HEADS = 8
NA_HEAD_DIM = 64
NA_WIN_ROWS = 8
NA_WIN_COLS = 16
MLA_HEADS = 8
MLA_Q_RANK = 256
MLA_KV_RANK = 128
MLA_NOPE_DIM = 64
MLA_ROPE_DIM = 32
MLA_V_DIM = 64
RET_HEADS = 4
RET_QK_DIM = 64
RET_V_DIM = 128
RET_CHUNK = 128
N_EXPERTS = 16
EC_CAPACITY_FACTOR = 2

LANES = 128
VMEM_LIMIT = 56 * 1024 * 1024
NEG_BIG = -1e30
TOK_TILE = 256
PROJ_IN_TILES = 8
ATTN_Q_TILE = 512
FFN_ROW_TILES = 4
FFN_HIDDEN_BLOCK = 512
NA_GROUP_ROWS = 4
PAIR_CHUNK = 256
LOG2E = math.log2(math.e)

BF16 = jnp.bfloat16
F32 = jnp.float32


def _cparams(*sem):
    return pltpu.CompilerParams(dimension_semantics=sem, vmem_limit_bytes=VMEM_LIMIT)


def _dot(a, b):
    return jnp.dot(a, b, preferred_element_type=F32)


def _dot_nt(a, b):
    return lax.dot_general(a, b, (((1,), (1,)), ((), ())), preferred_element_type=F32)


def _dot_tn(a, b):
    return lax.dot_general(a, b, (((0,), (0,)), ((), ())), preferred_element_type=F32)


def _rms(x):
    return x * lax.rsqrt(jnp.mean(x * x, axis=-1, keepdims=True) + RMS_EPS)


def _silu(x):
    return x * (1.0 / (1.0 + jnp.exp(-x)))


def _tile_lanes(t, n):
    return jnp.concatenate([t] * n, axis=-1)


def _mods_kernel(c_ref, w_ref, b_ref, o_ref):
    o_ref[0] = _dot(_silu(c_ref[...]), w_ref[0]) + b_ref[0]


def _mods(cond, ada_w, ada_b):
    depth, d, n = ada_w.shape
    r = cond.shape[0]
    tn = 1536
    return pl.pallas_call(
        _mods_kernel,
        grid=(depth, n // tn),
        in_specs=[pl.BlockSpec((r, d), lambda l, j: (0, 0)),
                  pl.BlockSpec((1, d, tn), lambda l, j: (l, 0, j)),
                  pl.BlockSpec((1, 1, tn), lambda l, j: (l, 0, j))],
        out_specs=pl.BlockSpec((1, r, tn), lambda l, j: (l, 0, j)),
        out_shape=jax.ShapeDtypeStruct((depth, r, n), F32),
        compiler_params=_cparams("arbitrary", "arbitrary"),
    )(cond, ada_w, ada_b.reshape(depth, 1, n))


def _normed(x, gain, mods, shift_row, scale_row):
    return _rms(x) * gain * (1.0 + mods[scale_row:scale_row + 1]) + mods[shift_row:shift_row + 1]


def _mixer_input(x, gain, mods_lat, mods_ctx, n_lat):
    tm = x.shape[0]
    rows = pl.program_id(1) * tm + lax.broadcasted_iota(jnp.int32, (tm, 1), 0)
    is_ctx = rows >= n_lat
    scale = jnp.where(is_ctx, mods_ctx[1:2], mods_lat[1:2])
    shift = jnp.where(is_ctx, mods_ctx[0:1], mods_lat[0:1])
    return _rms(x) * gain * (1.0 + scale) + shift


def _proj_even_kernel(x_ref, ml_ref, mc_ref, g_ref, w_ref, cos_ref, sin_ref, o_ref, *, n_rope, n_out, n_lat):
    na_q0 = n_rope + n_rope // 2
    na_q1 = na_q0 + NA_HEADS * NA_HEAD_DIM
    h = _mixer_input(x_ref[0], g_ref[...], ml_ref[0], mc_ref[0], n_lat)
    acc = _dot(h.astype(BF16), w_ref[...])
    reps = n_rope // LANES
    cos = _tile_lanes(cos_ref[...], reps)
    sin = _tile_lanes(sin_ref[...], reps)
    roped = acc[:, :n_rope] * cos + acc[:, n_out:] * sin
    n_q = n_rope // 2
    o_ref[0, :, :n_q] = (roped[:, :n_q] * LOG2E).astype(BF16)
    o_ref[0, :, n_q:n_rope] = roped[:, n_q:].astype(BF16)
    o_ref[0, :, n_rope:na_q0] = acc[:, n_rope:na_q0].astype(BF16)
    o_ref[0, :, na_q0:na_q1] = (acc[:, na_q0:na_q1] * LOG2E).astype(BF16)
    o_ref[0, :, na_q1:] = acc[:, na_q1:n_out].astype(BF16)


def _proj_in_tile(t):
    tm = t // PROJ_IN_TILES
    assert tm * PROJ_IN_TILES == t and tm % 16 == 0
    return tm


def _proj_even(xa, mods, gain, w, cos, sin, n_lat, n_rope, n_out):
    b, t, d = xa.shape
    nb = mods.shape[0] - 1
    tm = _proj_in_tile(t)
    kern = functools.partial(_proj_even_kernel, n_rope=n_rope, n_out=n_out, n_lat=n_lat)
    return pl.pallas_call(
        kern,
        grid=(b, t // tm),
        in_specs=[pl.BlockSpec((1, tm, d), lambda i, j: (i, j, 0)),
                  pl.BlockSpec((1, 6, d), lambda i, j: (i, 0, 0)),
                  pl.BlockSpec((1, 6, d), lambda i, j: (nb, 0, 0)),
                  pl.BlockSpec((1, d), lambda i, j: (0, 0)),
                  pl.BlockSpec(w.shape, lambda i, j: (0, 0)),
                  pl.BlockSpec((tm, LANES), lambda i, j: (j, 0)),
                  pl.BlockSpec((tm, LANES), lambda i, j: (j, 0))],
        out_specs=pl.BlockSpec((1, tm, n_out), lambda i, j: (i, j, 0)),
        out_shape=jax.ShapeDtypeStruct((b, t, n_out), BF16),
        compiler_params=_cparams("parallel", "arbitrary"),
    )(xa, mods, mods, gain, w, cos, sin)


def _fill_v_ones(vx_ref, v_ref):
    vx_ref[:, :LANES] = v_ref[0]
    vx_ref[:, LANES:] = jnp.ones((vx_ref.shape[0], LANES), BF16)


def _weights(s, m):
    return jnp.exp2(s - m).astype(BF16)


def _rowmax(*ss):
    m = jnp.max(ss[0], axis=-1, keepdims=True)
    for s in ss[1:]:
        m = jnp.maximum(m, jnp.max(s, axis=-1, keepdims=True))
    return m


def _normalised(ox):
    return ox[:, :LANES] / ox[:, LANES:LANES + 1]


def _scores(q, segs):
    return [_dot_nt(q, k) if bias is None else _dot_nt(q, k) + bias for k, bias in segs]


def _pv(ps, vxs):
    o = _dot(ps[0], vxs[0])
    for p, vx in zip(ps[1:], vxs[1:]):
        o = o + _dot(p, vx)
    return o


def _two_stream_attention(qa, segs_a, qb, segs_b, vxs):
    sa = _scores(qa, segs_a)
    ma = _rowmax(*sa)
    sb = _scores(qb, segs_b)
    pa = [_weights(s, ma) for s in sa]
    mb = _rowmax(*sb)
    oa = _pv(pa, vxs)
    pb = [_weights(s, mb) for s in sb]
    ob = _pv(pb, vxs)
    return _normalised(oa), _normalised(ob)


def _diff_attn_kernel(lam_ref, g_ref, q_ref, k_ref, v_ref, o_ref, vx_ref, *, lam_init):
    @pl.when(pl.program_id(2) == 0)
    def _():
        _fill_v_ones(vx_ref, v_ref)

    lp = lam_ref[...]
    s1 = jnp.sum(lp[0:1] * lp[1:2], axis=-1, keepdims=True)
    s2 = jnp.sum(lp[2:3] * lp[3:4], axis=-1, keepdims=True)
    lam = jnp.exp(s1) - jnp.exp(s2) + lam_init
    q = q_ref[0]
    lane = lax.broadcasted_iota(jnp.int32, (1, LANES), 1)
    q1 = jnp.where(lane < DIFF_HEAD_DIM, q, jnp.zeros_like(q))
    q2 = jnp.where(lane >= DIFF_HEAD_DIM, q, jnp.zeros_like(q))
    k = k_ref[0]
    o1, o2 = _two_stream_attention(q1, [(k, None)], q2, [(k, None)], [vx_ref[...]])
    o = o1 - lam * o2
    o_ref[0] = (_rms(o) * g_ref[...] * (1.0 - lam_init)).astype(BF16)


def _diff_attn(qkv, lam_params, subln, lam_init, q_row0, n_q, tq, k_row0, n_k):
    b = qkv.shape[0]
    nh = DIFF_HEADS
    q0, k0 = q_row0 // tq, k_row0 // n_k
    assert q0 * tq == q_row0 and k0 * n_k == k_row0 and n_q % tq == 0
    return pl.pallas_call(
        functools.partial(_diff_attn_kernel, lam_init=lam_init),
        grid=(b, nh, n_q // tq),
        in_specs=[pl.BlockSpec(lam_params.shape, lambda i, h, j: (0, 0)),
                  pl.BlockSpec((1, LANES), lambda i, h, j: (0, 0)),
                  pl.BlockSpec((1, tq, LANES), lambda i, h, j: (i, q0 + j, h)),
                  pl.BlockSpec((1, n_k, LANES), lambda i, h, j: (i, k0, nh + h)),
                  pl.BlockSpec((1, n_k, LANES), lambda i, h, j: (i, k0, 2 * nh + h))],
        out_specs=pl.BlockSpec((1, tq, LANES), lambda i, h, j: (i, j, h)),
        out_shape=jax.ShapeDtypeStruct((b, n_q, nh * LANES), BF16),
        scratch_shapes=[pltpu.VMEM((n_k, 2 * LANES), BF16)],
        compiler_params=_cparams("parallel", "parallel", "arbitrary"),
    )(lam_params, subln, qkv, qkv, qkv)


def _na_patterns(rows):
    g = NA_GROUP_ROWS
    span = g + NA_WIN_ROWS - 1
    nq, nk = g * GRID_W, span * GRID_W
    qi = np.arange(nq)[:, None]
    ki = np.arange(nk)[None, :]
    qcol, kcol = qi % GRID_W, ki % GRID_W
    cs = np.clip(qcol - NA_WIN_COLS // 2, 0, GRID_W - NA_WIN_COLS)
    col_ok = (kcol >= cs) & (kcol < cs + NA_WIN_COLS)
    idx_c = np.clip(kcol - qcol + NA_WIN_COLS - 1, 0, 2 * NA_WIN_COLS - 2) + 0 * qi
    seen, pat_of_group, idx_r, valid = {}, [], [], []
    for r0 in range(0, rows, g):
        ks = min(max(r0 - NA_WIN_ROWS // 2, 0), rows - span)
        qrow = r0 + qi // GRID_W
        krow = ks + ki // GRID_W
        rs = np.clip(qrow - NA_WIN_ROWS // 2, 0, rows - NA_WIN_ROWS)
        ok = (krow >= rs) & (krow < rs + NA_WIN_ROWS) & col_ok
        ir = np.clip(krow - qrow + NA_WIN_ROWS - 1, 0, 2 * NA_WIN_ROWS - 2) + 0 * kcol
        key = (ok.tobytes(), np.where(ok, ir, 0).tobytes())
        if key not in seen:
            seen[key] = len(idx_r)
            idx_r.append(ir)
            valid.append(ok)
        pat_of_group.append(seen[key])
    n_pat = len(idx_r)
    return tuple(pat_of_group), np.stack(idx_r), np.stack([idx_c] * n_pat), np.stack(valid)


def _na_kernel(q_ref, k_ref, v_ref, bias_ref, o_ref, vx_ref, *, n_lat, pat_of_group):
    g_rows = NA_GROUP_ROWS
    span = g_rows + NA_WIN_ROWS - 1
    rows = n_lat // GRID_W
    nq, nk = g_rows * GRID_W, span * GRID_W
    lane = lax.broadcasted_iota(jnp.int32, (1, LANES), 1)
    head_masks = (lane < NA_HEAD_DIM, lane >= NA_HEAD_DIM)
    _fill_v_ones(vx_ref, v_ref)
    kc = k_ref[0, n_lat:, :]
    vxc = vx_ref[n_lat:, :]

    def both_heads(q, segs, vxs):
        qs = [jnp.where(hm, q, jnp.zeros_like(q)) for hm in head_masks]
        o0, o1 = _two_stream_attention(qs[0], [(k, None if bi is None else bias_ref[0, bi]) for k, bi in segs],
                                       qs[1], [(k, None if bi is None else bias_ref[1, bi]) for k, bi in segs], vxs)
        return jnp.where(head_masks[0], o0, o1).astype(BF16)

    def group(g, carry):
        r0 = g * g_rows
        ks = jnp.clip(r0 - NA_WIN_ROWS // 2, 0, rows - span)
        common = max(set(pat_of_group), key=pat_of_group.count)
        pat = jnp.int32(common)
        for gi, p in enumerate(pat_of_group):
            if p != common:
                pat = jnp.where(g == gi, p, pat)
        qg = q_ref[0, pl.ds(pl.multiple_of(g * nq, nq), nq), :]
        kw = k_ref[0, pl.ds(pl.multiple_of(ks * GRID_W, GRID_W), nk), :]
        vxw = vx_ref[pl.ds(pl.multiple_of(ks * GRID_W, GRID_W), nk), :]
        o_ref[0, pl.ds(pl.multiple_of(g * nq, nq), nq), :] = both_heads(qg, [(kw, pat), (kc, None)], [vxw, vxc])
        return carry

    lax.fori_loop(0, rows // g_rows, group, 0)

    o_ref[0, n_lat:, :] = both_heads(q_ref[0, n_lat:, :], [(kc, None)], [vxc])


def _na_attn(qkv, bias, n_lat, pat_of_group, col0):
    b, t, _ = qkv.shape
    npairs = NA_HEADS // 2
    cb = col0 // LANES
    kern = functools.partial(_na_kernel, n_lat=n_lat, pat_of_group=pat_of_group)
    return pl.pallas_call(
        kern,
        grid=(b, npairs),
        in_specs=[pl.BlockSpec((1, t, LANES), lambda i, h: (i, 0, cb + h)),
                  pl.BlockSpec((1, t, LANES), lambda i, h: (i, 0, cb + npairs + h)),
                  pl.BlockSpec((1, t, LANES), lambda i, h: (i, 0, cb + 2 * npairs + h)),
                  pl.BlockSpec((2,) + bias.shape[1:], lambda i, h: (h, 0, 0, 0))],
        out_specs=pl.BlockSpec((1, t, LANES), lambda i, h: (i, 0, h)),
        out_shape=jax.ShapeDtypeStruct((b, t, npairs * LANES), BF16),
        scratch_shapes=[pltpu.VMEM((t, 2 * LANES), BF16)],
        compiler_params=_cparams("parallel", "arbitrary"),
    )(qkv, qkv, qkv, bias)


def _proj_out_kernel(x_ref, a_ref, ac_ref, b_ref, wa_ref, wb_ref, m_ref, g_ref, wr_ref, xo_ref, h_ref, aff_ref,
                     *, n_lat_tiles):
    mods = m_ref[0]
    a = jnp.where(pl.program_id(1) >= n_lat_tiles, ac_ref[0], a_ref[0])
    y = _dot(a, wa_ref[...]) + _dot(b_ref[0], wb_ref[...])
    x = x_ref[0] + mods[2:3] * y
    xo_ref[0] = x
    h2 = _normed(x, g_ref[...], mods, 3, 4)
    h_hi = h2.astype(BF16)
    h_ref[0] = h_hi
    h_lo = (h2 - h_hi.astype(F32)).astype(BF16)
    wr = wr_ref[...]
    w_hi = wr.astype(BF16)
    w_lo = (wr - w_hi.astype(F32)).astype(BF16)
    logits = _dot(h_hi, w_hi) + (_dot(h_hi, w_lo) + _dot(h_lo, w_hi))
    lane = lax.broadcasted_iota(jnp.int32, logits.shape, 1)
    logits = jnp.where(lane < N_EXPERTS, logits, NEG_BIG)
    e = jnp.exp(logits - jnp.max(logits, axis=-1, keepdims=True))
    aff_ref[0] = e / jnp.sum(e, axis=-1, keepdims=True)


def _proj_out(xa, a, a_ctx, bb, w_out, mods, gain, w_router, n_tiles, n_lat_tiles):
    b, t, d = xa.shape
    nb = mods.shape[0] - 1
    tm = TOK_TILE
    wa, wb = w_out[:a.shape[-1]], w_out[a.shape[-1]:]
    tok = lambda i, j: (i, j, 0)
    const = lambda i, j: (0, 0)
    return pl.pallas_call(
        functools.partial(_proj_out_kernel, n_lat_tiles=n_lat_tiles),
        grid=(b, n_tiles),
        in_specs=[pl.BlockSpec((1, tm, d), tok),
                  pl.BlockSpec((1, tm, a.shape[-1]), lambda i, j: (i, jnp.minimum(j, n_lat_tiles - 1), 0)),
                  pl.BlockSpec((1, tm, a.shape[-1]), lambda i, j: (i, jnp.maximum(j - n_lat_tiles, 0), 0)),
                  pl.BlockSpec((1, tm, bb.shape[-1]), tok),
                  pl.BlockSpec(wa.shape, const),
                  pl.BlockSpec(wb.shape, const),
                  pl.BlockSpec((1, 6, d), lambda i, j: (jnp.where(j >= n_lat_tiles, nb, i), 0, 0)),
                  pl.BlockSpec((1, d), const),
                  pl.BlockSpec(w_router.shape, const)],
        out_specs=[pl.BlockSpec((1, tm, d), tok),
                   pl.BlockSpec((1, tm, d), tok),
                   pl.BlockSpec((1, tm, LANES), tok)],
        out_shape=[jax.ShapeDtypeStruct((b, n_tiles * tm, d), F32),
                   jax.ShapeDtypeStruct((b, n_tiles * tm, d), BF16),
                   jax.ShapeDtypeStruct((b, n_tiles * tm, LANES), F32)],
        compiler_params=_cparams("parallel", "arbitrary"),
    )(xa, a, a_ctx, bb, wa, wb, mods, gain, w_router)


def _ffn_kernel(x_ref, gate_ref, w1_ref, w3_ref, w2_ref, o_ref, w1b, w3b, w2b):
    @pl.when(pl.program_id(1) == 0)
    def _():
        w1b[...] = w1_ref[0].astype(BF16)
        w3b[...] = w3_ref[0].astype(BF16)
        w2b[...] = w2_ref[0].astype(BF16)

    x = x_ref[0]
    f = w1b.shape[1]
    y = None
    for f0 in range(0, f, FFN_HIDDEN_BLOCK):
        f1 = f0 + FFN_HIDDEN_BLOCK
        hid = _silu(_dot(x, w1b[:, f0:f1])) * _dot(x, w3b[:, f0:f1])
        part = _dot(hid.astype(BF16), w2b[f0:f1, :])
        y = part if y is None else y + part
    o_ref[0] = (y * gate_ref[0]).astype(BF16)


def _expert_ffn(xe, gate, w1, w3, w2, layer):
    e, m, d = xe.shape
    f = w1.shape[-1]
    tm = m // FFN_ROW_TILES
    assert tm * FFN_ROW_TILES == m and tm % 16 == 0
    wspec = lambda r, c: pl.BlockSpec((None, 1, r, c), lambda i, j: (layer, i, 0, 0), pipeline_mode=pl.Buffered(1))
    return pl.pallas_call(
        _ffn_kernel,
        grid=(e, m // tm),
        in_specs=[pl.BlockSpec((1, tm, d), lambda i, j: (i, j, 0)),
                  pl.BlockSpec((1, tm, 1), lambda i, j: (i, j, 0)),
                  wspec(d, f), wspec(d, f), wspec(f, d)],
        out_specs=pl.BlockSpec((1, tm, d), lambda i, j: (i, j, 0)),
        out_shape=jax.ShapeDtypeStruct((e, m, d), BF16),
        scratch_shapes=[pltpu.VMEM((d, f), BF16), pltpu.VMEM((d, f), BF16), pltpu.VMEM((f, d), BF16)],
        compiler_params=_cparams("parallel", "arbitrary"),
    )(xe, gate, w1, w3, w2)


def _route(aff, segments):
    b, r, _ = aff.shape
    e = N_EXPERTS
    gates, flats, toks = [], [], []
    for row0, n in segments:
        cap = (EC_CAPACITY_FACTOR * n) // e
        gate, idx = lax.top_k(jnp.swapaxes(aff[:, row0:row0 + n, :e], 1, 2), cap)
        tok = idx + row0
        gates.append(jnp.swapaxes(gate, 0, 1).reshape(e, b * cap))
        flats.append(jnp.swapaxes(tok + jnp.arange(b, dtype=idx.dtype)[:, None, None] * r, 0, 1).reshape(e, b * cap))
        toks.append(tok)
    return jnp.concatenate(gates, axis=1)[..., None], jnp.concatenate(flats, axis=1), toks


def _combine_kernel(lo_ref, x_ref, tok_ref, y_ref, m_ref, g_ref, o_ref, *, final):
    i, j = pl.program_id(0), pl.program_id(1)
    tm = x_ref.shape[1]
    n_tiles = pl.num_programs(1)
    lo = lo_ref[i * (n_tiles + 1) + j]
    hi = lo_ref[i * (n_tiles + 1) + j + 1]
    rows = j * tm + lax.broadcasted_iota(jnp.int32, (tm, 1), 0)
    o_ref[0] = jnp.zeros(o_ref.shape[1:], F32)

    def chunk(c, carry):
        sel = (tok_ref[0, pl.ds(c, 1), :] == rows).astype(BF16)
        o_ref[0] += _dot(sel, y_ref[0, pl.ds(pl.multiple_of(c * PAIR_CHUNK, PAIR_CHUNK), PAIR_CHUNK), :])
        return carry

    lax.fori_loop(lo // PAIR_CHUNK, (hi + PAIR_CHUNK - 1) // PAIR_CHUNK, chunk, 0)
    x = x_ref[0] + m_ref[0][5:6] * o_ref[0]
    if final:
        x = _rms(x) * g_ref[...]
    o_ref[0] = x


def _combine(x, tok_sorted, y_sorted, lo, mods, gain, n_lat_tiles, final):
    b, t, d = x.shape
    p = tok_sorted.shape[1]
    nb = mods.shape[0] - 1
    tm = TOK_TILE
    tok = lambda i, j, lo_ref: (i, j, 0)
    whole = lambda i, j, lo_ref: (i, 0, 0)
    grid_spec = pltpu.PrefetchScalarGridSpec(
        num_scalar_prefetch=1,
        grid=(b, t // tm),
        in_specs=[pl.BlockSpec((1, tm, d), tok),
                  pl.BlockSpec((1, p // PAIR_CHUNK, PAIR_CHUNK), whole),
                  pl.BlockSpec((1, p, d), whole, pipeline_mode=pl.Buffered(1)),
                  pl.BlockSpec((1, 6, d), lambda i, j, lo_ref: (jnp.where(j >= n_lat_tiles, nb, i), 0, 0)),
                  pl.BlockSpec((1, d), lambda i, j, lo_ref: (0, 0))],
        out_specs=pl.BlockSpec((1, tm, d), tok))
    return pl.pallas_call(
        functools.partial(_combine_kernel, final=final),
        grid_spec=grid_spec,
        out_shape=jax.ShapeDtypeStruct((b, t, d), F32),
        compiler_params=_cparams("parallel", "arbitrary"),
    )(lo, x, tok_sorted.reshape(b, p // PAIR_CHUNK, PAIR_CHUNK), y_sorted, mods, gain)


def _moe(x_mid, h2, aff, segments, mods, gain, n_lat_tiles, final, w1, w3, w2, layer):
    b, r, d = h2.shape
    gate, flat, toks = _route(aff, segments)
    e, m = flat.shape
    xe = h2.reshape(b * r, d).at[flat].get(mode="promise_in_bounds")
    ye = _expert_ffn(xe, gate, w1, w3, w2, layer)
    srcs, off = [], 0
    for tok in toks:
        cap = tok.shape[-1]
        src = (jnp.arange(e, dtype=jnp.int32)[None, :, None] * m + off
               + jnp.arange(b, dtype=jnp.int32)[:, None, None] * cap + jnp.arange(cap, dtype=jnp.int32))
        srcs.append(src.reshape(b, e * cap))
        off += b * cap
    tok_all = jnp.concatenate([tok.reshape(b, -1) for tok in toks], axis=1)
    tok_sorted, src_sorted = lax.sort_key_val(tok_all, jnp.concatenate(srcs, axis=1), dimension=1)
    y_sorted = ye.reshape(e * m, d).at[src_sorted].get(mode="promise_in_bounds")
    starts = jnp.arange(r // TOK_TILE + 1, dtype=jnp.int32) * TOK_TILE
    lo = jnp.sum((tok_sorted[:, None, :] < starts[None, :, None]).astype(jnp.int32), axis=-1)
    return _combine(x_mid, tok_sorted, y_sorted, lo.reshape(-1), mods, gain, n_lat_tiles, final)


def _proj_odd_kernel(x_ref, ml_ref, mc_ref, g_ref, w_ref, qn_ref, kvn_ref, wq_ref, wkv_ref, cos_ref, sin_ref,
                     q_ref, k_ref, v_ref, rq_ref, rk_ref, rv_ref, rg_ref, *, q_scale, n_lat):
    h = _mixer_input(x_ref[0], g_ref[...], ml_ref[0], mc_ref[0], n_lat)
    acc = _dot(h.astype(BF16), w_ref[...])
    cos, sin = cos_ref[...], sin_ref[...]
    c0 = MLA_Q_RANK
    c1 = c0 + MLA_KV_RANK
    cq = acc[:, :c0]
    ckv = acc[:, c0:c1]
    kr = acc[:, c1:c1 + LANES] * cos + acc[:, c1 + LANES:c1 + 2 * LANES] * sin
    c2 = c1 + 2 * LANES
    nq = RET_HEADS * RET_QK_DIM
    nv = RET_HEADS * RET_V_DIM
    rq_ref[0] = acc[:, c2:c2 + nq].astype(BF16)
    rk_ref[0] = acc[:, c2 + nq:c2 + 2 * nq].astype(BF16)
    rv_ref[0] = acc[:, c2 + 2 * nq:c2 + 2 * nq + nv].astype(BF16)
    rg_ref[0] = acc[:, c2 + 2 * nq + nv:]

    nqk = MLA_HEADS * LANES
    qq = _dot((_rms(cq) * qn_ref[...]).astype(BF16), wq_ref[...])
    q = qq[:, :nqk] * _tile_lanes(cos, MLA_HEADS) + qq[:, nqk:] * _tile_lanes(sin, MLA_HEADS)
    q_ref[0] = (q * q_scale).astype(BF16)
    kv = _dot((_rms(ckv) * kvn_ref[...]).astype(BF16), wkv_ref[...])
    k_ref[0] = (kv[:, :nqk] + _tile_lanes(kr, MLA_HEADS)).astype(BF16)
    v_ref[0] = kv[:, nqk:].astype(BF16)


def _proj_odd(xa, mods, gain, w, q_norm, kv_norm, wq, wkv, cos, sin, n_lat):
    b, t, d = xa.shape
    nb = mods.shape[0] - 1
    tm = _proj_in_tile(t)
    tok = lambda i, j: (i, j, 0)
    const = lambda i, j: (0, 0)
    widths = (MLA_HEADS * LANES, MLA_HEADS * LANES, MLA_HEADS * MLA_V_DIM,
              RET_HEADS * RET_QK_DIM, RET_HEADS * RET_QK_DIM, RET_HEADS * RET_V_DIM, RET_HEADS * RET_V_DIM)
    dtypes = (BF16,) * 6 + (F32,)
    kern = functools.partial(_proj_odd_kernel, q_scale=float((MLA_NOPE_DIM + MLA_ROPE_DIM) ** -0.5) * LOG2E,
                             n_lat=n_lat)
    return pl.pallas_call(
        kern,
        grid=(b, t // tm),
        in_specs=[pl.BlockSpec((1, tm, d), tok),
                  pl.BlockSpec((1, 6, d), lambda i, j: (i, 0, 0)),
                  pl.BlockSpec((1, 6, d), lambda i, j: (nb, 0, 0)),
                  pl.BlockSpec((1, d), const),
                  pl.BlockSpec(w.shape, const),
                  pl.BlockSpec(q_norm.shape, const),
                  pl.BlockSpec(kv_norm.shape, const),
                  pl.BlockSpec(wq.shape, const),
                  pl.BlockSpec(wkv.shape, const),
                  pl.BlockSpec((tm, LANES), lambda i, j: (j, 0)),
                  pl.BlockSpec((tm, LANES), lambda i, j: (j, 0))],
        out_specs=[pl.BlockSpec((1, tm, wd), tok) for wd in widths],
        out_shape=[jax.ShapeDtypeStruct((b, t, wd), dt) for wd, dt in zip(widths, dtypes)],
        compiler_params=_cparams("parallel", "arbitrary"),
    )(xa, mods, mods, gain, w, q_norm, kv_norm, wq, wkv, cos, sin)


def _mla_kernel(q_ref, k_ref, v_ref, o_ref, vx_ref):
    @pl.when(pl.program_id(2) == 0)
    def _():
        _fill_v_ones(vx_ref, v_ref)

    o0, o1 = _two_stream_attention(q_ref[0, :, :LANES], [(k_ref[0, :, :LANES], None)],
                                   q_ref[0, :, LANES:], [(k_ref[0, :, LANES:], None)], [vx_ref[...]])
    lane = lax.broadcasted_iota(jnp.int32, (1, LANES), 1)
    o_ref[0] = jnp.where(lane < MLA_V_DIM, o0, o1).astype(BF16)


def _mla_attn(q, k, v, n_lat, tq):
    b, t, _ = k.shape
    npairs = MLA_HEADS // 2
    return pl.pallas_call(
        _mla_kernel,
        grid=(b, npairs, n_lat // tq),
        in_specs=[pl.BlockSpec((1, tq, 2 * LANES), lambda i, h, j: (i, j, h)),
                  pl.BlockSpec((1, t, 2 * LANES), lambda i, h, j: (i, 0, h)),
                  pl.BlockSpec((1, t, LANES), lambda i, h, j: (i, 0, h))],
        out_specs=pl.BlockSpec((1, tq, LANES), lambda i, h, j: (i, j, h)),
        out_shape=jax.ShapeDtypeStruct((b, n_lat, npairs * LANES), BF16),
        scratch_shapes=[pltpu.VMEM((t, 2 * LANES), BF16)],
        compiler_params=_cparams("parallel", "parallel", "arbitrary"),
    )(q, k, v)


def _retention_kernel(dl_ref, q_ref, k_ref, v_ref, g_ref, o_ref, f_ref, b_ref, st_ref, *, n_lat):
    c = RET_CHUNK
    t = k_ref.shape[1]
    n_chunks = n_lat // c
    n_ctx = (t - n_lat) // c
    hp = pl.program_id(1)
    lane = lax.broadcasted_iota(jnp.int32, (1, LANES), 1)
    pos_r = lax.broadcasted_iota(jnp.int32, (c, 1), 0).astype(F32)
    ii = lax.broadcasted_iota(jnp.int32, (c, c), 0)
    jj = lax.broadcasted_iota(jnp.int32, (c, c), 1)
    rel = (ii - jj).astype(F32)
    dl = dl_ref[...]
    hsel = lax.broadcasted_iota(jnp.int32, (1, RET_HEADS), 1)

    chains = []
    for hh in range(2):
        hmask = (lane >= hh * RET_QK_DIM) & (lane < (hh + 1) * RET_QK_DIM)
        for direction in range(2):
            logit = jnp.sum(jnp.where(hsel == 2 * hp + hh, dl[direction:direction + 1], 0.0),
                            axis=-1, keepdims=True)
            lg = jnp.minimum(logit, 0.0) - jnp.log(1.0 + jnp.exp(-jnp.abs(logit)))
            if direction == 0:
                mask = rel >= 0
                d_in = jnp.where(mask, jnp.exp(lg * jnp.where(mask, rel, 0.0)), 0.0)
                d_q = jnp.exp(lg * (pos_r + 1.0))
                d_k = jnp.exp(lg * (c - 1.0 - pos_r))
            else:
                mask = rel < 0
                d_in = jnp.where(mask, jnp.exp(lg * jnp.where(mask, -rel, 0.0)), 0.0)
                d_q = jnp.exp(lg * (c - pos_r))
                d_k = jnp.exp(lg * pos_r)
            chains.append(dict(idx=2 * hh + direction, v0=hh * RET_V_DIM, direction=direction, hmask=hmask,
                               d_in=d_in, d_q=d_q, d_k=d_k, d_chunk=jnp.exp(lg * c)))

    st_ref[...] = jnp.zeros(st_ref.shape, F32)

    def kv_update(ch, start):
        kb = k_ref[0, pl.ds(start, c), :].astype(F32)
        vb = v_ref[0, pl.ds(start, c), ch["v0"]:ch["v0"] + RET_V_DIM]
        kd = jnp.where(ch["hmask"], kb * ch["d_k"], 0.0).astype(BF16)
        st_ref[ch["idx"]] = ch["d_chunk"] * st_ref[ch["idx"]] + _dot_tn(kd, vb)

    for i in range(n_ctx):
        for ch in chains:
            ci = i if ch["direction"] == 0 else n_ctx - 1 - i
            kv_update(ch, n_lat + ci * c)

    def lat_step(i, carry):
        starts = [pl.multiple_of((i if ch["direction"] == 0 else n_chunks - 1 - i) * c, c) for ch in chains]
        qms, avs, crs = [], [], []
        for ch, start in zip(chains, starts):
            qb = q_ref[0, pl.ds(start, c), :]
            qm = jnp.where(ch["hmask"], qb, jnp.zeros_like(qb))
            qms.append(qm)
            avs.append(_dot_nt(qm, k_ref[0, pl.ds(start, c), :]))
        for ch, qm in zip(chains, qms):
            qd = (qm.astype(F32) * ch["d_q"]).astype(BF16)
            crs.append(_dot(qd, st_ref[ch["idx"]].astype(BF16)))
        for ch, start, a, cross in zip(chains, starts, avs, crs):
            vb = v_ref[0, pl.ds(start, c), ch["v0"]:ch["v0"] + RET_V_DIM]
            res = _dot((a * ch["d_in"]).astype(BF16), vb) + cross
            dst = f_ref if ch["direction"] == 0 else b_ref
            dst[pl.ds(start, c), ch["v0"]:ch["v0"] + RET_V_DIM] = res
        for ch, start in zip(chains, starts):
            kv_update(ch, start)
        return carry

    lax.fori_loop(0, n_chunks, lat_step, 0)

    def finish(i, carry):
        start = pl.multiple_of(i * c, c)
        for hh in range(2):
            v0 = hh * RET_V_DIM
            r = _rms(f_ref[pl.ds(start, c), v0:v0 + RET_V_DIM] + b_ref[pl.ds(start, c), v0:v0 + RET_V_DIM])
            gate = _silu(g_ref[0, pl.ds(start, c), v0:v0 + RET_V_DIM])
            o_ref[0, pl.ds(start, c), v0:v0 + RET_V_DIM] = (r * gate).astype(BF16)
        return carry

    lax.fori_loop(0, n_chunks, finish, 0)


def _retention(rq, rk, rv, rg, decay_logit, n_lat):
    b, t, _ = rq.shape
    npairs = RET_HEADS // 2
    kern = functools.partial(_retention_kernel, n_lat=n_lat)
    return pl.pallas_call(
        kern,
        grid=(b, npairs),
        in_specs=[pl.BlockSpec(decay_logit.shape, lambda i, h: (0, 0)),
                  pl.BlockSpec((1, n_lat, LANES), lambda i, h: (i, 0, h)),
                  pl.BlockSpec((1, t, LANES), lambda i, h: (i, 0, h)),
                  pl.BlockSpec((1, t, 2 * RET_V_DIM), lambda i, h: (i, 0, h)),
                  pl.BlockSpec((1, n_lat, 2 * RET_V_DIM), lambda i, h: (i, 0, h))],
        out_specs=pl.BlockSpec((1, n_lat, 2 * RET_V_DIM), lambda i, h: (i, 0, h)),
        out_shape=jax.ShapeDtypeStruct((b, n_lat, RET_HEADS * RET_V_DIM), BF16),
        scratch_shapes=[pltpu.VMEM((n_lat, 2 * RET_V_DIM), F32), pltpu.VMEM((n_lat, 2 * RET_V_DIM), F32),
                        pltpu.VMEM((4, LANES, RET_V_DIM), F32)],
        compiler_params=_cparams("parallel", "arbitrary"),
    )(decay_logit, rq, rk, rv, rg)


def _rope_tables(n_lat, n_ctx, rot_dim, lane0):
    t = np.arange(n_lat)
    rows = (t // GRID_W).astype(np.float32)
    cols = (t % GRID_W).astype(np.float32)
    m = rot_dim // 4
    freqs = jnp.asarray(ROPE_BASE, F32) ** (-jnp.arange(m, dtype=F32) / m)
    ang_r = jnp.asarray(rows)[:, None] * freqs
    ang_c = jnp.asarray(cols)[:, None] * freqs
    cos = jnp.concatenate([jnp.cos(ang_r)] * 2 + [jnp.cos(ang_c)] * 2, axis=-1)
    sin = jnp.concatenate([-jnp.sin(ang_r), jnp.sin(ang_r), -jnp.sin(ang_c), jnp.sin(ang_c)], axis=-1)
    reps = (LANES - lane0) // rot_dim if lane0 == 0 else 1
    cos = jnp.concatenate([jnp.ones((n_lat, lane0), F32)] + [cos] * reps
                          + [jnp.ones((n_lat, LANES - lane0 - reps * rot_dim), F32)], axis=-1)
    sin = jnp.concatenate([jnp.zeros((n_lat, lane0), F32)] + [sin] * reps
                          + [jnp.zeros((n_lat, LANES - lane0 - reps * rot_dim), F32)], axis=-1)
    cos = jnp.concatenate([cos, jnp.ones((n_ctx, LANES), F32)], axis=0)
    sin = jnp.concatenate([sin, jnp.zeros((n_ctx, LANES), F32)], axis=0)
    return cos, sin


def _rot_partner(w, rot_dim):
    k, n = w.shape
    q = rot_dim // 4
    return w.reshape(k, n // rot_dim, 2, 2, q)[:, :, :, ::-1, :].reshape(k, n)


def _pad_cols(w, lane0, width=LANES):
    k, n = w.shape
    return jnp.concatenate([jnp.zeros((k, lane0), w.dtype), w, jnp.zeros((k, width - lane0 - n), w.dtype)], axis=-1)


def _even_weights(w_in):
    n_rope = 2 * DIFF_HEADS * 2 * DIFF_HEAD_DIM
    dw = DIFF_HEADS * 2 * DIFF_HEAD_DIM
    nw = NA_HEADS * NA_HEAD_DIM
    scale = jnp.concatenate([jnp.full((dw,), DIFF_HEAD_DIM ** -0.5, F32), jnp.ones((2 * dw,), F32),
                             jnp.full((nw,), NA_HEAD_DIM ** -0.5, F32), jnp.ones((2 * nw,), F32)])
    w = w_in * scale
    return jnp.concatenate([w, _rot_partner(w[:, :n_rope], DIFF_HEAD_DIM)], axis=-1).astype(BF16), n_rope


def _odd_weights(w_in, w_uq, w_ukv):
    c0 = MLA_Q_RANK
    c1 = c0 + MLA_KV_RANK
    c2 = c1 + MLA_ROPE_DIM
    nq = RET_HEADS * RET_QK_DIM
    kr = w_in[:, c1:c2]
    w = jnp.concatenate([w_in[:, :c1],
                         _pad_cols(kr, MLA_NOPE_DIM), _pad_cols(_rot_partner(kr, MLA_ROPE_DIM), MLA_NOPE_DIM),
                         w_in[:, c2:c2 + nq], w_in[:, c2 + nq:c2 + 2 * nq] * (RET_QK_DIM ** -0.5),
                         w_in[:, c2 + 2 * nq:]], axis=-1).astype(BF16)
    r = w_uq.shape[0]
    uq = w_uq.reshape(r, MLA_HEADS, MLA_NOPE_DIM + MLA_ROPE_DIM)
    pad = jnp.zeros((r, MLA_HEADS, LANES - MLA_NOPE_DIM - MLA_ROPE_DIM), F32)
    uq_rot = _rot_partner(uq[:, :, MLA_NOPE_DIM:].reshape(r, -1), MLA_ROPE_DIM).reshape(r, MLA_HEADS, MLA_ROPE_DIM)
    wq = jnp.concatenate([uq, pad], axis=-1).reshape(r, -1)
    wq_rot = jnp.concatenate([jnp.zeros_like(uq[:, :, :MLA_NOPE_DIM]), uq_rot, pad], axis=-1).reshape(r, -1)
    rk = w_ukv.shape[0]
    ukv = w_ukv.reshape(rk, MLA_HEADS, MLA_NOPE_DIM + MLA_V_DIM)
    wk = jnp.concatenate([ukv[:, :, :MLA_NOPE_DIM], jnp.zeros((rk, MLA_HEADS, LANES - MLA_NOPE_DIM), F32)],
                         axis=-1).reshape(rk, -1)
    wv = ukv[:, :, MLA_NOPE_DIM:].reshape(rk, -1)
    return w, jnp.concatenate([wq, wq_rot], axis=-1).astype(BF16), jnp.concatenate([wk, wv], axis=-1).astype(BF16)


def _na_bias(rpb, n_lat):
    pat_of_group, idx_r, idx_c, valid = _na_patterns(n_lat // GRID_W)
    g = NA_GROUP_ROWS
    span = g + NA_WIN_ROWS - 1
    n_pat = idx_r.shape[0]
    col_sel = (idx_c[0, :GRID_W, :GRID_W, None] == np.arange(2 * NA_WIN_COLS - 1)).astype(np.float32)
    row_idx = idx_r.reshape(n_pat, g, GRID_W, span, GRID_W)[:, :, 0, :, 0]
    row_sel = (row_idx[..., None] == np.arange(2 * NA_WIN_ROWS - 1)).astype(np.float32)
    cols = jnp.einsum('hrc,qkc->hrqk', rpb, col_sel, precision=lax.Precision.HIGHEST)
    bias = jnp.einsum('pijr,hrqk->hpiqjk', row_sel, cols, precision=lax.Precision.HIGHEST)
    bias = bias.reshape(rpb.shape[0], n_pat, g * GRID_W, span * GRID_W)
    return jnp.where(valid[None], bias * LOG2E, NEG_BIG), pat_of_group


def _router_weights(w_router):
    return _pad_cols(w_router, 0)


def kernel(x, c, ctx, c_ctx, ada_w, ada_b, norm_mix, norm_ffn, final_norm, even_w_in, even_w_out,
           diff_lambda, diff_subln, na_rpb, odd_w_in, odd_w_out, mla_q_norm, mla_w_uq, mla_kv_norm,
           mla_w_ukv, ret_decay_logit, moe_router, moe_w1, moe_w3, moe_w2):
    b, n_lat, d = x.shape
    n_ctx = ctx.shape[1]
    depth = ada_w.shape[0]
    n_lat_tiles = n_lat // TOK_TILE
    n_tiles = (n_lat + n_ctx) // TOK_TILE

    cond = jnp.concatenate([c, c_ctx[None], jnp.zeros((7, d), F32)], axis=0)
    mods_all = _mods(cond, ada_w, ada_b)[:, :b + 1].reshape(depth, b + 1, 6, d)
    xa = jnp.concatenate([x, ctx], axis=1)

    for l in range(depth):
        mods = mods_all[l]
        need_ctx = l < depth - 1
        i = l // 2
        gain_mix = norm_mix[l][None]
        if l % 2 == 0:
            w, n_rope = _even_weights(even_w_in[i])
            cos, sin = _rope_tables(n_lat, n_ctx, DIFF_HEAD_DIM, 0)
            qkv = _proj_even(xa, mods, gain_mix, w, cos, sin, n_lat, n_rope, even_w_in.shape[-1])
            lam_init = 0.8 - 0.6 * math.exp(-0.3 * l)
            t_all = n_lat + n_ctx
            mix_a = _diff_attn(qkv, diff_lambda[i], diff_subln[i][None], lam_init, 0, n_lat, ATTN_Q_TILE, 0, t_all)
            mix_a_ctx = _diff_attn(qkv, diff_lambda[i], diff_subln[i][None], lam_init, n_lat, n_ctx, n_ctx,
                                   n_lat, n_ctx)
            bias, offsets = _na_bias(na_rpb[i], n_lat)
            mix_b = _na_attn(qkv, bias, n_lat, offsets, 3 * DIFF_HEADS * 2 * DIFF_HEAD_DIM)
            w_out = even_w_out[i].astype(BF16)
        else:
            w, wq, wkv = _odd_weights(odd_w_in[i], mla_w_uq[i], mla_w_ukv[i])
            cos, sin = _rope_tables(n_lat, n_ctx, MLA_ROPE_DIM, MLA_NOPE_DIM)
            q, k, v, rq, rk, rv, rg = _proj_odd(xa, mods, gain_mix, w, mla_q_norm[i][None], mla_kv_norm[i][None],
                                                wq, wkv, cos, sin, n_lat)
            mix_a = _mla_attn(q, k, v, n_lat, ATTN_Q_TILE)
            mix_a_ctx = mix_a
            mix_b = _retention(rq, rk, rv, rg, ret_decay_logit[i], n_lat)
            w_out = odd_w_out[i].astype(BF16)

        tiles = n_tiles if need_ctx else n_lat_tiles
        x_mid, h2, aff = _proj_out(xa, mix_a, mix_a_ctx, mix_b, w_out, mods, norm_ffn[l][None],
                                   _router_weights(moe_router[l]), tiles, n_lat_tiles)
        segments = ((0, n_lat), (n_lat, n_ctx)) if need_ctx else ((0, n_lat),)
        xa = _moe(x_mid, h2, aff, segments, mods, final_norm[None], n_lat_tiles, l == depth - 1,
                  moe_w1, moe_w3, moe_w2, l)
    return xa
```

```python
import functools
import math

import jax
import jax.numpy as jnp
import numpy as np
from jax import lax
from jax.experimental import pallas as pl
from jax.experimental.pallas import tpu as pltpu

GRID_W = 64
ROPE_BASE = 10000.0
RMS_EPS = 1e-6
DIFF_HEADS = 4
DIFF_HEAD_DIM = 64
NA_HEADS = 8
NA_HEAD_DIM = 64
NA_WIN_ROWS = 8
NA_WIN_COLS = 16
MLA_HEADS = 8
MLA_Q_RANK = 256
MLA_KV_RANK = 128
MLA_NOPE_DIM = 64
MLA_ROPE_DIM = 32
MLA_V_DIM = 64
RET_HEADS = 4
RET_QK_DIM = 64
RET_V_DIM = 128
RET_CHUNK = 128
N_EXPERTS = 16
EC_CAPACITY_FACTOR = 2

LANES = 128
VMEM_LIMIT = 56 * 1024 * 1024
NEG_BIG = -1e30
TOK_TILE = 256
ROW_TILES = 8
ATTN_Q_TILE = 512
FFN_ROW_TILES = 8
FFN_HIDDEN_BLOCK = 512
NA_GROUP_ROWS = 4
NA_GROUPS_PER_STEP = 2
PAIR_CHUNK = 256
LOG2E = math.log2(math.e)

BF16 = jnp.bfloat16
F32 = jnp.float32


def _cparams(*sem):
    return pltpu.CompilerParams(dimension_semantics=sem, vmem_limit_bytes=VMEM_LIMIT)


def _dot(a, b):
    return jnp.dot(a, b, preferred_element_type=F32)


def _dot_nt(a, b):
    return lax.dot_general(a, b, (((1,), (1,)), ((), ())), preferred_element_type=F32)


def _dot_tn(a, b):
    return lax.dot_general(a, b, (((0,), (0,)), ((), ())), preferred_element_type=F32)


def _rms(x):
    return x * lax.rsqrt(jnp.mean(x * x, axis=-1, keepdims=True) + RMS_EPS)


def _silu(x):
    return x * (1.0 / (1.0 + jnp.exp(-x)))


def _tile_lanes(t, n):
    return jnp.concatenate([t] * n, axis=-1)


def _mods_kernel(c_ref, w_ref, b_ref, o_ref):
    o_ref[0] = _dot(_silu(c_ref[...]), w_ref[0]) + b_ref[0]


def _mods(cond, ada_w, ada_b):
    depth, d, n = ada_w.shape
    r = cond.shape[0]
    tn = 1536
    return pl.pallas_call(
        _mods_kernel,
        grid=(depth, n // tn),
        in_specs=[pl.BlockSpec((r, d), lambda l, j: (0, 0)),
                  pl.BlockSpec((1, d, tn), lambda l, j: (l, 0, j)),
                  pl.BlockSpec((1, 1, tn), lambda l, j: (l, 0, j))],
        out_specs=pl.BlockSpec((1, r, tn), lambda l, j: (l, 0, j)),
        out_shape=jax.ShapeDtypeStruct((depth, r, n), F32),
        compiler_params=_cparams("arbitrary", "arbitrary"),
    )(cond, ada_w, ada_b.reshape(depth, 1, n))


def _normed(x, gain, mods, shift_row, scale_row):
    return _rms(x) * gain * (1.0 + mods[scale_row:scale_row + 1]) + mods[shift_row:shift_row + 1]


def _mixer_input(x, gain, mods_lat, mods_ctx, n_lat):
    tm = x.shape[0]
    rows = pl.program_id(1) * tm + lax.broadcasted_iota(jnp.int32, (tm, 1), 0)
    is_ctx = rows >= n_lat
    scale = jnp.where(is_ctx, mods_ctx[1:2], mods_lat[1:2])
    shift = jnp.where(is_ctx, mods_ctx[0:1], mods_lat[0:1])
    return _rms(x) * gain * (1.0 + scale) + shift


def _proj_even_kernel(x_ref, ml_ref, mc_ref, g_ref, w_ref, cos_ref, sin_ref, o_ref, *, n_rope, n_out, n_lat):
    na_q0 = n_rope + n_rope // 2
    na_q1 = na_q0 + NA_HEADS * NA_HEAD_DIM
    h = _mixer_input(x_ref[0], g_ref[...], ml_ref[0], mc_ref[0], n_lat)
    acc = _dot(h.astype(BF16), w_ref[...])
    reps = n_rope // LANES
    cos = _tile_lanes(cos_ref[...], reps)
    sin = _tile_lanes(sin_ref[...], reps)
    roped = acc[:, :n_rope] * cos + acc[:, n_out:] * sin
    n_q = n_rope // 2
    o_ref[0, :, :n_q] = (roped[:, :n_q] * LOG2E).astype(BF16)
    o_ref[0, :, n_q:n_rope] = roped[:, n_q:].astype(BF16)
    o_ref[0, :, n_rope:na_q0] = acc[:, n_rope:na_q0].astype(BF16)
    o_ref[0, :, na_q0:na_q1] = (acc[:, na_q0:na_q1] * LOG2E).astype(BF16)
    o_ref[0, :, na_q1:] = acc[:, na_q1:n_out].astype(BF16)


def _row_tile(t):
    tm = t // ROW_TILES
    assert tm * ROW_TILES == t and tm % 16 == 0
    return tm


def _proj_even(xa, mods, gain, w, cos, sin, n_lat, n_rope, n_out):
    b, t, d = xa.shape
    nb = mods.shape[0] - 1
    tm = _row_tile(t)
    kern = functools.partial(_proj_even_kernel, n_rope=n_rope, n_out=n_out, n_lat=n_lat)
    return pl.pallas_call(
        kern,
        grid=(b, t // tm),
        in_specs=[pl.BlockSpec((1, tm, d), lambda i, j: (i, j, 0)),
                  pl.BlockSpec((1, 6, d), lambda i, j: (i, 0, 0)),
                  pl.BlockSpec((1, 6, d), lambda i, j: (nb, 0, 0)),
                  pl.BlockSpec((1, d), lambda i, j: (0, 0)),
                  pl.BlockSpec(w.shape, lambda i, j: (0, 0)),
                  pl.BlockSpec((tm, LANES), lambda i, j: (j, 0)),
                  pl.BlockSpec((tm, LANES), lambda i, j: (j, 0))],
        out_specs=pl.BlockSpec((1, tm, n_out), lambda i, j: (i, j, 0)),
        out_shape=jax.ShapeDtypeStruct((b, t, n_out), BF16),
        compiler_params=_cparams("parallel", "arbitrary"),
    )(xa, mods, mods, gain, w, cos, sin)


def _fill_v_ones(vx_ref, v_ref):
    vx_ref[:, :LANES] = v_ref[0]
    vx_ref[:, LANES:] = jnp.ones((vx_ref.shape[0], LANES), BF16)


def _weights(s, m):
    return jnp.exp2(s - m).astype(BF16)


def _rowmax(*ss):
    m = jnp.max(ss[0], axis=-1, keepdims=True)
    for s in ss[1:]:
        m = jnp.maximum(m, jnp.max(s, axis=-1, keepdims=True))
    return m


def _normalised(ox):
    return ox[:, :LANES] / ox[:, LANES:LANES + 1]


def _scores(q, segs):
    return [_dot_nt(q, k) if bias is None else _dot_nt(q, k) + bias for k, bias in segs]


def _pv(ps, vxs):
    o = _dot(ps[0], vxs[0])
    for p, vx in zip(ps[1:], vxs[1:]):
        o = o + _dot(p, vx)
    return o


def _multi_stream_attention(streams):
    n = len(streams)
    s, m, p, o = [None] * n, [None] * n, [None] * n, [None] * n
    for t in range(n + 2):
        if t < n:
            s[t] = _scores(streams[t][0], streams[t][1])
        if 0 <= t - 2 < n:
            o[t - 2] = _normalised(_pv(p[t - 2], streams[t - 2][2]))
        if 0 <= t - 1 < n:
            p[t - 1] = [_weights(x, m[t - 1]) for x in s[t - 1]]
        if t < n:
            m[t] = _rowmax(*s[t])
    return o


def _two_stream_attention(qa, segs_a, qb, segs_b, vxs):
    return _multi_stream_attention([(qa, segs_a, vxs), (qb, segs_b, vxs)])


def _diff_attn_kernel(lam_ref, g_ref, q_ref, k_ref, v_ref, o_ref, vx_ref, *, lam_init, n_lat):
    qi = pl.program_id(2)
    tq, t = q_ref.shape[1], k_ref.shape[1]

    @pl.when(qi == 0)
    def _():
        _fill_v_ones(vx_ref, v_ref)

    lp = lam_ref[...]
    s1 = jnp.sum(lp[0:1] * lp[1:2], axis=-1, keepdims=True)
    s2 = jnp.sum(lp[2:3] * lp[3:4], axis=-1, keepdims=True)
    lam = jnp.exp(s1) - jnp.exp(s2) + lam_init
    q = q_ref[0]
    lane = lax.broadcasted_iota(jnp.int32, (1, LANES), 1)
    q1 = jnp.where(lane < DIFF_HEAD_DIM, q, jnp.zeros_like(q))
    q2 = jnp.where(lane >= DIFF_HEAD_DIM, q, jnp.zeros_like(q))

    def attend(bias):
        k = k_ref[0]
        o1, o2 = _two_stream_attention(q1, [(k, bias)], q2, [(k, bias)], [vx_ref[...]])
        o = o1 - lam * o2
        o_ref[0] = (_rms(o) * g_ref[...] * (1.0 - lam_init)).astype(BF16)

    has_ctx_rows = (qi + 1) * tq > n_lat

    @pl.when(jnp.logical_not(has_ctx_rows))
    def _():
        attend(None)

    @pl.when(has_ctx_rows)
    def _():
        rows = qi * tq + lax.broadcasted_iota(jnp.int32, (tq, 1), 0)
        keys = lax.broadcasted_iota(jnp.int32, (1, t), 1)
        attend(jnp.where((rows >= n_lat) & (keys < n_lat), NEG_BIG, 0.0))


def _diff_attn(qkv, lam_params, subln, lam_init, n_lat):
    b, t, _ = qkv.shape
    nh = DIFF_HEADS
    tq = _row_tile(t)
    return pl.pallas_call(
        functools.partial(_diff_attn_kernel, lam_init=lam_init, n_lat=n_lat),
        grid=(b, nh, t // tq),
        in_specs=[pl.BlockSpec(lam_params.shape, lambda i, h, j: (0, 0)),
                  pl.BlockSpec((1, LANES), lambda i, h, j: (0, 0)),
                  pl.BlockSpec((1, tq, LANES), lambda i, h, j: (i, j, h)),
                  pl.BlockSpec((1, t, LANES), lambda i, h, j: (i, 0, nh + h)),
                  pl.BlockSpec((1, t, LANES), lambda i, h, j: (i, 0, 2 * nh + h))],
        out_specs=pl.BlockSpec((1, tq, LANES), lambda i, h, j: (i, j, h)),
        out_shape=jax.ShapeDtypeStruct((b, t, nh * LANES), BF16),
        scratch_shapes=[pltpu.VMEM((t, 2 * LANES), BF16)],
        compiler_params=_cparams("parallel", "parallel", "arbitrary"),
    )(lam_params, subln, qkv, qkv, qkv)


def _na_patterns(rows):
    g = NA_GROUP_ROWS
    span = g + NA_WIN_ROWS - 1
    nq, nk = g * GRID_W, span * GRID_W
    qi = np.arange(nq)[:, None]
    ki = np.arange(nk)[None, :]
    qcol, kcol = qi % GRID_W, ki % GRID_W
    cs = np.clip(qcol - NA_WIN_COLS // 2, 0, GRID_W - NA_WIN_COLS)
    col_ok = (kcol >= cs) & (kcol < cs + NA_WIN_COLS)
    idx_c = np.clip(kcol - qcol + NA_WIN_COLS - 1, 0, 2 * NA_WIN_COLS - 2) + 0 * qi
    seen, pat_of_group, idx_r, valid = {}, [], [], []
    for r0 in range(0, rows, g):
        ks = min(max(r0 - NA_WIN_ROWS // 2, 0), rows - span)
        qrow = r0 + qi // GRID_W
        krow = ks + ki // GRID_W
        rs = np.clip(qrow - NA_WIN_ROWS // 2, 0, rows - NA_WIN_ROWS)
        ok = (krow >= rs) & (krow < rs + NA_WIN_ROWS) & col_ok
        ir = np.clip(krow - qrow + NA_WIN_ROWS - 1, 0, 2 * NA_WIN_ROWS - 2) + 0 * kcol
        key = (ok.tobytes(), np.where(ok, ir, 0).tobytes())
        if key not in seen:
            seen[key] = len(idx_r)
            idx_r.append(ir)
            valid.append(ok)
        pat_of_group.append(seen[key])
    n_pat = len(idx_r)
    return tuple(pat_of_group), np.stack(idx_r), np.stack([idx_c] * n_pat), np.stack(valid)


def _na_kernel(q_ref, k_ref, v_ref, bias_ref, o_ref, vx_ref, *, n_lat, pat_of_group):
    g_rows = NA_GROUP_ROWS
    span = g_rows + NA_WIN_ROWS - 1
    rows = n_lat // GRID_W
    nq, nk = g_rows * GRID_W, span * GRID_W
    lane = lax.broadcasted_iota(jnp.int32, (1, LANES), 1)
    head_masks = (lane < NA_HEAD_DIM, lane >= NA_HEAD_DIM)
    _fill_v_ones(vx_ref, v_ref)
    kc = k_ref[0, n_lat:, :]
    vxc = vx_ref[n_lat:, :]

    def both_heads(blocks):
        streams = []
        for q, segs, vxs in blocks:
            for hh, hm in enumerate(head_masks):
                streams.append((jnp.where(hm, q, jnp.zeros_like(q)),
                                [(k, None if bi is None else bias_ref[hh, bi]) for k, bi in segs], vxs))
        outs = _multi_stream_attention(streams)
        return [jnp.where(head_masks[0], outs[2 * i], outs[2 * i + 1]).astype(BF16) for i in range(len(blocks))]

    def group_block(g):
        r0 = g * g_rows
        ks = jnp.clip(r0 - NA_WIN_ROWS // 2, 0, rows - span)
        common = max(set(pat_of_group), key=pat_of_group.count)
        pat = jnp.int32(common)
        for gi, p in enumerate(pat_of_group):
            if p != common:
                pat = jnp.where(g == gi, p, pat)
        qg = q_ref[0, pl.ds(pl.multiple_of(g * nq, nq), nq), :]
        kw = k_ref[0, pl.ds(pl.multiple_of(ks * GRID_W, GRID_W), nk), :]
        vxw = vx_ref[pl.ds(pl.multiple_of(ks * GRID_W, GRID_W), nk), :]
        return qg, [(kw, pat), (kc, None)], [vxw, vxc]

    def group_step(i, carry):
        gs = [i * NA_GROUPS_PER_STEP + u for u in range(NA_GROUPS_PER_STEP)]
        for g, out in zip(gs, both_heads([group_block(g) for g in gs])):
            o_ref[0, pl.ds(pl.multiple_of(g * nq, nq), nq), :] = out
        return carry

    lax.fori_loop(0, rows // g_rows // NA_GROUPS_PER_STEP, group_step, 0)

    o_ref[0, n_lat:, :] = both_heads([(q_ref[0, n_lat:, :], [(kc, None)], [vxc])])[0]


def _na_attn(qkv, bias, n_lat, pat_of_group, col0):
    b, t, _ = qkv.shape
    npairs = NA_HEADS // 2
    cb = col0 // LANES
    kern = functools.partial(_na_kernel, n_lat=n_lat, pat_of_group=pat_of_group)
    return pl.pallas_call(
        kern,
        grid=(b, npairs),
        in_specs=[pl.BlockSpec((1, t, LANES), lambda i, h: (i, 0, cb + h)),
                  pl.BlockSpec((1, t, LANES), lambda i, h: (i, 0, cb + npairs + h)),
                  pl.BlockSpec((1, t, LANES), lambda i, h: (i, 0, cb + 2 * npairs + h)),
                  pl.BlockSpec((2,) + bias.shape[1:], lambda i, h: (h, 0, 0, 0))],
        out_specs=pl.BlockSpec((1, t, LANES), lambda i, h: (i, 0, h)),
        out_shape=jax.ShapeDtypeStruct((b, t, npairs * LANES), BF16),
        scratch_shapes=[pltpu.VMEM((t, 2 * LANES), BF16)],
        compiler_params=_cparams("parallel", "arbitrary"),
    )(qkv, qkv, qkv, bias)


def _proj_out_kernel(x_ref, a_ref, b_ref, wa_ref, wb_ref, ml_ref, mc_ref, g_ref, wr_ref, xo_ref, h_ref, aff_ref,
                     *, n_lat):
    tm = x_ref.shape[1]
    is_ctx = pl.program_id(1) * tm + lax.broadcasted_iota(jnp.int32, (tm, 1), 0) >= n_lat
    ml, mc = ml_ref[0], mc_ref[0]
    mod = lambda r: jnp.where(is_ctx, mc[r:r + 1], ml[r:r + 1])
    y = _dot(a_ref[0], wa_ref[...]) + _dot(b_ref[0], wb_ref[...])
    x = x_ref[0] + mod(2) * y
    xo_ref[0] = x
    h2 = _rms(x) * g_ref[...] * (1.0 + mod(4)) + mod(3)
    h_hi = h2.astype(BF16)
    h_ref[0] = h_hi
    h_lo = (h2 - h_hi.astype(F32)).astype(BF16)
    wr = wr_ref[...]
    w_hi = wr.astype(BF16)
    w_lo = (wr - w_hi.astype(F32)).astype(BF16)
    logits = _dot(h_hi, w_hi) + (_dot(h_hi, w_lo) + _dot(h_lo, w_hi))
    lane = lax.broadcasted_iota(jnp.int32, logits.shape, 1)
    logits = jnp.where(lane < N_EXPERTS, logits, NEG_BIG)
    e = jnp.exp(logits - jnp.max(logits, axis=-1, keepdims=True))
    aff_ref[0] = e / jnp.sum(e, axis=-1, keepdims=True)


def _proj_out(xa, a, bb, w_out, mods, gain, w_router, n_rows, n_lat):
    b, _, d = xa.shape
    nb = mods.shape[0] - 1
    tm = _row_tile(n_rows)
    wa, wb = w_out[:a.shape[-1]], w_out[a.shape[-1]:]
    tok = lambda i, j: (i, j, 0)
    const = lambda i, j: (0, 0)
    return pl.pallas_call(
        functools.partial(_proj_out_kernel, n_lat=n_lat),
        grid=(b, n_rows // tm),
        in_specs=[pl.BlockSpec((1, tm, d), tok),
                  pl.BlockSpec((1, tm, a.shape[-1]), tok),
                  pl.BlockSpec((1, tm, bb.shape[-1]), tok),
                  pl.BlockSpec(wa.shape, const),
                  pl.BlockSpec(wb.shape, const),
                  pl.BlockSpec((1, 6, d), lambda i, j: (i, 0, 0)),
                  pl.BlockSpec((1, 6, d), lambda i, j: (nb, 0, 0)),
                  pl.BlockSpec((1, d), const),
                  pl.BlockSpec(w_router.shape, const)],
        out_specs=[pl.BlockSpec((1, tm, d), tok),
                   pl.BlockSpec((1, tm, d), tok),
                   pl.BlockSpec((1, tm, LANES), tok)],
        out_shape=[jax.ShapeDtypeStruct((b, n_rows, d), F32),
                   jax.ShapeDtypeStruct((b, n_rows, d), BF16),
                   jax.ShapeDtypeStruct((b, n_rows, LANES), F32)],
        compiler_params=_cparams("parallel", "arbitrary"),
    )(xa, a, bb, wa, wb, mods, mods, gain, w_router)


def _ffn_kernel(x_ref, gate_ref, w1_ref, w3_ref, w2_ref, o_ref, w1b, w3b, w2b):
    @pl.when(pl.program_id(1) == 0)
    def _():
        w1b[...] = w1_ref[0].astype(BF16)
        w3b[...] = w3_ref[0].astype(BF16)
        w2b[...] = w2_ref[0].astype(BF16)

    x = x_ref[0]
    f = w1b.shape[1]
    y = None
    for f0 in range(0, f, FFN_HIDDEN_BLOCK):
        f1 = f0 + FFN_HIDDEN_BLOCK
        hid = _silu(_dot(x, w1b[:, f0:f1])) * _dot(x, w3b[:, f0:f1])
        part = _dot(hid.astype(BF16), w2b[f0:f1, :])
        y = part if y is None else y + part
    o_ref[0] = (y * gate_ref[0]).astype(BF16)


def _expert_ffn(xe, gate, w1, w3, w2, layer):
    e, m, d = xe.shape
    f = w1.shape[-1]
    tm = m // FFN_ROW_TILES
    assert tm * FFN_ROW_TILES == m and tm % 16 == 0
    wspec = lambda r, c: pl.BlockSpec((None, 1, r, c), lambda i, j: (layer, i, 0, 0))
    return pl.pallas_call(
        _ffn_kernel,
        grid=(e, m // tm),
        in_specs=[pl.BlockSpec((1, tm, d), lambda i, j: (i, j, 0)),
                  pl.BlockSpec((1, tm, 1), lambda i, j: (i, j, 0)),
                  wspec(d, f), wspec(d, f), wspec(f, d)],
        out_specs=pl.BlockSpec((1, tm, d), lambda i, j: (i, j, 0)),
        out_shape=jax.ShapeDtypeStruct((e, m, d), BF16),
        scratch_shapes=[pltpu.VMEM((d, f), BF16), pltpu.VMEM((d, f), BF16), pltpu.VMEM((f, d), BF16)],
        compiler_params=_cparams("parallel", "arbitrary"),
    )(xe, gate, w1, w3, w2)


def _route(aff, segments):
    b, r, _ = aff.shape
    e = N_EXPERTS
    gates, flats, toks = [], [], []
    for row0, n in segments:
        cap = (EC_CAPACITY_FACTOR * n) // e
        gate, idx = lax.top_k(jnp.swapaxes(aff[:, row0:row0 + n, :e], 1, 2), cap)
        tok = idx + row0
        gates.append(jnp.swapaxes(gate, 0, 1).reshape(e, b * cap))
        flats.append(jnp.swapaxes(tok + jnp.arange(b, dtype=idx.dtype)[:, None, None] * r, 0, 1).reshape(e, b * cap))
        toks.append(tok)
    return jnp.concatenate(gates, axis=1)[..., None], jnp.concatenate(flats, axis=1), toks


def _combine_kernel(lo_ref, x_ref, tok_ref, y_ref, m_ref, g_ref, o_ref, *, final):
    i, j = pl.program_id(0), pl.program_id(1)
    tm = x_ref.shape[1]
    n_tiles = pl.num_programs(1)
    lo = lo_ref[i * (n_tiles + 1) + j]
    hi = lo_ref[i * (n_tiles + 1) + j + 1]
    rows = j * tm + lax.broadcasted_iota(jnp.int32, (tm, 1), 0)
    o_ref[0] = jnp.zeros(o_ref.shape[1:], F32)

    def chunk(c, carry):
        sel = (tok_ref[0, pl.ds(c, 1), :] == rows).astype(BF16)
        o_ref[0] += _dot(sel, y_ref[0, pl.ds(pl.multiple_of(c * PAIR_CHUNK, PAIR_CHUNK), PAIR_CHUNK), :])
        return carry

    lax.fori_loop(lo // PAIR_CHUNK, (hi + PAIR_CHUNK - 1) // PAIR_CHUNK, chunk, 0)
    x = x_ref[0] + m_ref[0][5:6] * o_ref[0]
    if final:
        x = _rms(x) * g_ref[...]
    o_ref[0] = x


def _combine(x, tok_sorted, y_sorted, lo, mods, gain, n_lat_tiles, final):
    b, t, d = x.shape
    p = tok_sorted.shape[1]
    nb = mods.shape[0] - 1
    tm = TOK_TILE
    tok = lambda i, j, lo_ref: (i, j, 0)
    whole = lambda i, j, lo_ref: (i, 0, 0)
    grid_spec = pltpu.PrefetchScalarGridSpec(
        num_scalar_prefetch=1,
        grid=(b, t // tm),
        in_specs=[pl.BlockSpec((1, tm, d), tok),
                  pl.BlockSpec((1, p // PAIR_CHUNK, PAIR_CHUNK), whole),
                  pl.BlockSpec((1, p, d), whole, pipeline_mode=pl.Buffered(1)),
                  pl.BlockSpec((1, 6, d), lambda i, j, lo_ref: (jnp.where(j >= n_lat_tiles, nb, i), 0, 0)),
                  pl.BlockSpec((1, d), lambda i, j, lo_ref: (0, 0))],
        out_specs=pl.BlockSpec((1, tm, d), tok))
    return pl.pallas_call(
        functools.partial(_combine_kernel, final=final),
        grid_spec=grid_spec,
        out_shape=jax.ShapeDtypeStruct((b, t, d), F32),
        compiler_params=_cparams("parallel", "arbitrary"),
    )(lo, x, tok_sorted.reshape(b, p // PAIR_CHUNK, PAIR_CHUNK), y_sorted, mods, gain)


def _moe(x_mid, h2, aff, segments, mods, gain, n_lat_tiles, final, w1, w3, w2, layer):
    b, r, d = h2.shape
    gate, flat, toks = _route(aff, segments)
    e, m = flat.shape
    xe = h2.reshape(b * r, d).at[flat].get(mode="promise_in_bounds")
    ye = _expert_ffn(xe, gate, w1, w3, w2, layer)
    srcs, off = [], 0
    for tok in toks:
        cap = tok.shape[-1]
        src = (jnp.arange(e, dtype=jnp.int32)[None, :, None] * m + off
               + jnp.arange(b, dtype=jnp.int32)[:, None, None] * cap + jnp.arange(cap, dtype=jnp.int32))
        srcs.append(src.reshape(b, e * cap))
        off += b * cap
    tok_all = jnp.concatenate([tok.reshape(b, -1) for tok in toks], axis=1)
    tok_sorted, src_sorted = lax.sort_key_val(tok_all, jnp.concatenate(srcs, axis=1), dimension=1)
    y_sorted = ye.reshape(e * m, d).at[src_sorted].get(mode="promise_in_bounds")
    starts = jnp.arange(r // TOK_TILE + 1, dtype=jnp.int32) * TOK_TILE
    lo = jnp.sum((tok_sorted[:, None, :] < starts[None, :, None]).astype(jnp.int32), axis=-1)
    return _combine(x_mid, tok_sorted, y_sorted, lo.reshape(-1), mods, gain, n_lat_tiles, final)


def _proj_odd_kernel(x_ref, ml_ref, mc_ref, g_ref, w_ref, qn_ref, kvn_ref, wq_ref, wkv_ref, cos_ref, sin_ref,
                     q_ref, k_ref, v_ref, rq_ref, rk_ref, rv_ref, rg_ref, *, q_scale, n_lat):
    h = _mixer_input(x_ref[0], g_ref[...], ml_ref[0], mc_ref[0], n_lat)
    acc = _dot(h.astype(BF16), w_ref[...])
    cos, sin = cos_ref[...], sin_ref[...]
    c0 = MLA_Q_RANK
    c1 = c0 + MLA_KV_RANK
    cq = acc[:, :c0]
    ckv = acc[:, c0:c1]
    kr = acc[:, c1:c1 + LANES] * cos + acc[:, c1 + LANES:c1 + 2 * LANES] * sin
    c2 = c1 + 2 * LANES
    nq = RET_HEADS * RET_QK_DIM
    nv = RET_HEADS * RET_V_DIM
    rq_ref[0] = acc[:, c2:c2 + nq].astype(BF16)
    rk_ref[0] = acc[:, c2 + nq:c2 + 2 * nq].astype(BF16)
    rv_ref[0] = acc[:, c2 + 2 * nq:c2 + 2 * nq + nv].astype(BF16)
    rg_ref[0] = acc[:, c2 + 2 * nq + nv:]

    nqk = MLA_HEADS * LANES
    qq = _dot((_rms(cq) * qn_ref[...]).astype(BF16), wq_ref[...])
    q = qq[:, :nqk] * _tile_lanes(cos, MLA_HEADS) + qq[:, nqk:] * _tile_lanes(sin, MLA_HEADS)
    q_ref[0] = (q * q_scale).astype(BF16)
    kv = _dot((_rms(ckv) * kvn_ref[...]).astype(BF16), wkv_ref[...])
    k_ref[0] = (kv[:, :nqk] + _tile_lanes(kr, MLA_HEADS)).astype(BF16)
    v_ref[0] = kv[:, nqk:].astype(BF16)


def _proj_odd(xa, mods, gain, w, q_norm, kv_norm, wq, wkv, cos, sin, n_lat):
    b, t, d = xa.shape
    nb = mods.shape[0] - 1
    tm = _row_tile(t)
    tok = lambda i, j: (i, j, 0)
    const = lambda i, j: (0, 0)
    widths = (MLA_HEADS * LANES, MLA_HEADS * LANES, MLA_HEADS * MLA_V_DIM,
              RET_HEADS * RET_QK_DIM, RET_HEADS * RET_QK_DIM, RET_HEADS * RET_V_DIM, RET_HEADS * RET_V_DIM)
    dtypes = (BF16,) * 6 + (F32,)
    kern = functools.partial(_proj_odd_kernel, q_scale=float((MLA_NOPE_DIM + MLA_ROPE_DIM) ** -0.5) * LOG2E,
                             n_lat=n_lat)
    return pl.pallas_call(
        kern,
        grid=(b, t // tm),
        in_specs=[pl.BlockSpec((1, tm, d), tok),
                  pl.BlockSpec((1, 6, d), lambda i, j: (i, 0, 0)),
                  pl.BlockSpec((1, 6, d), lambda i, j: (nb, 0, 0)),
                  pl.BlockSpec((1, d), const),
                  pl.BlockSpec(w.shape, const),
                  pl.BlockSpec(q_norm.shape, const),
                  pl.BlockSpec(kv_norm.shape, const),
                  pl.BlockSpec(wq.shape, const),
                  pl.BlockSpec(wkv.shape, const),
                  pl.BlockSpec((tm, LANES), lambda i, j: (j, 0)),
                  pl.BlockSpec((tm, LANES), lambda i, j: (j, 0))],
        out_specs=[pl.BlockSpec((1, tm, wd), tok) for wd in widths],
        out_shape=[jax.ShapeDtypeStruct((b, t, wd), dt) for wd, dt in zip(widths, dtypes)],
        compiler_params=_cparams("parallel", "arbitrary"),
    )(xa, mods, mods, gain, w, q_norm, kv_norm, wq, wkv, cos, sin)


def _mla_kernel(q_ref, k_ref, v_ref, o_ref, vx_ref):
    @pl.when(pl.program_id(2) == 0)
    def _():
        _fill_v_ones(vx_ref, v_ref)

    o0, o1 = _two_stream_attention(q_ref[0, :, :LANES], [(k_ref[0, :, :LANES], None)],
                                   q_ref[0, :, LANES:], [(k_ref[0, :, LANES:], None)], [vx_ref[...]])
    lane = lax.broadcasted_iota(jnp.int32, (1, LANES), 1)
    o_ref[0] = jnp.where(lane < MLA_V_DIM, o0, o1).astype(BF16)


def _mla_attn(q, k, v, n_lat, tq):
    b, t, _ = k.shape
    npairs = MLA_HEADS // 2
    return pl.pallas_call(
        _mla_kernel,
        grid=(b, npairs, n_lat // tq),
        in_specs=[pl.BlockSpec((1, tq, 2 * LANES), lambda i, h, j: (i, j, h)),
                  pl.BlockSpec((1, t, 2 * LANES), lambda i, h, j: (i, 0, h)),
                  pl.BlockSpec((1, t, LANES), lambda i, h, j: (i, 0, h))],
        out_specs=pl.BlockSpec((1, tq, LANES), lambda i, h, j: (i, j, h)),
        out_shape=jax.ShapeDtypeStruct((b, n_lat, npairs * LANES), BF16),
        scratch_shapes=[pltpu.VMEM((t, 2 * LANES), BF16)],
        compiler_params=_cparams("parallel", "parallel", "arbitrary"),
    )(q, k, v)


def _retention_kernel(dl_ref, q_ref, k_ref, v_ref, g_ref, o_ref, f_ref, b_ref, st_ref, *, n_lat):
    c = RET_CHUNK
    t = k_ref.shape[1]
    n_chunks = n_lat // c
    n_ctx = (t - n_lat) // c
    hp = pl.program_id(1)
    lane = lax.broadcasted_iota(jnp.int32, (1, LANES), 1)
    pos_r = lax.broadcasted_iota(jnp.int32, (c, 1), 0).astype(F32)
    ii = lax.broadcasted_iota(jnp.int32, (c, c), 0)
    jj = lax.broadcasted_iota(jnp.int32, (c, c), 1)
    rel = (ii - jj).astype(F32)
    dl = dl_ref[...]
    hsel = lax.broadcasted_iota(jnp.int32, (1, RET_HEADS), 1)

    chains = []
    for hh in range(2):
        hmask = (lane >= hh * RET_QK_DIM) & (lane < (hh + 1) * RET_QK_DIM)
        for direction in range(2):
            logit = jnp.sum(jnp.where(hsel == 2 * hp + hh, dl[direction:direction + 1], 0.0),
                            axis=-1, keepdims=True)
            lg = jnp.minimum(logit, 0.0) - jnp.log(1.0 + jnp.exp(-jnp.abs(logit)))
            if direction == 0:
                mask = rel >= 0
                d_in = jnp.where(mask, jnp.exp(lg * jnp.where(mask, rel, 0.0)), 0.0)
                d_q = jnp.exp(lg * (pos_r + 1.0))
                d_k = jnp.exp(lg * (c - 1.0 - pos_r))
            else:
                mask = rel < 0
                d_in = jnp.where(mask, jnp.exp(lg * jnp.where(mask, -rel, 0.0)), 0.0)
                d_q = jnp.exp(lg * (c - pos_r))
                d_k = jnp.exp(lg * pos_r)
            chains.append(dict(idx=2 * hh + direction, v0=hh * RET_V_DIM, direction=direction, hmask=hmask,
                               d_in=d_in, d_q=d_q, d_k=d_k, d_chunk=jnp.exp(lg * c)))

    st_ref[...] = jnp.zeros(st_ref.shape, F32)

    def kv_update(ch, start):
        kb = k_ref[0, pl.ds(start, c), :].astype(F32)
        vb = v_ref[0, pl.ds(start, c), ch["v0"]:ch["v0"] + RET_V_DIM]
        kd = jnp.where(ch["hmask"], kb * ch["d_k"], 0.0).astype(BF16)
        st_ref[ch["idx"]] = ch["d_chunk"] * st_ref[ch["idx"]] + _dot_tn(kd, vb)

    for i in range(n_ctx):
        for ch in chains:
            ci = i if ch["direction"] == 0 else n_ctx - 1 - i
            kv_update(ch, n_lat + ci * c)

    def lat_step(i, carry):
        starts = [pl.multiple_of((i if ch["direction"] == 0 else n_chunks - 1 - i) * c, c) for ch in chains]
        qms, avs, crs = [], [], []
        for ch, start in zip(chains, starts):
            qb = q_ref[0, pl.ds(start, c), :]
            qm = jnp.where(ch["hmask"], qb, jnp.zeros_like(qb))
            qms.append(qm)
            avs.append(_dot_nt(qm, k_ref[0, pl.ds(start, c), :]))
        for ch, qm in zip(chains, qms):
            qd = (qm.astype(F32) * ch["d_q"]).astype(BF16)
            crs.append(_dot(qd, st_ref[ch["idx"]].astype(BF16)))
        for ch, start, a, cross in zip(chains, starts, avs, crs):
            vb = v_ref[0, pl.ds(start, c), ch["v0"]:ch["v0"] + RET_V_DIM]
            res = _dot((a * ch["d_in"]).astype(BF16), vb) + cross
            dst = f_ref if ch["direction"] == 0 else b_ref
            dst[pl.ds(start, c), ch["v0"]:ch["v0"] + RET_V_DIM] = res
        for ch, start in zip(chains, starts):
            kv_update(ch, start)
        return carry

    lax.fori_loop(0, n_chunks, lat_step, 0)

    def finish(i, carry):
        start = pl.multiple_of(i * c, c)
        for hh in range(2):
            v0 = hh * RET_V_DIM
            r = _rms(f_ref[pl.ds(start, c), v0:v0 + RET_V_DIM] + b_ref[pl.ds(start, c), v0:v0 + RET_V_DIM])
            gate = _silu(g_ref[0, pl.ds(start, c), v0:v0 + RET_V_DIM])
            o_ref[0, pl.ds(start, c), v0:v0 + RET_V_DIM] = (r * gate).astype(BF16)
        return carry

    lax.fori_loop(0, n_chunks, finish, 0)


def _retention(rq, rk, rv, rg, decay_logit, n_lat):
    b, t, _ = rq.shape
    npairs = RET_HEADS // 2
    kern = functools.partial(_retention_kernel, n_lat=n_lat)
    return pl.pallas_call(
        kern,
        grid=(b, npairs),
        in_specs=[pl.BlockSpec(decay_logit.shape, lambda i, h: (0, 0)),
                  pl.BlockSpec((1, n_lat, LANES), lambda i, h: (i, 0, h)),
                  pl.BlockSpec((1, t, LANES), lambda i, h: (i, 0, h)),
                  pl.BlockSpec((1, t, 2 * RET_V_DIM), lambda i, h: (i, 0, h)),
                  pl.BlockSpec((1, n_lat, 2 * RET_V_DIM), lambda i, h: (i, 0, h))],
        out_specs=pl.BlockSpec((1, n_lat, 2 * RET_V_DIM), lambda i, h: (i, 0, h)),
        out_shape=jax.ShapeDtypeStruct((b, n_lat, RET_HEADS * RET_V_DIM), BF16),
        scratch_shapes=[pltpu.VMEM((n_lat, 2 * RET_V_DIM), F32), pltpu.VMEM((n_lat, 2 * RET_V_DIM), F32),
                        pltpu.VMEM((4, LANES, RET_V_DIM), F32)],
        compiler_params=_cparams("parallel", "arbitrary"),
    )(decay_logit, rq, rk, rv, rg)


def _rope_tables(n_lat, n_ctx, rot_dim, lane0):
    t = np.arange(n_lat)
    rows = (t // GRID_W).astype(np.float32)
    cols = (t % GRID_W).astype(np.float32)
    m = rot_dim // 4
    freqs = jnp.asarray(ROPE_BASE, F32) ** (-jnp.arange(m, dtype=F32) / m)
    ang_r = jnp.asarray(rows)[:, None] * freqs
    ang_c = jnp.asarray(cols)[:, None] * freqs
    cos = jnp.concatenate([jnp.cos(ang_r)] * 2 + [jnp.cos(ang_c)] * 2, axis=-1)
    sin = jnp.concatenate([-jnp.sin(ang_r), jnp.sin(ang_r), -jnp.sin(ang_c), jnp.sin(ang_c)], axis=-1)
    reps = (LANES - lane0) // rot_dim if lane0 == 0 else 1
    cos = jnp.concatenate([jnp.ones((n_lat, lane0), F32)] + [cos] * reps
                          + [jnp.ones((n_lat, LANES - lane0 - reps * rot_dim), F32)], axis=-1)
    sin = jnp.concatenate([jnp.zeros((n_lat, lane0), F32)] + [sin] * reps
                          + [jnp.zeros((n_lat, LANES - lane0 - reps * rot_dim), F32)], axis=-1)
    cos = jnp.concatenate([cos, jnp.ones((n_ctx, LANES), F32)], axis=0)
    sin = jnp.concatenate([sin, jnp.zeros((n_ctx, LANES), F32)], axis=0)
    return cos, sin


def _rot_partner(w, rot_dim):
    k, n = w.shape
    q = rot_dim // 4
    return w.reshape(k, n // rot_dim, 2, 2, q)[:, :, :, ::-1, :].reshape(k, n)


def _pad_cols(w, lane0, width=LANES):
    k, n = w.shape
    return jnp.concatenate([jnp.zeros((k, lane0), w.dtype), w, jnp.zeros((k, width - lane0 - n), w.dtype)], axis=-1)


def _even_weights(w_in):
    n_rope = 2 * DIFF_HEADS * 2 * DIFF_HEAD_DIM
    dw = DIFF_HEADS * 2 * DIFF_HEAD_DIM
    nw = NA_HEADS * NA_HEAD_DIM
    scale = jnp.concatenate([jnp.full((dw,), DIFF_HEAD_DIM ** -0.5, F32), jnp.ones((2 * dw,), F32),
                             jnp.full((nw,), NA_HEAD_DIM ** -0.5, F32), jnp.ones((2 * nw,), F32)])
    w = w_in * scale
    return jnp.concatenate([w, _rot_partner(w[:, :n_rope], DIFF_HEAD_DIM)], axis=-1).astype(BF16), n_rope


def _odd_weights(w_in, w_uq, w_ukv):
    c0 = MLA_Q_RANK
    c1 = c0 + MLA_KV_RANK
    c2 = c1 + MLA_ROPE_DIM
    nq = RET_HEADS * RET_QK_DIM
    kr = w_in[:, c1:c2]
    w = jnp.concatenate([w_in[:, :c1],
                         _pad_cols(kr, MLA_NOPE_DIM), _pad_cols(_rot_partner(kr, MLA_ROPE_DIM), MLA_NOPE_DIM),
                         w_in[:, c2:c2 + nq], w_in[:, c2 + nq:c2 + 2 * nq] * (RET_QK_DIM ** -0.5),
                         w_in[:, c2 + 2 * nq:]], axis=-1).astype(BF16)
    r = w_uq.shape[0]
    uq = w_uq.reshape(r, MLA_HEADS, MLA_NOPE_DIM + MLA_ROPE_DIM)
    pad = jnp.zeros((r, MLA_HEADS, LANES - MLA_NOPE_DIM - MLA_ROPE_DIM), F32)
    uq_rot = _rot_partner(uq[:, :, MLA_NOPE_DIM:].reshape(r, -1), MLA_ROPE_DIM).reshape(r, MLA_HEADS, MLA_ROPE_DIM)
    wq = jnp.concatenate([uq, pad], axis=-1).reshape(r, -1)
    wq_rot = jnp.concatenate([jnp.zeros_like(uq[:, :, :MLA_NOPE_DIM]), uq_rot, pad], axis=-1).reshape(r, -1)
    rk = w_ukv.shape[0]
    ukv = w_ukv.reshape(rk, MLA_HEADS, MLA_NOPE_DIM + MLA_V_DIM)
    wk = jnp.concatenate([ukv[:, :, :MLA_NOPE_DIM], jnp.zeros((rk, MLA_HEADS, LANES - MLA_NOPE_DIM), F32)],
                         axis=-1).reshape(rk, -1)
    wv = ukv[:, :, MLA_NOPE_DIM:].reshape(rk, -1)
    return w, jnp.concatenate([wq, wq_rot], axis=-1).astype(BF16), jnp.concatenate([wk, wv], axis=-1).astype(BF16)


def _na_bias(rpb, n_lat):
    pat_of_group, idx_r, idx_c, valid = _na_patterns(n_lat // GRID_W)
    g = NA_GROUP_ROWS
    span = g + NA_WIN_ROWS - 1
    n_pat = idx_r.shape[0]
    col_sel = (idx_c[0, :GRID_W, :GRID_W, None] == np.arange(2 * NA_WIN_COLS - 1)).astype(np.float32)
    row_idx = idx_r.reshape(n_pat, g, GRID_W, span, GRID_W)[:, :, 0, :, 0]
    row_sel = (row_idx[..., None] == np.arange(2 * NA_WIN_ROWS - 1)).astype(np.float32)
    cols = jnp.einsum('hrc,qkc->hrqk', rpb, col_sel, precision=lax.Precision.HIGHEST)
    bias = jnp.einsum('pijr,hrqk->hpiqjk', row_sel, cols, precision=lax.Precision.HIGHEST)
    bias = bias.reshape(rpb.shape[0], n_pat, g * GRID_W, span * GRID_W)
    return jnp.where(valid[None], bias * LOG2E, NEG_BIG), pat_of_group


def _router_weights(w_router):
    return _pad_cols(w_router, 0)


def kernel(x, c, ctx, c_ctx, ada_w, ada_b, norm_mix, norm_ffn, final_norm, even_w_in, even_w_out,
           diff_lambda, diff_subln, na_rpb, odd_w_in, odd_w_out, mla_q_norm, mla_w_uq, mla_kv_norm,
           mla_w_ukv, ret_decay_logit, moe_router, moe_w1, moe_w3, moe_w2):
    b, n_lat, d = x.shape
    n_ctx = ctx.shape[1]
    depth = ada_w.shape[0]
    n_lat_tiles = n_lat // TOK_TILE

    cond = jnp.concatenate([c, c_ctx[None], jnp.zeros((7, d), F32)], axis=0)
    mods_all = _mods(cond, ada_w, ada_b)[:, :b + 1].reshape(depth, b + 1, 6, d)
    xa = jnp.concatenate([x, ctx], axis=1)

    for l in range(depth):
        mods = mods_all[l]
        need_ctx = l < depth - 1
        i = l // 2
        gain_mix = norm_mix[l][None]
        if l % 2 == 0:
            w, n_rope = _even_weights(even_w_in[i])
            cos, sin = _rope_tables(n_lat, n_ctx, DIFF_HEAD_DIM, 0)
            qkv = _proj_even(xa, mods, gain_mix, w, cos, sin, n_lat, n_rope, even_w_in.shape[-1])
            lam_init = 0.8 - 0.6 * math.exp(-0.3 * l)
            mix_a = _diff_attn(qkv, diff_lambda[i], diff_subln[i][None], lam_init, n_lat)
            bias, offsets = _na_bias(na_rpb[i], n_lat)
            mix_b = _na_attn(qkv, bias, n_lat, offsets, 3 * DIFF_HEADS * 2 * DIFF_HEAD_DIM)
            w_out = even_w_out[i].astype(BF16)
        else:
            w, wq, wkv = _odd_weights(odd_w_in[i], mla_w_uq[i], mla_w_ukv[i])
            cos, sin = _rope_tables(n_lat, n_ctx, MLA_ROPE_DIM, MLA_NOPE_DIM)
            q, k, v, rq, rk, rv, rg = _proj_odd(xa, mods, gain_mix, w, mla_q_norm[i][None], mla_kv_norm[i][None],
                                                wq, wkv, cos, sin, n_lat)
            mix_a = _mla_attn(q, k, v, n_lat, ATTN_Q_TILE)
            mix_b = _retention(rq, rk, rv, rg, ret_decay_logit[i], n_lat)
            w_out = odd_w_out[i].astype(BF16)

        n_rows = n_lat + n_ctx if need_ctx else n_lat
        x_mid, h2, aff = _proj_out(xa, mix_a, mix_b, w_out, mods, norm_ffn[l][None],
                                   _router_weights(moe_router[l]), n_rows, n_lat)
        segments = ((0, n_lat), (n_lat, n_ctx)) if need_ctx else ((0, n_lat),)
        xa = _moe(x_mid, h2, aff, segments, mods, final_norm[None], n_lat_tiles, l == depth - 1,
                  moe_w1, moe_w3, moe_w2, l)
    return xa
```

```python
import functools
import math

import jax
import jax.numpy as jnp
import numpy as np
from jax import lax
from jax.experimental import pallas as pl
from jax.experimental.pallas import tpu as pltpu

GRID_W = 64
ROPE_BASE = 10000.0
RMS_EPS = 1e-6
DIFF_HEADS = 4
DIFF_HEAD_DIM = 64
NA_HEADS = 8
NA_HEAD_DIM = 64
NA_WIN_ROWS = 8
NA_WIN_COLS = 16
MLA_HEADS = 8
MLA_Q_RANK = 256
MLA_KV_RANK = 128
MLA_NOPE_DIM = 64
MLA_ROPE_DIM = 32
MLA_V_DIM = 64
RET_HEADS = 4
RET_QK_DIM = 64
RET_V_DIM = 128
RET_CHUNK = 128
N_EXPERTS = 16
EC_CAPACITY_FACTOR = 2

LANES = 128
VMEM_LIMIT = 56 * 1024 * 1024
NEG_BIG = -1e30
TOK_TILE = 256
ROW_TILES = 8
ATTN_Q_TILE = 512
FFN_ROW_TILES = 8
FFN_HIDDEN_BLOCK = 512
NA_GROUP_ROWS = 4
NA_GROUPS_PER_STEP = 2
PAIR_CHUNK = 256
LOG2E = math.log2(math.e)

BF16 = jnp.bfloat16
F32 = jnp.float32


def _cparams(*sem):
    return pltpu.CompilerParams(dimension_semantics=sem, vmem_limit_bytes=VMEM_LIMIT)


def _dot(a, b):
    return jnp.dot(a, b, preferred_element_type=F32)


def _dot_nt(a, b):
    return lax.dot_general(a, b, (((1,), (1,)), ((), ())), preferred_element_type=F32)


def _dot_tn(a, b):
    return lax.dot_general(a, b, (((0,), (0,)), ((), ())), preferred_element_type=F32)


def _rms(x):
    return x * lax.rsqrt(jnp.mean(x * x, axis=-1, keepdims=True) + RMS_EPS)


def _silu(x):
    return x * (1.0 / (1.0 + jnp.exp(-x)))


def _tile_lanes(t, n):
    return jnp.concatenate([t] * n, axis=-1)


def _mods_kernel(c_ref, w_ref, b_ref, o_ref):
    o_ref[0] = _dot(_silu(c_ref[...]), w_ref[0]) + b_ref[0]


def _mods(cond, ada_w, ada_b):
    depth, d, n = ada_w.shape
    r = cond.shape[0]
    tn = 1536
    return pl.pallas_call(
        _mods_kernel,
        grid=(depth, n // tn),
        in_specs=[pl.BlockSpec((r, d), lambda l, j: (0, 0)),
                  pl.BlockSpec((1, d, tn), lambda l, j: (l, 0, j)),
                  pl.BlockSpec((1, 1, tn), lambda l, j: (l, 0, j))],
        out_specs=pl.BlockSpec((1, r, tn), lambda l, j: (l, 0, j)),
        out_shape=jax.ShapeDtypeStruct((depth, r, n), F32),
        compiler_params=_cparams("arbitrary", "arbitrary"),
    )(cond, ada_w, ada_b.reshape(depth, 1, n))


def _normed(x, gain, mods, shift_row, scale_row):
    return _rms(x) * gain * (1.0 + mods[scale_row:scale_row + 1]) + mods[shift_row:shift_row + 1]


def _mixer_input(x, gain, mods_lat, mods_ctx, n_lat):
    tm = x.shape[0]
    rows = pl.program_id(1) * tm + lax.broadcasted_iota(jnp.int32, (tm, 1), 0)
    is_ctx = rows >= n_lat
    scale = jnp.where(is_ctx, mods_ctx[1:2], mods_lat[1:2])
    shift = jnp.where(is_ctx, mods_ctx[0:1], mods_lat[0:1])
    return _rms(x) * gain * (1.0 + scale) + shift


def _proj_even_kernel(x_ref, ml_ref, mc_ref, g_ref, w_ref, cos_ref, sin_ref, o_ref, *, n_rope, n_out, n_lat):
    na_q0 = n_rope + n_rope // 2
    na_q1 = na_q0 + NA_HEADS * NA_HEAD_DIM
    h = _mixer_input(x_ref[0], g_ref[...], ml_ref[0], mc_ref[0], n_lat)
    acc = _dot(h.astype(BF16), w_ref[...])
    reps = n_rope // LANES
    cos = _tile_lanes(cos_ref[...], reps)
    sin = _tile_lanes(sin_ref[...], reps)
    roped = acc[:, :n_rope] * cos + acc[:, n_out:] * sin
    n_q = n_rope // 2
    o_ref[0, :, :n_q] = (roped[:, :n_q] * LOG2E).astype(BF16)
    o_ref[0, :, n_q:n_rope] = roped[:, n_q:].astype(BF16)
    o_ref[0, :, n_rope:na_q0] = acc[:, n_rope:na_q0].astype(BF16)
    o_ref[0, :, na_q0:na_q1] = (acc[:, na_q0:na_q1] * LOG2E).astype(BF16)
    o_ref[0, :, na_q1:] = acc[:, na_q1:n_out].astype(BF16)


def _row_tile(t):
    tm = t // ROW_TILES
    assert tm * ROW_TILES == t and tm % 16 == 0
    return tm


def _proj_even(xa, mods, gain, w, cos, sin, n_lat, n_rope, n_out):
    b, t, d = xa.shape
    nb = mods.shape[0] - 1
    tm = _row_tile(t)
    kern = functools.partial(_proj_even_kernel, n_rope=n_rope, n_out=n_out, n_lat=n_lat)
    return pl.pallas_call(
        kern,
        grid=(b, t // tm),
        in_specs=[pl.BlockSpec((1, tm, d), lambda i, j: (i, j, 0)),
                  pl.BlockSpec((1, 6, d), lambda i, j: (i, 0, 0)),
                  pl.BlockSpec((1, 6, d), lambda i, j: (nb, 0, 0)),
                  pl.BlockSpec((1, d), lambda i, j: (0, 0)),
                  pl.BlockSpec(w.shape, lambda i, j: (0, 0)),
                  pl.BlockSpec((tm, LANES), lambda i, j: (j, 0)),
                  pl.BlockSpec((tm, LANES), lambda i, j: (j, 0))],
        out_specs=pl.BlockSpec((1, tm, n_out), lambda i, j: (i, j, 0)),
        out_shape=jax.ShapeDtypeStruct((b, t, n_out), BF16),
        compiler_params=_cparams("parallel", "arbitrary"),
    )(xa, mods, mods, gain, w, cos, sin)


def _fill_v_ones(vx_ref, v_ref):
    vx_ref[:, :LANES] = v_ref[0]
    vx_ref[:, LANES:] = jnp.ones((vx_ref.shape[0], LANES), BF16)


def _weights(s, m):
    return jnp.exp2(s - m).astype(BF16)


def _rowmax(*ss):
    m = jnp.max(ss[0], axis=-1, keepdims=True)
    for s in ss[1:]:
        m = jnp.maximum(m, jnp.max(s, axis=-1, keepdims=True))
    return m


def _normalised(ox):
    return ox[:, :LANES] / ox[:, LANES:LANES + 1]


def _scores(q, segs):
    return [_dot_nt(q, k) if bias is None else _dot_nt(q, k) + bias for k, bias in segs]


def _pv(ps, vxs):
    o = _dot(ps[0], vxs[0])
    for p, vx in zip(ps[1:], vxs[1:]):
        o = o + _dot(p, vx)
    return o


def _multi_stream_attention(streams):
    n = len(streams)
    s, m, p, o = [None] * n, [None] * n, [None] * n, [None] * n
    for t in range(n + 2):
        if t < n:
            s[t] = _scores(streams[t][0], streams[t][1])
        if 0 <= t - 2 < n:
            o[t - 2] = _normalised(_pv(p[t - 2], streams[t - 2][2]))
        if 0 <= t - 1 < n:
            p[t - 1] = [_weights(x, m[t - 1]) for x in s[t - 1]]
        if t < n:
            m[t] = _rowmax(*s[t])
    return o


def _two_stream_attention(qa, segs_a, qb, segs_b, vxs):
    return _multi_stream_attention([(qa, segs_a, vxs), (qb, segs_b, vxs)])


def _diff_attn_kernel(lam_ref, g_ref, q_ref, k_ref, v_ref, o_ref, vx_ref, *, lam_init):
    @pl.when(pl.program_id(2) == 0)
    def _():
        _fill_v_ones(vx_ref, v_ref)

    lp = lam_ref[...]
    s1 = jnp.sum(lp[0:1] * lp[1:2], axis=-1, keepdims=True)
    s2 = jnp.sum(lp[2:3] * lp[3:4], axis=-1, keepdims=True)
    lam = jnp.exp(s1) - jnp.exp(s2) + lam_init
    q = q_ref[0]
    lane = lax.broadcasted_iota(jnp.int32, (1, LANES), 1)
    q1 = jnp.where(lane < DIFF_HEAD_DIM, q, jnp.zeros_like(q))
    q2 = jnp.where(lane >= DIFF_HEAD_DIM, q, jnp.zeros_like(q))

    k = k_ref[0]
    o1, o2 = _two_stream_attention(q1, [(k, None)], q2, [(k, None)], [vx_ref[...]])
    o = o1 - lam * o2
    o_ref[0] = (_rms(o) * g_ref[...] * (1.0 - lam_init)).astype(BF16)


def _diff_attn(qkv, lam_params, subln, lam_init, q_row0, n_q, tq, k_row0, n_k):
    b = qkv.shape[0]
    nh = DIFF_HEADS
    q0, k0 = q_row0 // tq, k_row0 // n_k
    assert q0 * tq == q_row0 and k0 * n_k == k_row0 and n_q % tq == 0
    return pl.pallas_call(
        functools.partial(_diff_attn_kernel, lam_init=lam_init),
        grid=(b, nh, n_q // tq),
        in_specs=[pl.BlockSpec(lam_params.shape, lambda i, h, j: (0, 0)),
                  pl.BlockSpec((1, LANES), lambda i, h, j: (0, 0)),
                  pl.BlockSpec((1, tq, LANES), lambda i, h, j: (i, q0 + j, h)),
                  pl.BlockSpec((1, n_k, LANES), lambda i, h, j: (i, k0, nh + h)),
                  pl.BlockSpec((1, n_k, LANES), lambda i, h, j: (i, k0, 2 * nh + h))],
        out_specs=pl.BlockSpec((1, tq, LANES), lambda i, h, j: (i, j, h)),
        out_shape=jax.ShapeDtypeStruct((b, n_q, nh * LANES), BF16),
        scratch_shapes=[pltpu.VMEM((n_k, 2 * LANES), BF16)],
        compiler_params=_cparams("parallel", "parallel", "arbitrary"),
    )(lam_params, subln, qkv, qkv, qkv)


def _na_patterns(rows):
    g = NA_GROUP_ROWS
    span = g + NA_WIN_ROWS - 1
    nq, nk = g * GRID_W, span * GRID_W
    qi = np.arange(nq)[:, None]
    ki = np.arange(nk)[None, :]
    qcol, kcol = qi % GRID_W, ki % GRID_W
    cs = np.clip(qcol - NA_WIN_COLS // 2, 0, GRID_W - NA_WIN_COLS)
    col_ok = (kcol >= cs) & (kcol < cs + NA_WIN_COLS)
    idx_c = np.clip(kcol - qcol + NA_WIN_COLS - 1, 0, 2 * NA_WIN_COLS - 2) + 0 * qi
    seen, pat_of_group, idx_r, valid = {}, [], [], []
    for r0 in range(0, rows, g):
        ks = min(max(r0 - NA_WIN_ROWS // 2, 0), rows - span)
        qrow = r0 + qi // GRID_W
        krow = ks + ki // GRID_W
        rs = np.clip(qrow - NA_WIN_ROWS // 2, 0, rows - NA_WIN_ROWS)
        ok = (krow >= rs) & (krow < rs + NA_WIN_ROWS) & col_ok
        ir = np.clip(krow - qrow + NA_WIN_ROWS - 1, 0, 2 * NA_WIN_ROWS - 2) + 0 * kcol
        key = (ok.tobytes(), np.where(ok, ir, 0).tobytes())
        if key not in seen:
            seen[key] = len(idx_r)
            idx_r.append(ir)
            valid.append(ok)
        pat_of_group.append(seen[key])
    n_pat = len(idx_r)
    return tuple(pat_of_group), np.stack(idx_r), np.stack([idx_c] * n_pat), np.stack(valid)


def _na_kernel(q_ref, k_ref, v_ref, bias_ref, o_ref, vx_ref, *, n_lat, pat_of_group):
    g_rows = NA_GROUP_ROWS
    span = g_rows + NA_WIN_ROWS - 1
    rows = n_lat // GRID_W
    nq, nk = g_rows * GRID_W, span * GRID_W
    lane = lax.broadcasted_iota(jnp.int32, (1, LANES), 1)
    head_masks = (lane < NA_HEAD_DIM, lane >= NA_HEAD_DIM)
    _fill_v_ones(vx_ref, v_ref)
    kc = k_ref[0, n_lat:, :]
    vxc = vx_ref[n_lat:, :]

    def both_heads(blocks):
        streams = []
        for q, segs, vxs in blocks:
            for hh, hm in enumerate(head_masks):
                streams.append((jnp.where(hm, q, jnp.zeros_like(q)),
                                [(k, None if bi is None else bias_ref[hh, bi]) for k, bi in segs], vxs))
        outs = _multi_stream_attention(streams)
        return [jnp.where(head_masks[0], outs[2 * i], outs[2 * i + 1]).astype(BF16) for i in range(len(blocks))]

    def group_block(g):
        r0 = g * g_rows
        ks = jnp.clip(r0 - NA_WIN_ROWS // 2, 0, rows - span)
        common = max(set(pat_of_group), key=pat_of_group.count)
        pat = jnp.int32(common)
        for gi, p in enumerate(pat_of_group):
            if p != common:
                pat = jnp.where(g == gi, p, pat)
        qg = q_ref[0, pl.ds(pl.multiple_of(g * nq, nq), nq), :]
        kw = k_ref[0, pl.ds(pl.multiple_of(ks * GRID_W, GRID_W), nk), :]
        vxw = vx_ref[pl.ds(pl.multiple_of(ks * GRID_W, GRID_W), nk), :]
        return qg, [(kw, pat), (kc, None)], [vxw, vxc]

    def group_step(i, carry):
        gs = [i * NA_GROUPS_PER_STEP + u for u in range(NA_GROUPS_PER_STEP)]
        for g, out in zip(gs, both_heads([group_block(g) for g in gs])):
            o_ref[0, pl.ds(pl.multiple_of(g * nq, nq), nq), :] = out
        return carry

    lax.fori_loop(0, rows // g_rows // NA_GROUPS_PER_STEP, group_step, 0)

    o_ref[0, n_lat:, :] = both_heads([(q_ref[0, n_lat:, :], [(kc, None)], [vxc])])[0]


def _na_attn(qkv, bias, n_lat, pat_of_group, col0):
    b, t, _ = qkv.shape
    npairs = NA_HEADS // 2
    cb = col0 // LANES
    kern = functools.partial(_na_kernel, n_lat=n_lat, pat_of_group=pat_of_group)
    return pl.pallas_call(
        kern,
        grid=(b, npairs),
        in_specs=[pl.BlockSpec((1, t, LANES), lambda i, h: (i, 0, cb + h)),
                  pl.BlockSpec((1, t, LANES), lambda i, h: (i, 0, cb + npairs + h)),
                  pl.BlockSpec((1, t, LANES), lambda i, h: (i, 0, cb + 2 * npairs + h)),
                  pl.BlockSpec((2,) + bias.shape[1:], lambda i, h: (h, 0, 0, 0))],
        out_specs=pl.BlockSpec((1, t, LANES), lambda i, h: (i, 0, h)),
        out_shape=jax.ShapeDtypeStruct((b, t, npairs * LANES), BF16),
        scratch_shapes=[pltpu.VMEM((t, 2 * LANES), BF16)],
        compiler_params=_cparams("parallel", "arbitrary"),
    )(qkv, qkv, qkv, bias)


def _proj_out_kernel(x_ref, a_ref, ac_ref, b_ref, wa_ref, wb_ref, m_ref, g_ref, wr_ref, xo_ref, h_ref, aff_ref,
                     *, n_lat_tiles):
    mods = m_ref[0]
    a = jnp.where(pl.program_id(1) >= n_lat_tiles, ac_ref[0], a_ref[0])
    y = _dot(a, wa_ref[...]) + _dot(b_ref[0], wb_ref[...])
    x = x_ref[0] + mods[2:3] * y
    xo_ref[0] = x
    h2 = _normed(x, g_ref[...], mods, 3, 4)
    h_hi = h2.astype(BF16)
    h_ref[0] = h_hi
    h_lo = (h2 - h_hi.astype(F32)).astype(BF16)
    wr = wr_ref[...]
    w_hi = wr.astype(BF16)
    w_lo = (wr - w_hi.astype(F32)).astype(BF16)
    logits = _dot(h_hi, w_hi) + (_dot(h_hi, w_lo) + _dot(h_lo, w_hi))
    lane = lax.broadcasted_iota(jnp.int32, logits.shape, 1)
    logits = jnp.where(lane < N_EXPERTS, logits, NEG_BIG)
    e = jnp.exp(logits - jnp.max(logits, axis=-1, keepdims=True))
    aff_ref[0] = e / jnp.sum(e, axis=-1, keepdims=True)


def _proj_out(xa, a, a_ctx, bb, w_out, mods, gain, w_router, n_tiles, n_lat_tiles):
    b, t, d = xa.shape
    nb = mods.shape[0] - 1
    tm = TOK_TILE
    wa, wb = w_out[:a.shape[-1]], w_out[a.shape[-1]:]
    tok = lambda i, j: (i, j, 0)
    const = lambda i, j: (0, 0)
    return pl.pallas_call(
        functools.partial(_proj_out_kernel, n_lat_tiles=n_lat_tiles),
        grid=(b, n_tiles),
        in_specs=[pl.BlockSpec((1, tm, d), tok),
                  pl.BlockSpec((1, tm, a.shape[-1]), lambda i, j: (i, jnp.minimum(j, n_lat_tiles - 1), 0)),
                  pl.BlockSpec((1, tm, a.shape[-1]), lambda i, j: (i, jnp.maximum(j - n_lat_tiles, 0), 0)),
                  pl.BlockSpec((1, tm, bb.shape[-1]), tok),
                  pl.BlockSpec(wa.shape, const),
                  pl.BlockSpec(wb.shape, const),
                  pl.BlockSpec((1, 6, d), lambda i, j: (jnp.where(j >= n_lat_tiles, nb, i), 0, 0)),
                  pl.BlockSpec((1, d), const),
                  pl.BlockSpec(w_router.shape, const)],
        out_specs=[pl.BlockSpec((1, tm, d), tok),
                   pl.BlockSpec((1, tm, d), tok),
                   pl.BlockSpec((1, tm, LANES), tok)],
        out_shape=[jax.ShapeDtypeStruct((b, n_tiles * tm, d), F32),
                   jax.ShapeDtypeStruct((b, n_tiles * tm, d), BF16),
                   jax.ShapeDtypeStruct((b, n_tiles * tm, LANES), F32)],
        compiler_params=_cparams("parallel", "arbitrary"),
    )(xa, a, a_ctx, bb, wa, wb, mods, gain, w_router)


def _ffn_kernel(x_ref, gate_ref, w1_ref, w3_ref, w2_ref, o_ref, w1b, w3b, w2b):
    @pl.when(pl.program_id(1) == 0)
    def _():
        w1b[...] = w1_ref[0].astype(BF16)
        w3b[...] = w3_ref[0].astype(BF16)
        w2b[...] = w2_ref[0].astype(BF16)

    x = x_ref[0]
    f = w1b.shape[1]
    y = None
    for f0 in range(0, f, FFN_HIDDEN_BLOCK):
        f1 = f0 + FFN_HIDDEN_BLOCK
        hid = _silu(_dot(x, w1b[:, f0:f1])) * _dot(x, w3b[:, f0:f1])
        part = _dot(hid.astype(BF16), w2b[f0:f1, :])
        y = part if y is None else y + part
    o_ref[0] = (y * gate_ref[0]).astype(BF16)


def _expert_ffn(xe, gate, w1, w3, w2, layer):
    e, m, d = xe.shape
    f = w1.shape[-1]
    tm = m // FFN_ROW_TILES
    assert tm * FFN_ROW_TILES == m and tm % 16 == 0
    wspec = lambda r, c: pl.BlockSpec((None, 1, r, c), lambda i, j: (layer, i, 0, 0))
    return pl.pallas_call(
        _ffn_kernel,
        grid=(e, m // tm),
        in_specs=[pl.BlockSpec((1, tm, d), lambda i, j: (i, j, 0)),
                  pl.BlockSpec((1, tm, 1), lambda i, j: (i, j, 0)),
                  wspec(d, f), wspec(d, f), wspec(f, d)],
        out_specs=pl.BlockSpec((1, tm, d), lambda i, j: (i, j, 0)),
        out_shape=jax.ShapeDtypeStruct((e, m, d), BF16),
        scratch_shapes=[pltpu.VMEM((d, f), BF16), pltpu.VMEM((d, f), BF16), pltpu.VMEM((f, d), BF16)],
        compiler_params=_cparams("parallel", "arbitrary"),
    )(xe, gate, w1, w3, w2)


def _route(aff, segments):
    b, r, _ = aff.shape
    e = N_EXPERTS
    gates, flats, toks = [], [], []
    for row0, n in segments:
        cap = (EC_CAPACITY_FACTOR * n) // e
        gate, idx = lax.top_k(jnp.swapaxes(aff[:, row0:row0 + n, :e], 1, 2), cap)
        tok = idx + row0
        gates.append(jnp.swapaxes(gate, 0, 1).reshape(e, b * cap))
        flats.append(jnp.swapaxes(tok + jnp.arange(b, dtype=idx.dtype)[:, None, None] * r, 0, 1).reshape(e, b * cap))
        toks.append(tok)
    return jnp.concatenate(gates, axis=1)[..., None], jnp.concatenate(flats, axis=1), toks


def _combine_kernel(lo_ref, x_ref, tok_ref, y_ref, m_ref, g_ref, o_ref, *, final):
    i, j = pl.program_id(0), pl.program_id(1)
    tm = x_ref.shape[1]
    n_tiles = pl.num_programs(1)
    lo = lo_ref[i * (n_tiles + 1) + j]
    hi = lo_ref[i * (n_tiles + 1) + j + 1]
    rows = j * tm + lax.broadcasted_iota(jnp.int32, (tm, 1), 0)
    o_ref[0] = jnp.zeros(o_ref.shape[1:], F32)

    def chunk(c, carry):
        sel = (tok_ref[0, pl.ds(c, 1), :] == rows).astype(BF16)
        o_ref[0] += _dot(sel, y_ref[0, pl.ds(pl.multiple_of(c * PAIR_CHUNK, PAIR_CHUNK), PAIR_CHUNK), :])
        return carry

    lax.fori_loop(lo // PAIR_CHUNK, (hi + PAIR_CHUNK - 1) // PAIR_CHUNK, chunk, 0)
    x = x_ref[0] + m_ref[0][5:6] * o_ref[0]
    if final:
        x = _rms(x) * g_ref[...]
    o_ref[0] = x


def _combine(x, tok_sorted, y_sorted, lo, mods, gain, n_lat_tiles, final):
    b, t, d = x.shape
    p = tok_sorted.shape[1]
    nb = mods.shape[0] - 1
    tm = TOK_TILE
    tok = lambda i, j, lo_ref: (i, j, 0)
    whole = lambda i, j, lo_ref: (i, 0, 0)
    grid_spec = pltpu.PrefetchScalarGridSpec(
        num_scalar_prefetch=1,
        grid=(b, t // tm),
        in_specs=[pl.BlockSpec((1, tm, d), tok),
                  pl.BlockSpec((1, p // PAIR_CHUNK, PAIR_CHUNK), whole),
                  pl.BlockSpec((1, p, d), whole),
                  pl.BlockSpec((1, 6, d), lambda i, j, lo_ref: (jnp.where(j >= n_lat_tiles, nb, i), 0, 0)),
                  pl.BlockSpec((1, d), lambda i, j, lo_ref: (0, 0))],
        out_specs=pl.BlockSpec((1, tm, d), tok))
    return pl.pallas_call(
        functools.partial(_combine_kernel, final=final),
        grid_spec=grid_spec,
        out_shape=jax.ShapeDtypeStruct((b, t, d), F32),
        compiler_params=_cparams("parallel", "arbitrary"),
    )(lo, x, tok_sorted.reshape(b, p // PAIR_CHUNK, PAIR_CHUNK), y_sorted, mods, gain)


def _moe(x_mid, h2, aff, segments, mods, gain, n_lat_tiles, final, w1, w3, w2, layer):
    b, r, d = h2.shape
    gate, flat, toks = _route(aff, segments)
    e, m = flat.shape
    xe = h2.reshape(b * r, d).at[flat].get(mode="promise_in_bounds")
    ye = _expert_ffn(xe, gate, w1, w3, w2, layer)
    srcs, off = [], 0
    for tok in toks:
        cap = tok.shape[-1]
        src = (jnp.arange(e, dtype=jnp.int32)[None, :, None] * m + off
               + jnp.arange(b, dtype=jnp.int32)[:, None, None] * cap + jnp.arange(cap, dtype=jnp.int32))
        srcs.append(src.reshape(b, e * cap))
        off += b * cap
    tok_all = jnp.concatenate([tok.reshape(b, -1) for tok in toks], axis=1)
    tok_sorted, src_sorted = lax.sort_key_val(tok_all, jnp.concatenate(srcs, axis=1), dimension=1)
    y_sorted = ye.reshape(e * m, d).at[src_sorted].get(mode="promise_in_bounds")
    starts = jnp.arange(r // TOK_TILE + 1, dtype=jnp.int32) * TOK_TILE
    lo = jnp.sum((tok_sorted[:, None, :] < starts[None, :, None]).astype(jnp.int32), axis=-1)
    return _combine(x_mid, tok_sorted, y_sorted, lo.reshape(-1), mods, gain, n_lat_tiles, final)


def _proj_odd_kernel(x_ref, ml_ref, mc_ref, g_ref, w_ref, qn_ref, kvn_ref, wq_ref, wkv_ref, cos_ref, sin_ref,
                     q_ref, k_ref, v_ref, rq_ref, rk_ref, rv_ref, rg_ref, *, q_scale, n_lat):
    h = _mixer_input(x_ref[0], g_ref[...], ml_ref[0], mc_ref[0], n_lat)
    acc = _dot(h.astype(BF16), w_ref[...])
    cos, sin = cos_ref[...], sin_ref[...]
    c0 = MLA_Q_RANK
    c1 = c0 + MLA_KV_RANK
    cq = acc[:, :c0]
    ckv = acc[:, c0:c1]
    kr = acc[:, c1:c1 + LANES] * cos + acc[:, c1 + LANES:c1 + 2 * LANES] * sin
    c2 = c1 + 2 * LANES
    nq = RET_HEADS * RET_QK_DIM
    nv = RET_HEADS * RET_V_DIM
    rq_ref[0] = acc[:, c2:c2 + nq].astype(BF16)
    rk_ref[0] = acc[:, c2 + nq:c2 + 2 * nq].astype(BF16)
    rv_ref[0] = acc[:, c2 + 2 * nq:c2 + 2 * nq + nv].astype(BF16)
    rg_ref[0] = acc[:, c2 + 2 * nq + nv:]

    nqk = MLA_HEADS * LANES
    qq = _dot((_rms(cq) * qn_ref[...]).astype(BF16), wq_ref[...])
    q = qq[:, :nqk] * _tile_lanes(cos, MLA_HEADS) + qq[:, nqk:] * _tile_lanes(sin, MLA_HEADS)
    q_ref[0] = (q * q_scale).astype(BF16)
    kv = _dot((_rms(ckv) * kvn_ref[...]).astype(BF16), wkv_ref[...])
    k_ref[0] = (kv[:, :nqk] + _tile_lanes(kr, MLA_HEADS)).astype(BF16)
    v_ref[0] = kv[:, nqk:].astype(BF16)


def _proj_odd(xa, mods, gain, w, q_norm, kv_norm, wq, wkv, cos, sin, n_lat):
    b, t, d = xa.shape
    nb = mods.shape[0] - 1
    tm = _row_tile(t)
    tok = lambda i, j: (i, j, 0)
    const = lambda i, j: (0, 0)
    widths = (MLA_HEADS * LANES, MLA_HEADS * LANES, MLA_HEADS * MLA_V_DIM,
              RET_HEADS * RET_QK_DIM, RET_HEADS * RET_QK_DIM, RET_HEADS * RET_V_DIM, RET_HEADS * RET_V_DIM)
    dtypes = (BF16,) * 6 + (F32,)
    kern = functools.partial(_proj_odd_kernel, q_scale=float((MLA_NOPE_DIM + MLA_ROPE_DIM) ** -0.5) * LOG2E,
                             n_lat=n_lat)
    return pl.pallas_call(
        kern,
        grid=(b, t // tm),
        in_specs=[pl.BlockSpec((1, tm, d), tok),
                  pl.BlockSpec((1, 6, d), lambda i, j: (i, 0, 0)),
                  pl.BlockSpec((1, 6, d), lambda i, j: (nb, 0, 0)),
                  pl.BlockSpec((1, d), const),
                  pl.BlockSpec(w.shape, const),
                  pl.BlockSpec(q_norm.shape, const),
                  pl.BlockSpec(kv_norm.shape, const),
                  pl.BlockSpec(wq.shape, const),
                  pl.BlockSpec(wkv.shape, const),
                  pl.BlockSpec((tm, LANES), lambda i, j: (j, 0)),
                  pl.BlockSpec((tm, LANES), lambda i, j: (j, 0))],
        out_specs=[pl.BlockSpec((1, tm, wd), tok) for wd in widths],
        out_shape=[jax.ShapeDtypeStruct((b, t, wd), dt) for wd, dt in zip(widths, dtypes)],
        compiler_params=_cparams("parallel", "arbitrary"),
    )(xa, mods, mods, gain, w, q_norm, kv_norm, wq, wkv, cos, sin)


def _mla_kernel(q_ref, k_ref, v_ref, o_ref, vx_ref):
    @pl.when(pl.program_id(2) == 0)
    def _():
        _fill_v_ones(vx_ref, v_ref)

    o0, o1 = _two_stream_attention(q_ref[0, :, :LANES], [(k_ref[0, :, :LANES], None)],
                                   q_ref[0, :, LANES:], [(k_ref[0, :, LANES:], None)], [vx_ref[...]])
    lane = lax.broadcasted_iota(jnp.int32, (1, LANES), 1)
    o_ref[0] = jnp.where(lane < MLA_V_DIM, o0, o1).astype(BF16)


def _mla_attn(q, k, v, n_lat, tq):
    b, t, _ = k.shape
    npairs = MLA_HEADS // 2
    return pl.pallas_call(
        _mla_kernel,
        grid=(b, npairs, n_lat // tq),
        in_specs=[pl.BlockSpec((1, tq, 2 * LANES), lambda i, h, j: (i, j, h)),
                  pl.BlockSpec((1, t, 2 * LANES), lambda i, h, j: (i, 0, h)),
                  pl.BlockSpec((1, t, LANES), lambda i, h, j: (i, 0, h))],
        out_specs=pl.BlockSpec((1, tq, LANES), lambda i, h, j: (i, j, h)),
        out_shape=jax.ShapeDtypeStruct((b, n_lat, npairs * LANES), BF16),
        scratch_shapes=[pltpu.VMEM((t, 2 * LANES), BF16)],
        compiler_params=_cparams("parallel", "parallel", "arbitrary"),
    )(q, k, v)


def _retention_kernel(dl_ref, q_ref, k_ref, v_ref, g_ref, o_ref, f_ref, b_ref, st_ref, *, n_lat):
    c = RET_CHUNK
    t = k_ref.shape[1]
    n_chunks = n_lat // c
    n_ctx = (t - n_lat) // c
    hp = pl.program_id(1)
    lane = lax.broadcasted_iota(jnp.int32, (1, LANES), 1)
    pos_r = lax.broadcasted_iota(jnp.int32, (c, 1), 0).astype(F32)
    ii = lax.broadcasted_iota(jnp.int32, (c, c), 0)
    jj = lax.broadcasted_iota(jnp.int32, (c, c), 1)
    rel = (ii - jj).astype(F32)
    dl = dl_ref[...]
    hsel = lax.broadcasted_iota(jnp.int32, (1, RET_HEADS), 1)

    chains = []
    for hh in range(2):
        hmask = (lane >= hh * RET_QK_DIM) & (lane < (hh + 1) * RET_QK_DIM)
        for direction in range(2):
            logit = jnp.sum(jnp.where(hsel == 2 * hp + hh, dl[direction:direction + 1], 0.0),
                            axis=-1, keepdims=True)
            lg = jnp.minimum(logit, 0.0) - jnp.log(1.0 + jnp.exp(-jnp.abs(logit)))
            if direction == 0:
                mask = rel >= 0
                d_in = jnp.where(mask, jnp.exp(lg * jnp.where(mask, rel, 0.0)), 0.0)
                d_q = jnp.exp(lg * (pos_r + 1.0))
                d_k = jnp.exp(lg * (c - 1.0 - pos_r))
            else:
                mask = rel < 0
                d_in = jnp.where(mask, jnp.exp(lg * jnp.where(mask, -rel, 0.0)), 0.0)
                d_q = jnp.exp(lg * (c - pos_r))
                d_k = jnp.exp(lg * pos_r)
            chains.append(dict(idx=2 * hh + direction, v0=hh * RET_V_DIM, direction=direction, hmask=hmask,
                               d_in=d_in, d_q=d_q, d_k=d_k, d_chunk=jnp.exp(lg * c)))

    st_ref[...] = jnp.zeros(st_ref.shape, F32)

    def kv_update(ch, start):
        kb = k_ref[0, pl.ds(start, c), :].astype(F32)
        vb = v_ref[0, pl.ds(start, c), ch["v0"]:ch["v0"] + RET_V_DIM]
        kd = jnp.where(ch["hmask"], kb * ch["d_k"], 0.0).astype(BF16)
        st_ref[ch["idx"]] = ch["d_chunk"] * st_ref[ch["idx"]] + _dot_tn(kd, vb)

    for i in range(n_ctx):
        for ch in chains:
            ci = i if ch["direction"] == 0 else n_ctx - 1 - i
            kv_update(ch, n_lat + ci * c)

    def lat_step(i, carry):
        starts = [pl.multiple_of((i if ch["direction"] == 0 else n_chunks - 1 - i) * c, c) for ch in chains]
        qms, avs, crs = [], [], []
        for ch, start in zip(chains, starts):
            qb = q_ref[0, pl.ds(start, c), :]
            qm = jnp.where(ch["hmask"], qb, jnp.zeros_like(qb))
            qms.append(qm)
            avs.append(_dot_nt(qm, k_ref[0, pl.ds(start, c), :]))
        for ch, qm in zip(chains, qms):
            qd = (qm.astype(F32) * ch["d_q"]).astype(BF16)
            crs.append(_dot(qd, st_ref[ch["idx"]].astype(BF16)))
        for ch, start, a, cross in zip(chains, starts, avs, crs):
            vb = v_ref[0, pl.ds(start, c), ch["v0"]:ch["v0"] + RET_V_DIM]
            res = _dot((a * ch["d_in"]).astype(BF16), vb) + cross
            dst = f_ref if ch["direction"] == 0 else b_ref
            dst[pl.ds(start, c), ch["v0"]:ch["v0"] + RET_V_DIM] = res
        for ch, start in zip(chains, starts):
            kv_update(ch, start)
        return carry

    lax.fori_loop(0, n_chunks, lat_step, 0)

    def finish(i, carry):
        start = pl.multiple_of(i * c, c)
        for hh in range(2):
            v0 = hh * RET_V_DIM
            r = _rms(f_ref[pl.ds(start, c), v0:v0 + RET_V_DIM] + b_ref[pl.ds(start, c), v0:v0 + RET_V_DIM])
            gate = _silu(g_ref[0, pl.ds(start, c), v0:v0 + RET_V_DIM])
            o_ref[0, pl.ds(start, c), v0:v0 + RET_V_DIM] = (r * gate).astype(BF16)
        return carry

    lax.fori_loop(0, n_chunks, finish, 0)


def _retention(rq, rk, rv, rg, decay_logit, n_lat):
    b, t, _ = rq.shape
    npairs = RET_HEADS // 2
    kern = functools.partial(_retention_kernel, n_lat=n_lat)
    return pl.pallas_call(
        kern,
        grid=(b, npairs),
        in_specs=[pl.BlockSpec(decay_logit.shape, lambda i, h: (0, 0)),
                  pl.BlockSpec((1, n_lat, LANES), lambda i, h: (i, 0, h)),
                  pl.BlockSpec((1, t, LANES), lambda i, h: (i, 0, h)),
                  pl.BlockSpec((1, t, 2 * RET_V_DIM), lambda i, h: (i, 0, h)),
                  pl.BlockSpec((1, n_lat, 2 * RET_V_DIM), lambda i, h: (i, 0, h))],
        out_specs=pl.BlockSpec((1, n_lat, 2 * RET_V_DIM), lambda i, h: (i, 0, h)),
        out_shape=jax.ShapeDtypeStruct((b, n_lat, RET_HEADS * RET_V_DIM), BF16),
        scratch_shapes=[pltpu.VMEM((n_lat, 2 * RET_V_DIM), F32), pltpu.VMEM((n_lat, 2 * RET_V_DIM), F32),
                        pltpu.VMEM((4, LANES, RET_V_DIM), F32)],
        compiler_params=_cparams("parallel", "arbitrary"),
    )(decay_logit, rq, rk, rv, rg)


def _rope_tables(n_lat, n_ctx, rot_dim, lane0):
    t = np.arange(n_lat)
    rows = (t // GRID_W).astype(np.float32)
    cols = (t % GRID_W).astype(np.float32)
    m = rot_dim // 4
    freqs = jnp.asarray(ROPE_BASE, F32) ** (-jnp.arange(m, dtype=F32) / m)
    ang_r = jnp.asarray(rows)[:, None] * freqs
    ang_c = jnp.asarray(cols)[:, None] * freqs
    cos = jnp.concatenate([jnp.cos(ang_r)] * 2 + [jnp.cos(ang_c)] * 2, axis=-1)
    sin = jnp.concatenate([-jnp.sin(ang_r), jnp.sin(ang_r), -jnp.sin(ang_c), jnp.sin(ang_c)], axis=-1)
    reps = (LANES - lane0) // rot_dim if lane0 == 0 else 1
    cos = jnp.concatenate([jnp.ones((n_lat, lane0), F32)] + [cos] * reps
                          + [jnp.ones((n_lat, LANES - lane0 - reps * rot_dim), F32)], axis=-1)
    sin = jnp.concatenate([jnp.zeros((n_lat, lane0), F32)] + [sin] * reps
                          + [jnp.zeros((n_lat, LANES - lane0 - reps * rot_dim), F32)], axis=-1)
    cos = jnp.concatenate([cos, jnp.ones((n_ctx, LANES), F32)], axis=0)
    sin = jnp.concatenate([sin, jnp.zeros((n_ctx, LANES), F32)], axis=0)
    return cos, sin


def _rot_partner(w, rot_dim):
    k, n = w.shape
    q = rot_dim // 4
    return w.reshape(k, n // rot_dim, 2, 2, q)[:, :, :, ::-1, :].reshape(k, n)


def _pad_cols(w, lane0, width=LANES):
    k, n = w.shape
    return jnp.concatenate([jnp.zeros((k, lane0), w.dtype), w, jnp.zeros((k, width - lane0 - n), w.dtype)], axis=-1)


def _even_weights(w_in):
    n_rope = 2 * DIFF_HEADS * 2 * DIFF_HEAD_DIM
    dw = DIFF_HEADS * 2 * DIFF_HEAD_DIM
    nw = NA_HEADS * NA_HEAD_DIM
    scale = jnp.concatenate([jnp.full((dw,), DIFF_HEAD_DIM ** -0.5, F32), jnp.ones((2 * dw,), F32),
                             jnp.full((nw,), NA_HEAD_DIM ** -0.5, F32), jnp.ones((2 * nw,), F32)])
    w = w_in * scale
    return jnp.concatenate([w, _rot_partner(w[:, :n_rope], DIFF_HEAD_DIM)], axis=-1).astype(BF16), n_rope


def _odd_weights(w_in, w_uq, w_ukv):
    c0 = MLA_Q_RANK
    c1 = c0 + MLA_KV_RANK
    c2 = c1 + MLA_ROPE_DIM
    nq = RET_HEADS * RET_QK_DIM
    kr = w_in[:, c1:c2]
    w = jnp.concatenate([w_in[:, :c1],
                         _pad_cols(kr, MLA_NOPE_DIM), _pad_cols(_rot_partner(kr, MLA_ROPE_DIM), MLA_NOPE_DIM),
                         w_in[:, c2:c2 + nq], w_in[:, c2 + nq:c2 + 2 * nq] * (RET_QK_DIM ** -0.5),
                         w_in[:, c2 + 2 * nq:]], axis=-1).astype(BF16)
    r = w_uq.shape[0]
    uq = w_uq.reshape(r, MLA_HEADS, MLA_NOPE_DIM + MLA_ROPE_DIM)
    pad = jnp.zeros((r, MLA_HEADS, LANES - MLA_NOPE_DIM - MLA_ROPE_DIM), F32)
    uq_rot = _rot_partner(uq[:, :, MLA_NOPE_DIM:].reshape(r, -1), MLA_ROPE_DIM).reshape(r, MLA_HEADS, MLA_ROPE_DIM)
    wq = jnp.concatenate([uq, pad], axis=-1).reshape(r, -1)
    wq_rot = jnp.concatenate([jnp.zeros_like(uq[:, :, :MLA_NOPE_DIM]), uq_rot, pad], axis=-1).reshape(r, -1)
    rk = w_ukv.shape[0]
    ukv = w_ukv.reshape(rk, MLA_HEADS, MLA_NOPE_DIM + MLA_V_DIM)
    wk = jnp.concatenate([ukv[:, :, :MLA_NOPE_DIM], jnp.zeros((rk, MLA_HEADS, LANES - MLA_NOPE_DIM), F32)],
                         axis=-1).reshape(rk, -1)
    wv = ukv[:, :, MLA_NOPE_DIM:].reshape(rk, -1)
    return w, jnp.concatenate([wq, wq_rot], axis=-1).astype(BF16), jnp.concatenate([wk, wv], axis=-1).astype(BF16)


def _na_bias(rpb, n_lat):
    pat_of_group, idx_r, idx_c, valid = _na_patterns(n_lat // GRID_W)
    g = NA_GROUP_ROWS
    span = g + NA_WIN_ROWS - 1
    n_pat = idx_r.shape[0]
    col_sel = (idx_c[0, :GRID_W, :GRID_W, None] == np.arange(2 * NA_WIN_COLS - 1)).astype(np.float32)
    row_idx = idx_r.reshape(n_pat, g, GRID_W, span, GRID_W)[:, :, 0, :, 0]
    row_sel = (row_idx[..., None] == np.arange(2 * NA_WIN_ROWS - 1)).astype(np.float32)
    cols = jnp.einsum('hrc,qkc->hrqk', rpb, col_sel, precision=lax.Precision.HIGHEST)
    bias = jnp.einsum('pijr,hrqk->hpiqjk', row_sel, cols, precision=lax.Precision.HIGHEST)
    bias = bias.reshape(rpb.shape[0], n_pat, g * GRID_W, span * GRID_W)
    return jnp.where(valid[None], bias * LOG2E, NEG_BIG), pat_of_group


def _router_weights(w_router):
    return _pad_cols(w_router, 0)


def kernel(x, c, ctx, c_ctx, ada_w, ada_b, norm_mix, norm_ffn, final_norm, even_w_in, even_w_out,
           diff_lambda, diff_subln, na_rpb, odd_w_in, odd_w_out, mla_q_norm, mla_w_uq, mla_kv_norm,
           mla_w_ukv, ret_decay_logit, moe_router, moe_w1, moe_w3, moe_w2):
    b, n_lat, d = x.shape
    n_ctx = ctx.shape[1]
    depth = ada_w.shape[0]
    n_lat_tiles = n_lat // TOK_TILE

    cond = jnp.concatenate([c, c_ctx[None], jnp.zeros((7, d), F32)], axis=0)
    mods_all = _mods(cond, ada_w, ada_b)[:, :b + 1].reshape(depth, b + 1, 6, d)
    xa = jnp.concatenate([x, ctx], axis=1)

    for l in range(depth):
        mods = mods_all[l]
        need_ctx = l < depth - 1
        i = l // 2
        gain_mix = norm_mix[l][None]
        if l % 2 == 0:
            w, n_rope = _even_weights(even_w_in[i])
            cos, sin = _rope_tables(n_lat, n_ctx, DIFF_HEAD_DIM, 0)
            qkv = _proj_even(xa, mods, gain_mix, w, cos, sin, n_lat, n_rope, even_w_in.shape[-1])
            lam_init = 0.8 - 0.6 * math.exp(-0.3 * l)
            mix_a = _diff_attn(qkv, diff_lambda[i], diff_subln[i][None], lam_init, 0, n_lat, ATTN_Q_TILE,
                               0, n_lat + n_ctx)
            mix_a_ctx = _diff_attn(qkv, diff_lambda[i], diff_subln[i][None], lam_init, n_lat, n_ctx, n_ctx,
                                   n_lat, n_ctx)
            bias, offsets = _na_bias(na_rpb[i], n_lat)
            mix_b = _na_attn(qkv, bias, n_lat, offsets, 3 * DIFF_HEADS * 2 * DIFF_HEAD_DIM)
            w_out = even_w_out[i].astype(BF16)
        else:
            w, wq, wkv = _odd_weights(odd_w_in[i], mla_w_uq[i], mla_w_ukv[i])
            cos, sin = _rope_tables(n_lat, n_ctx, MLA_ROPE_DIM, MLA_NOPE_DIM)
            q, k, v, rq, rk, rv, rg = _proj_odd(xa, mods, gain_mix, w, mla_q_norm[i][None], mla_kv_norm[i][None],
                                                wq, wkv, cos, sin, n_lat)
            mix_a = _mla_attn(q, k, v, n_lat, ATTN_Q_TILE)
            mix_a_ctx = mix_a
            mix_b = _retention(rq, rk, rv, rg, ret_decay_logit[i], n_lat)
            w_out = odd_w_out[i].astype(BF16)

        tiles = (n_lat + n_ctx) // TOK_TILE if need_ctx else n_lat_tiles
        x_mid, h2, aff = _proj_out(xa, mix_a, mix_a_ctx, mix_b, w_out, mods, norm_ffn[l][None],
                                   _router_weights(moe_router[l]), tiles, n_lat_tiles)
        segments = ((0, n_lat), (n_lat, n_ctx)) if need_ctx else ((0, n_lat),)
        xa = _moe(x_mid, h2, aff, segments, mods, final_norm[None], n_lat_tiles, l == depth - 1,
                  moe_w1, moe_w3, moe_w2, l)
    return xa
```

```python
import functools
import math

import jax
import jax.numpy as jnp
import numpy as np
from jax import lax
from jax.experimental import pallas as pl
from jax.experimental.pallas import tpu as pltpu

GRID_W = 64
ROPE_BASE = 10000.0
RMS_EPS = 1e-6
DIFF_HEADS = 4
DIFF_HEAD_DIM = 64
NA_HEADS = 8
NA_HEAD_DIM = 64
NA_WIN_ROWS = 8
NA_WIN_COLS = 16
MLA_HEADS = 8
MLA_Q_RANK = 256
MLA_KV_RANK = 128
MLA_NOPE_DIM = 64
MLA_ROPE_DIM = 32
MLA_V_DIM = 64
RET_HEADS = 4
RET_QK_DIM = 64
RET_V_DIM = 128
RET_CHUNK = 128
N_EXPERTS = 16
EC_CAPACITY_FACTOR = 2

LANES = 128
VMEM_LIMIT = 56 * 1024 * 1024
NEG_BIG = -1e30
TOK_TILE = 256
ROW_TILES = 8
ATTN_Q_TILE = 512
FFN_ROW_TILES = 8
FFN_HIDDEN_BLOCK = 512
NA_GROUP_ROWS = 8
NA_GROUPS_PER_STEP = 2
PAIR_CHUNK = 256
LOG2E = math.log2(math.e)

BF16 = jnp.bfloat16
F32 = jnp.float32


def _cparams(*sem):
    return pltpu.CompilerParams(dimension_semantics=sem, vmem_limit_bytes=VMEM_LIMIT)


def _dot(a, b):
    return jnp.dot(a, b, preferred_element_type=F32)


def _dot_nt(a, b):
    return lax.dot_general(a, b, (((1,), (1,)), ((), ())), preferred_element_type=F32)


def _dot_tn(a, b):
    return lax.dot_general(a, b, (((0,), (0,)), ((), ())), preferred_element_type=F32)


def _rms(x):
    return x * lax.rsqrt(jnp.mean(x * x, axis=-1, keepdims=True) + RMS_EPS)


def _silu(x):
    return x * (1.0 / (1.0 + jnp.exp(-x)))


def _tile_lanes(t, n):
    return jnp.concatenate([t] * n, axis=-1)


def _mods_kernel(c_ref, w_ref, b_ref, o_ref):
    o_ref[0] = _dot(_silu(c_ref[...]), w_ref[0]) + b_ref[0]


def _mods(cond, ada_w, ada_b):
    depth, d, n = ada_w.shape
    r = cond.shape[0]
    tn = 1536
    return pl.pallas_call(
        _mods_kernel,
        grid=(depth, n // tn),
        in_specs=[pl.BlockSpec((r, d), lambda l, j: (0, 0)),
                  pl.BlockSpec((1, d, tn), lambda l, j: (l, 0, j)),
                  pl.BlockSpec((1, 1, tn), lambda l, j: (l, 0, j))],
        out_specs=pl.BlockSpec((1, r, tn), lambda l, j: (l, 0, j)),
        out_shape=jax.ShapeDtypeStruct((depth, r, n), F32),
        compiler_params=_cparams("arbitrary", "arbitrary"),
    )(cond, ada_w, ada_b.reshape(depth, 1, n))


def _normed(x, gain, mods, shift_row, scale_row):
    return _rms(x) * gain * (1.0 + mods[scale_row:scale_row + 1]) + mods[shift_row:shift_row + 1]


def _mixer_input(x, gain, mods_lat, mods_ctx, n_lat):
    tm = x.shape[0]
    rows = pl.program_id(1) * tm + lax.broadcasted_iota(jnp.int32, (tm, 1), 0)
    is_ctx = rows >= n_lat
    scale = jnp.where(is_ctx, mods_ctx[1:2], mods_lat[1:2])
    shift = jnp.where(is_ctx, mods_ctx[0:1], mods_lat[0:1])
    return _rms(x) * gain * (1.0 + scale) + shift


def _proj_even_kernel(x_ref, ml_ref, mc_ref, g_ref, w_ref, cos_ref, sin_ref, o_ref, *, n_rope, n_out, n_lat):
    na_q0 = n_rope + n_rope // 2
    na_q1 = na_q0 + NA_HEADS * NA_HEAD_DIM
    h = _mixer_input(x_ref[0], g_ref[...], ml_ref[0], mc_ref[0], n_lat)
    acc = _dot(h.astype(BF16), w_ref[...])
    reps = n_rope // LANES
    cos = _tile_lanes(cos_ref[...], reps)
    sin = _tile_lanes(sin_ref[...], reps)
    roped = acc[:, :n_rope] * cos + acc[:, n_out:] * sin
    n_q = n_rope // 2
    o_ref[0, :, :n_q] = (roped[:, :n_q] * LOG2E).astype(BF16)
    o_ref[0, :, n_q:n_rope] = roped[:, n_q:].astype(BF16)
    o_ref[0, :, n_rope:na_q0] = acc[:, n_rope:na_q0].astype(BF16)
    o_ref[0, :, na_q0:na_q1] = (acc[:, na_q0:na_q1] * LOG2E).astype(BF16)
    o_ref[0, :, na_q1:] = acc[:, na_q1:n_out].astype(BF16)


def _row_tile(t):
    tm = t // ROW_TILES
    assert tm * ROW_TILES == t and tm % 16 == 0
    return tm


def _proj_even(xa, mods, gain, w, cos, sin, n_lat, n_rope, n_out):
    b, t, d = xa.shape
    nb = mods.shape[0] - 1
    tm = _row_tile(t)
    kern = functools.partial(_proj_even_kernel, n_rope=n_rope, n_out=n_out, n_lat=n_lat)
    return pl.pallas_call(
        kern,
        grid=(b, t // tm),
        in_specs=[pl.BlockSpec((1, tm, d), lambda i, j: (i, j, 0)),
                  pl.BlockSpec((1, 6, d), lambda i, j: (i, 0, 0)),
                  pl.BlockSpec((1, 6, d), lambda i, j: (nb, 0, 0)),
                  pl.BlockSpec((1, d), lambda i, j: (0, 0)),
                  pl.BlockSpec(w.shape, lambda i, j: (0, 0)),
                  pl.BlockSpec((tm, LANES), lambda i, j: (j, 0)),
                  pl.BlockSpec((tm, LANES), lambda i, j: (j, 0))],
        out_specs=pl.BlockSpec((1, tm, n_out), lambda i, j: (i, j, 0)),
        out_shape=jax.ShapeDtypeStruct((b, t, n_out), BF16),
        compiler_params=_cparams("parallel", "arbitrary"),
    )(xa, mods, mods, gain, w, cos, sin)


def _fill_v_ones(vx_ref, v_ref):
    vx_ref[:, :LANES] = v_ref[0]
    vx_ref[:, LANES:] = jnp.ones((vx_ref.shape[0], LANES), BF16)


def _weights(s, m):
    return jnp.exp2(s - m).astype(BF16)


def _rowmax(*ss):
    m = jnp.max(ss[0], axis=-1, keepdims=True)
    for s in ss[1:]:
        m = jnp.maximum(m, jnp.max(s, axis=-1, keepdims=True))
    return m


def _normalised(ox):
    return ox[:, :LANES] / ox[:, LANES:LANES + 1]


def _scores(q, segs):
    return [_dot_nt(q, k) if bias is None else _dot_nt(q, k) + bias for k, bias in segs]


def _pv(ps, vxs):
    o = _dot(ps[0], vxs[0])
    for p, vx in zip(ps[1:], vxs[1:]):
        o = o + _dot(p, vx)
    return o


def _multi_stream_attention(streams):
    n = len(streams)
    s, m, p, o = [None] * n, [None] * n, [None] * n, [None] * n
    for t in range(n + 2):
        if t < n:
            s[t] = _scores(streams[t][0], streams[t][1])
        if 0 <= t - 2 < n:
            o[t - 2] = _normalised(_pv(p[t - 2], streams[t - 2][2]))
        if 0 <= t - 1 < n:
            p[t - 1] = [_weights(x, m[t - 1]) for x in s[t - 1]]
        if t < n:
            m[t] = _rowmax(*s[t])
    return o


def _two_stream_attention(qa, segs_a, qb, segs_b, vxs):
    return _multi_stream_attention([(qa, segs_a, vxs), (qb, segs_b, vxs)])


def _diff_attn_kernel(lam_ref, g_ref, q_ref, k_ref, v_ref, o_ref, vx_ref, *, lam_init):
    @pl.when(pl.program_id(2) == 0)
    def _():
        _fill_v_ones(vx_ref, v_ref)

    lp = lam_ref[...]
    s1 = jnp.sum(lp[0:1] * lp[1:2], axis=-1, keepdims=True)
    s2 = jnp.sum(lp[2:3] * lp[3:4], axis=-1, keepdims=True)
    lam = jnp.exp(s1) - jnp.exp(s2) + lam_init
    q = q_ref[0]
    lane = lax.broadcasted_iota(jnp.int32, (1, LANES), 1)
    q1 = jnp.where(lane < DIFF_HEAD_DIM, q, jnp.zeros_like(q))
    q2 = jnp.where(lane >= DIFF_HEAD_DIM, q, jnp.zeros_like(q))

    k = k_ref[0]
    o1, o2 = _two_stream_attention(q1, [(k, None)], q2, [(k, None)], [vx_ref[...]])
    o = o1 - lam * o2
    o_ref[0] = (_rms(o) * g_ref[...] * (1.0 - lam_init)).astype(BF16)


def _diff_attn(qkv, lam_params, subln, lam_init, q_row0, n_q, tq, k_row0, n_k):
    b = qkv.shape[0]
    nh = DIFF_HEADS
    q0, k0 = q_row0 // tq, k_row0 // n_k
    assert q0 * tq == q_row0 and k0 * n_k == k_row0 and n_q % tq == 0
    return pl.pallas_call(
        functools.partial(_diff_attn_kernel, lam_init=lam_init),
        grid=(b, nh, n_q // tq),
        in_specs=[pl.BlockSpec(lam_params.shape, lambda i, h, j: (0, 0)),
                  pl.BlockSpec((1, LANES), lambda i, h, j: (0, 0)),
                  pl.BlockSpec((1, tq, LANES), lambda i, h, j: (i, q0 + j, h)),
                  pl.BlockSpec((1, n_k, LANES), lambda i, h, j: (i, k0, nh + h)),
                  pl.BlockSpec((1, n_k, LANES), lambda i, h, j: (i, k0, 2 * nh + h))],
        out_specs=pl.BlockSpec((1, tq, LANES), lambda i, h, j: (i, j, h)),
        out_shape=jax.ShapeDtypeStruct((b, n_q, nh * LANES), BF16),
        scratch_shapes=[pltpu.VMEM((n_k, 2 * LANES), BF16)],
        compiler_params=_cparams("parallel", "parallel", "arbitrary"),
    )(lam_params, subln, qkv, qkv, qkv)


def _na_patterns(rows):
    g = NA_GROUP_ROWS
    span = g + NA_WIN_ROWS - 1
    nq, nk = g * GRID_W, span * GRID_W
    qi = np.arange(nq)[:, None]
    ki = np.arange(nk)[None, :]
    qcol, kcol = qi % GRID_W, ki % GRID_W
    cs = np.clip(qcol - NA_WIN_COLS // 2, 0, GRID_W - NA_WIN_COLS)
    col_ok = (kcol >= cs) & (kcol < cs + NA_WIN_COLS)
    idx_c = np.clip(kcol - qcol + NA_WIN_COLS - 1, 0, 2 * NA_WIN_COLS - 2) + 0 * qi
    seen, pat_of_group, idx_r, valid = {}, [], [], []
    for r0 in range(0, rows, g):
        ks = min(max(r0 - NA_WIN_ROWS // 2, 0), rows - span)
        qrow = r0 + qi // GRID_W
        krow = ks + ki // GRID_W
        rs = np.clip(qrow - NA_WIN_ROWS // 2, 0, rows - NA_WIN_ROWS)
        ok = (krow >= rs) & (krow < rs + NA_WIN_ROWS) & col_ok
        ir = np.clip(krow - qrow + NA_WIN_ROWS - 1, 0, 2 * NA_WIN_ROWS - 2) + 0 * kcol
        key = (ok.tobytes(), np.where(ok, ir, 0).tobytes())
        if key not in seen:
            seen[key] = len(idx_r)
            idx_r.append(ir)
            valid.append(ok)
        pat_of_group.append(seen[key])
    n_pat = len(idx_r)
    return tuple(pat_of_group), np.stack(idx_r), np.stack([idx_c] * n_pat), np.stack(valid)


def _na_kernel(q_ref, k_ref, v_ref, bias_ref, o_ref, vx_ref, *, n_lat, pat_of_group):
    g_rows = NA_GROUP_ROWS
    span = g_rows + NA_WIN_ROWS - 1
    rows = n_lat // GRID_W
    nq, nk = g_rows * GRID_W, span * GRID_W
    lane = lax.broadcasted_iota(jnp.int32, (1, LANES), 1)
    head_masks = (lane < NA_HEAD_DIM, lane >= NA_HEAD_DIM)
    _fill_v_ones(vx_ref, v_ref)
    kc = k_ref[0, n_lat:, :]
    vxc = vx_ref[n_lat:, :]

    def both_heads(blocks):
        streams = []
        for q, segs, vxs in blocks:
            for hh, hm in enumerate(head_masks):
                streams.append((jnp.where(hm, q, jnp.zeros_like(q)),
                                [(k, None if bi is None else bias_ref[hh, bi]) for k, bi in segs], vxs))
        outs = _multi_stream_attention(streams)
        return [jnp.where(head_masks[0], outs[2 * i], outs[2 * i + 1]).astype(BF16) for i in range(len(blocks))]

    def group_block(g):
        r0 = g * g_rows
        ks = jnp.clip(r0 - NA_WIN_ROWS // 2, 0, rows - span)
        common = max(set(pat_of_group), key=pat_of_group.count)
        pat = jnp.int32(common)
        for gi, p in enumerate(pat_of_group):
            if p != common:
                pat = jnp.where(g == gi, p, pat)
        qg = q_ref[0, pl.ds(pl.multiple_of(g * nq, nq), nq), :]
        kw = k_ref[0, pl.ds(pl.multiple_of(ks * GRID_W, GRID_W), nk), :]
        vxw = vx_ref[pl.ds(pl.multiple_of(ks * GRID_W, GRID_W), nk), :]
        return qg, [(kw, pat), (kc, None)], [vxw, vxc]

    def group_step(i, carry):
        gs = [i * NA_GROUPS_PER_STEP + u for u in range(NA_GROUPS_PER_STEP)]
        for g, out in zip(gs, both_heads([group_block(g) for g in gs])):
            o_ref[0, pl.ds(pl.multiple_of(g * nq, nq), nq), :] = out
        return carry

    lax.fori_loop(0, rows // g_rows // NA_GROUPS_PER_STEP, group_step, 0)

    o_ref[0, n_lat:, :] = both_heads([(q_ref[0, n_lat:, :], [(kc, None)], [vxc])])[0]


def _na_attn(qkv, bias, n_lat, pat_of_group, col0):
    b, t, _ = qkv.shape
    npairs = NA_HEADS // 2
    cb = col0 // LANES
    kern = functools.partial(_na_kernel, n_lat=n_lat, pat_of_group=pat_of_group)
    return pl.pallas_call(
        kern,
        grid=(npairs, b),
        in_specs=[pl.BlockSpec((1, t, LANES), lambda h, i: (i, 0, cb + h)),
                  pl.BlockSpec((1, t, LANES), lambda h, i: (i, 0, cb + npairs + h)),
                  pl.BlockSpec((1, t, LANES), lambda h, i: (i, 0, cb + 2 * npairs + h)),
                  pl.BlockSpec((2,) + bias.shape[1:], lambda h, i: (h, 0, 0, 0), pipeline_mode=pl.Buffered(1))],
        out_specs=pl.BlockSpec((1, t, LANES), lambda h, i: (i, 0, h)),
        out_shape=jax.ShapeDtypeStruct((b, t, npairs * LANES), BF16),
        scratch_shapes=[pltpu.VMEM((t, 2 * LANES), BF16)],
        compiler_params=_cparams("parallel", "arbitrary"),
    )(qkv, qkv, qkv, bias)


def _proj_out_kernel(x_ref, a_ref, ac_ref, b_ref, wa_ref, wb_ref, m_ref, g_ref, wr_ref, xo_ref, h_ref, aff_ref,
                     *, n_lat_tiles):
    mods = m_ref[0]
    a = jnp.where(pl.program_id(1) >= n_lat_tiles, ac_ref[0], a_ref[0])
    y = _dot(a, wa_ref[...]) + _dot(b_ref[0], wb_ref[...])
    x = x_ref[0] + mods[2:3] * y
    xo_ref[0] = x
    h2 = _normed(x, g_ref[...], mods, 3, 4)
    h_hi = h2.astype(BF16)
    h_ref[0] = h_hi
    h_lo = (h2 - h_hi.astype(F32)).astype(BF16)
    wr = wr_ref[...]
    w_hi = wr.astype(BF16)
    w_lo = (wr - w_hi.astype(F32)).astype(BF16)
    logits = _dot(h_hi, w_hi) + (_dot(h_hi, w_lo) + _dot(h_lo, w_hi))
    lane = lax.broadcasted_iota(jnp.int32, logits.shape, 1)
    logits = jnp.where(lane < N_EXPERTS, logits, NEG_BIG)
    e = jnp.exp(logits - jnp.max(logits, axis=-1, keepdims=True))
    aff_ref[0] = e / jnp.sum(e, axis=-1, keepdims=True)


def _proj_out(xa, a, a_ctx, bb, w_out, mods, gain, w_router, n_tiles, n_lat_tiles):
    b, t, d = xa.shape
    nb = mods.shape[0] - 1
    tm = TOK_TILE
    wa, wb = w_out[:a.shape[-1]], w_out[a.shape[-1]:]
    tok = lambda i, j: (i, j, 0)
    const = lambda i, j: (0, 0)
    return pl.pallas_call(
        functools.partial(_proj_out_kernel, n_lat_tiles=n_lat_tiles),
        grid=(b, n_tiles),
        in_specs=[pl.BlockSpec((1, tm, d), tok),
                  pl.BlockSpec((1, tm, a.shape[-1]), lambda i, j: (i, jnp.minimum(j, n_lat_tiles - 1), 0)),
                  pl.BlockSpec((1, tm, a.shape[-1]), lambda i, j: (i, jnp.maximum(j - n_lat_tiles, 0), 0)),
                  pl.BlockSpec((1, tm, bb.shape[-1]), tok),
                  pl.BlockSpec(wa.shape, const),
                  pl.BlockSpec(wb.shape, const),
                  pl.BlockSpec((1, 6, d), lambda i, j: (jnp.where(j >= n_lat_tiles, nb, i), 0, 0)),
                  pl.BlockSpec((1, d), const),
                  pl.BlockSpec(w_router.shape, const)],
        out_specs=[pl.BlockSpec((1, tm, d), tok),
                   pl.BlockSpec((1, tm, d), tok),
                   pl.BlockSpec((1, tm, LANES), tok)],
        out_shape=[jax.ShapeDtypeStruct((b, n_tiles * tm, d), F32),
                   jax.ShapeDtypeStruct((b, n_tiles * tm, d), BF16),
                   jax.ShapeDtypeStruct((b, n_tiles * tm, LANES), F32)],
        compiler_params=_cparams("parallel", "arbitrary"),
    )(xa, a, a_ctx, bb, wa, wb, mods, gain, w_router)


def _ffn_kernel(x_ref, gate_ref, w1_ref, w3_ref, w2_ref, o_ref, w1b, w3b, w2b):
    @pl.when(pl.program_id(1) == 0)
    def _():
        w1b[...] = w1_ref[0].astype(BF16)
        w3b[...] = w3_ref[0].astype(BF16)
        w2b[...] = w2_ref[0].astype(BF16)

    x = x_ref[0]
    f = w1b.shape[1]
    y = None
    for f0 in range(0, f, FFN_HIDDEN_BLOCK):
        f1 = f0 + FFN_HIDDEN_BLOCK
        hid = _silu(_dot(x, w1b[:, f0:f1])) * _dot(x, w3b[:, f0:f1])
        part = _dot(hid.astype(BF16), w2b[f0:f1, :])
        y = part if y is None else y + part
    o_ref[0] = (y * gate_ref[0]).astype(BF16)


def _expert_ffn(xe, gate, w1, w3, w2, layer):
    e, m, d = xe.shape
    f = w1.shape[-1]
    tm = m // FFN_ROW_TILES
    assert tm * FFN_ROW_TILES == m and tm % 16 == 0
    wspec = lambda r, c: pl.BlockSpec((None, 1, r, c), lambda i, j: (layer, i, 0, 0))
    return pl.pallas_call(
        _ffn_kernel,
        grid=(e, m // tm),
        in_specs=[pl.BlockSpec((1, tm, d), lambda i, j: (i, j, 0)),
                  pl.BlockSpec((1, tm, 1), lambda i, j: (i, j, 0)),
                  wspec(d, f), wspec(d, f), wspec(f, d)],
        out_specs=pl.BlockSpec((1, tm, d), lambda i, j: (i, j, 0)),
        out_shape=jax.ShapeDtypeStruct((e, m, d), BF16),
        scratch_shapes=[pltpu.VMEM((d, f), BF16), pltpu.VMEM((d, f), BF16), pltpu.VMEM((f, d), BF16)],
        compiler_params=_cparams("parallel", "arbitrary"),
    )(xe, gate, w1, w3, w2)


def _route(aff, segments):
    b, r, _ = aff.shape
    e = N_EXPERTS
    gates, flats, toks = [], [], []
    for row0, n in segments:
        cap = (EC_CAPACITY_FACTOR * n) // e
        gate, idx = lax.top_k(jnp.swapaxes(aff[:, row0:row0 + n, :e], 1, 2), cap)
        tok = idx + row0
        gates.append(jnp.swapaxes(gate, 0, 1).reshape(e, b * cap))
        flats.append(jnp.swapaxes(tok + jnp.arange(b, dtype=idx.dtype)[:, None, None] * r, 0, 1).reshape(e, b * cap))
        toks.append(tok)
    return jnp.concatenate(gates, axis=1)[..., None], jnp.concatenate(flats, axis=1), toks


def _combine_kernel(lo_ref, x_ref, tok_ref, y_ref, m_ref, g_ref, o_ref, *, final):
    i, j = pl.program_id(0), pl.program_id(1)
    tm = x_ref.shape[1]
    n_tiles = pl.num_programs(1)
    lo = lo_ref[i * (n_tiles + 1) + j]
    hi = lo_ref[i * (n_tiles + 1) + j + 1]
    rows = j * tm + lax.broadcasted_iota(jnp.int32, (tm, 1), 0)
    o_ref[0] = jnp.zeros(o_ref.shape[1:], F32)

    def chunk(c, carry):
        sel = (tok_ref[0, pl.ds(c, 1), :] == rows).astype(BF16)
        o_ref[0] += _dot(sel, y_ref[0, pl.ds(pl.multiple_of(c * PAIR_CHUNK, PAIR_CHUNK), PAIR_CHUNK), :])
        return carry

    lax.fori_loop(lo // PAIR_CHUNK, (hi + PAIR_CHUNK - 1) // PAIR_CHUNK, chunk, 0)
    x = x_ref[0] + m_ref[0][5:6] * o_ref[0]
    if final:
        x = _rms(x) * g_ref[...]
    o_ref[0] = x


def _combine(x, tok_sorted, y_sorted, lo, mods, gain, n_lat_tiles, final):
    b, t, d = x.shape
    p = tok_sorted.shape[1]
    nb = mods.shape[0] - 1
    tm = TOK_TILE
    tok = lambda i, j, lo_ref: (i, j, 0)
    whole = lambda i, j, lo_ref: (i, 0, 0)
    grid_spec = pltpu.PrefetchScalarGridSpec(
        num_scalar_prefetch=1,
        grid=(b, t // tm),
        in_specs=[pl.BlockSpec((1, tm, d), tok),
                  pl.BlockSpec((1, p // PAIR_CHUNK, PAIR_CHUNK), whole),
                  pl.BlockSpec((1, p, d), whole),
                  pl.BlockSpec((1, 6, d), lambda i, j, lo_ref: (jnp.where(j >= n_lat_tiles, nb, i), 0, 0)),
                  pl.BlockSpec((1, d), lambda i, j, lo_ref: (0, 0))],
        out_specs=pl.BlockSpec((1, tm, d), tok))
    return pl.pallas_call(
        functools.partial(_combine_kernel, final=final),
        grid_spec=grid_spec,
        out_shape=jax.ShapeDtypeStruct((b, t, d), F32),
        compiler_params=_cparams("parallel", "arbitrary"),
    )(lo, x, tok_sorted.reshape(b, p // PAIR_CHUNK, PAIR_CHUNK), y_sorted, mods, gain)


def _moe(x_mid, h2, aff, segments, mods, gain, n_lat_tiles, final, w1, w3, w2, layer):
    b, r, d = h2.shape
    gate, flat, toks = _route(aff, segments)
    e, m = flat.shape
    xe = h2.reshape(b * r, d).at[flat].get(mode="promise_in_bounds")
    ye = _expert_ffn(xe, gate, w1, w3, w2, layer)
    srcs, off = [], 0
    for tok in toks:
        cap = tok.shape[-1]
        src = (jnp.arange(e, dtype=jnp.int32)[None, :, None] * m + off
               + jnp.arange(b, dtype=jnp.int32)[:, None, None] * cap + jnp.arange(cap, dtype=jnp.int32))
        srcs.append(src.reshape(b, e * cap))
        off += b * cap
    tok_all = jnp.concatenate([tok.reshape(b, -1) for tok in toks], axis=1)
    tok_sorted, src_sorted = lax.sort_key_val(tok_all, jnp.concatenate(srcs, axis=1), dimension=1)
    y_sorted = ye.reshape(e * m, d).at[src_sorted].get(mode="promise_in_bounds")
    starts = jnp.arange(r // TOK_TILE + 1, dtype=jnp.int32) * TOK_TILE
    lo = jnp.sum((tok_sorted[:, None, :] < starts[None, :, None]).astype(jnp.int32), axis=-1)
    return _combine(x_mid, tok_sorted, y_sorted, lo.reshape(-1), mods, gain, n_lat_tiles, final)


def _proj_odd_kernel(x_ref, ml_ref, mc_ref, g_ref, w_ref, qn_ref, kvn_ref, wq_ref, wkv_ref, cos_ref, sin_ref,
                     q_ref, k_ref, v_ref, rq_ref, rk_ref, rv_ref, rg_ref, *, q_scale, n_lat):
    h = _mixer_input(x_ref[0], g_ref[...], ml_ref[0], mc_ref[0], n_lat)
    acc = _dot(h.astype(BF16), w_ref[...])
    cos, sin = cos_ref[...], sin_ref[...]
    c0 = MLA_Q_RANK
    c1 = c0 + MLA_KV_RANK
    cq = acc[:, :c0]
    ckv = acc[:, c0:c1]
    kr = acc[:, c1:c1 + LANES] * cos + acc[:, c1 + LANES:c1 + 2 * LANES] * sin
    c2 = c1 + 2 * LANES
    nq = RET_HEADS * RET_QK_DIM
    nv = RET_HEADS * RET_V_DIM
    rq_ref[0] = acc[:, c2:c2 + nq].astype(BF16)
    rk_ref[0] = acc[:, c2 + nq:c2 + 2 * nq].astype(BF16)
    rv_ref[0] = acc[:, c2 + 2 * nq:c2 + 2 * nq + nv].astype(BF16)
    rg_ref[0] = acc[:, c2 + 2 * nq + nv:]

    nqk = MLA_HEADS * LANES
    qq = _dot((_rms(cq) * qn_ref[...]).astype(BF16), wq_ref[...])
    q = qq[:, :nqk] * _tile_lanes(cos, MLA_HEADS) + qq[:, nqk:] * _tile_lanes(sin, MLA_HEADS)
    q_ref[0] = (q * q_scale).astype(BF16)
    kv = _dot((_rms(ckv) * kvn_ref[...]).astype(BF16), wkv_ref[...])
    k_ref[0] = (kv[:, :nqk] + _tile_lanes(kr, MLA_HEADS)).astype(BF16)
    v_ref[0] = kv[:, nqk:].astype(BF16)


def _proj_odd(xa, mods, gain, w, q_norm, kv_norm, wq, wkv, cos, sin, n_lat):
    b, t, d = xa.shape
    nb = mods.shape[0] - 1
    tm = _row_tile(t)
    tok = lambda i, j: (i, j, 0)
    const = lambda i, j: (0, 0)
    widths = (MLA_HEADS * LANES, MLA_HEADS * LANES, MLA_HEADS * MLA_V_DIM,
              RET_HEADS * RET_QK_DIM, RET_HEADS * RET_QK_DIM, RET_HEADS * RET_V_DIM, RET_HEADS * RET_V_DIM)
    dtypes = (BF16,) * 6 + (F32,)
    kern = functools.partial(_proj_odd_kernel, q_scale=float((MLA_NOPE_DIM + MLA_ROPE_DIM) ** -0.5) * LOG2E,
                             n_lat=n_lat)
    return pl.pallas_call(
        kern,
        grid=(b, t // tm),
        in_specs=[pl.BlockSpec((1, tm, d), tok),
                  pl.BlockSpec((1, 6, d), lambda i, j: (i, 0, 0)),
                  pl.BlockSpec((1, 6, d), lambda i, j: (nb, 0, 0)),
                  pl.BlockSpec((1, d), const),
                  pl.BlockSpec(w.shape, const),
                  pl.BlockSpec(q_norm.shape, const),
                  pl.BlockSpec(kv_norm.shape, const),
                  pl.BlockSpec(wq.shape, const),
                  pl.BlockSpec(wkv.shape, const),
                  pl.BlockSpec((tm, LANES), lambda i, j: (j, 0)),
                  pl.BlockSpec((tm, LANES), lambda i, j: (j, 0))],
        out_specs=[pl.BlockSpec((1, tm, wd), tok) for wd in widths],
        out_shape=[jax.ShapeDtypeStruct((b, t, wd), dt) for wd, dt in zip(widths, dtypes)],
        compiler_params=_cparams("parallel", "arbitrary"),
    )(xa, mods, mods, gain, w, q_norm, kv_norm, wq, wkv, cos, sin)


def _mla_kernel(q_ref, k_ref, v_ref, o_ref, vx_ref):
    @pl.when(pl.program_id(2) == 0)
    def _():
        _fill_v_ones(vx_ref, v_ref)

    o0, o1 = _two_stream_attention(q_ref[0, :, :LANES], [(k_ref[0, :, :LANES], None)],
                                   q_ref[0, :, LANES:], [(k_ref[0, :, LANES:], None)], [vx_ref[...]])
    lane = lax.broadcasted_iota(jnp.int32, (1, LANES), 1)
    o_ref[0] = jnp.where(lane < MLA_V_DIM, o0, o1).astype(BF16)


def _mla_attn(q, k, v, n_lat, tq):
    b, t, _ = k.shape
    npairs = MLA_HEADS // 2
    return pl.pallas_call(
        _mla_kernel,
        grid=(b, npairs, n_lat // tq),
        in_specs=[pl.BlockSpec((1, tq, 2 * LANES), lambda i, h, j: (i, j, h)),
                  pl.BlockSpec((1, t, 2 * LANES), lambda i, h, j: (i, 0, h)),
                  pl.BlockSpec((1, t, LANES), lambda i, h, j: (i, 0, h))],
        out_specs=pl.BlockSpec((1, tq, LANES), lambda i, h, j: (i, j, h)),
        out_shape=jax.ShapeDtypeStruct((b, n_lat, npairs * LANES), BF16),
        scratch_shapes=[pltpu.VMEM((t, 2 * LANES), BF16)],
        compiler_params=_cparams("parallel", "parallel", "arbitrary"),
    )(q, k, v)


def _retention_kernel(dl_ref, q_ref, k_ref, v_ref, g_ref, o_ref, f_ref, b_ref, st_ref, *, n_lat):
    c = RET_CHUNK
    t = k_ref.shape[1]
    n_chunks = n_lat // c
    n_ctx = (t - n_lat) // c
    hp = pl.program_id(1)
    lane = lax.broadcasted_iota(jnp.int32, (1, LANES), 1)
    pos_r = lax.broadcasted_iota(jnp.int32, (c, 1), 0).astype(F32)
    ii = lax.broadcasted_iota(jnp.int32, (c, c), 0)
    jj = lax.broadcasted_iota(jnp.int32, (c, c), 1)
    rel = (ii - jj).astype(F32)
    dl = dl_ref[...]
    hsel = lax.broadcasted_iota(jnp.int32, (1, RET_HEADS), 1)

    chains = []
    for hh in range(2):
        hmask = (lane >= hh * RET_QK_DIM) & (lane < (hh + 1) * RET_QK_DIM)
        for direction in range(2):
            logit = jnp.sum(jnp.where(hsel == 2 * hp + hh, dl[direction:direction + 1], 0.0),
                            axis=-1, keepdims=True)
            lg = jnp.minimum(logit, 0.0) - jnp.log(1.0 + jnp.exp(-jnp.abs(logit)))
            if direction == 0:
                mask = rel >= 0
                d_in = jnp.where(mask, jnp.exp(lg * jnp.where(mask, rel, 0.0)), 0.0)
                d_q = jnp.exp(lg * (pos_r + 1.0))
                d_k = jnp.exp(lg * (c - 1.0 - pos_r))
            else:
                mask = rel < 0
                d_in = jnp.where(mask, jnp.exp(lg * jnp.where(mask, -rel, 0.0)), 0.0)
                d_q = jnp.exp(lg * (c - pos_r))
                d_k = jnp.exp(lg * pos_r)
            chains.append(dict(idx=2 * hh + direction, v0=hh * RET_V_DIM, direction=direction, hmask=hmask,
                               d_in=d_in, d_q=d_q, d_k=d_k, d_chunk=jnp.exp(lg * c)))

    st_ref[...] = jnp.zeros(st_ref.shape, F32)

    def kv_update(ch, start):
        kb = k_ref[0, pl.ds(start, c), :].astype(F32)
        vb = v_ref[0, pl.ds(start, c), ch["v0"]:ch["v0"] + RET_V_DIM]
        kd = jnp.where(ch["hmask"], kb * ch["d_k"], 0.0).astype(BF16)
        st_ref[ch["idx"]] = ch["d_chunk"] * st_ref[ch["idx"]] + _dot_tn(kd, vb)

    for i in range(n_ctx):
        for ch in chains:
            ci = i if ch["direction"] == 0 else n_ctx - 1 - i
            kv_update(ch, n_lat + ci * c)

    def lat_step(i, carry):
        starts = [pl.multiple_of((i if ch["direction"] == 0 else n_chunks - 1 - i) * c, c) for ch in chains]
        qms, avs, crs = [], [], []
        for ch, start in zip(chains, starts):
            qb = q_ref[0, pl.ds(start, c), :]
            qm = jnp.where(ch["hmask"], qb, jnp.zeros_like(qb))
            qms.append(qm)
            avs.append(_dot_nt(qm, k_ref[0, pl.ds(start, c), :]))
        for ch, qm in zip(chains, qms):
            qd = (qm.astype(F32) * ch["d_q"]).astype(BF16)
            crs.append(_dot(qd, st_ref[ch["idx"]].astype(BF16)))
        for ch, start, a, cross in zip(chains, starts, avs, crs):
            vb = v_ref[0, pl.ds(start, c), ch["v0"]:ch["v0"] + RET_V_DIM]
            res = _dot((a * ch["d_in"]).astype(BF16), vb) + cross
            dst = f_ref if ch["direction"] == 0 else b_ref
            dst[pl.ds(start, c), ch["v0"]:ch["v0"] + RET_V_DIM] = res
        for ch, start in zip(chains, starts):
            kv_update(ch, start)
        return carry

    lax.fori_loop(0, n_chunks, lat_step, 0)

    def finish(i, carry):
        start = pl.multiple_of(i * c, c)
        for hh in range(2):
            v0 = hh * RET_V_DIM
            r = _rms(f_ref[pl.ds(start, c), v0:v0 + RET_V_DIM] + b_ref[pl.ds(start, c), v0:v0 + RET_V_DIM])
            gate = _silu(g_ref[0, pl.ds(start, c), v0:v0 + RET_V_DIM])
            o_ref[0, pl.ds(start, c), v0:v0 + RET_V_DIM] = (r * gate).astype(BF16)
        return carry

    lax.fori_loop(0, n_chunks, finish, 0)


def _retention(rq, rk, rv, rg, decay_logit, n_lat):
    b, t, _ = rq.shape
    npairs = RET_HEADS // 2
    kern = functools.partial(_retention_kernel, n_lat=n_lat)
    return pl.pallas_call(
        kern,
        grid=(b, npairs),
        in_specs=[pl.BlockSpec(decay_logit.shape, lambda i, h: (0, 0)),
                  pl.BlockSpec((1, n_lat, LANES), lambda i, h: (i, 0, h)),
                  pl.BlockSpec((1, t, LANES), lambda i, h: (i, 0, h)),
                  pl.BlockSpec((1, t, 2 * RET_V_DIM), lambda i, h: (i, 0, h)),
                  pl.BlockSpec((1, n_lat, 2 * RET_V_DIM), lambda i, h: (i, 0, h))],
        out_specs=pl.BlockSpec((1, n_lat, 2 * RET_V_DIM), lambda i, h: (i, 0, h)),
        out_shape=jax.ShapeDtypeStruct((b, n_lat, RET_HEADS * RET_V_DIM), BF16),
        scratch_shapes=[pltpu.VMEM((n_lat, 2 * RET_V_DIM), F32), pltpu.VMEM((n_lat, 2 * RET_V_DIM), F32),
                        pltpu.VMEM((4, LANES, RET_V_DIM), F32)],
        compiler_params=_cparams("parallel", "arbitrary"),
    )(decay_logit, rq, rk, rv, rg)


def _rope_tables(n_lat, n_ctx, rot_dim, lane0):
    t = np.arange(n_lat)
    rows = (t // GRID_W).astype(np.float32)
    cols = (t % GRID_W).astype(np.float32)
    m = rot_dim // 4
    freqs = jnp.asarray(ROPE_BASE, F32) ** (-jnp.arange(m, dtype=F32) / m)
    ang_r = jnp.asarray(rows)[:, None] * freqs
    ang_c = jnp.asarray(cols)[:, None] * freqs
    cos = jnp.concatenate([jnp.cos(ang_r)] * 2 + [jnp.cos(ang_c)] * 2, axis=-1)
    sin = jnp.concatenate([-jnp.sin(ang_r), jnp.sin(ang_r), -jnp.sin(ang_c), jnp.sin(ang_c)], axis=-1)
    reps = (LANES - lane0) // rot_dim if lane0 == 0 else 1
    cos = jnp.concatenate([jnp.ones((n_lat, lane0), F32)] + [cos] * reps
                          + [jnp.ones((n_lat, LANES - lane0 - reps * rot_dim), F32)], axis=-1)
    sin = jnp.concatenate([jnp.zeros((n_lat, lane0), F32)] + [sin] * reps
                          + [jnp.zeros((n_lat, LANES - lane0 - reps * rot_dim), F32)], axis=-1)
    cos = jnp.concatenate([cos, jnp.ones((n_ctx, LANES), F32)], axis=0)
    sin = jnp.concatenate([sin, jnp.zeros((n_ctx, LANES), F32)], axis=0)
    return cos, sin


def _rot_partner(w, rot_dim):
    k, n = w.shape
    q = rot_dim // 4
    return w.reshape(k, n // rot_dim, 2, 2, q)[:, :, :, ::-1, :].reshape(k, n)


def _pad_cols(w, lane0, width=LANES):
    k, n = w.shape
    return jnp.concatenate([jnp.zeros((k, lane0), w.dtype), w, jnp.zeros((k, width - lane0 - n), w.dtype)], axis=-1)


def _even_weights(w_in):
    n_rope = 2 * DIFF_HEADS * 2 * DIFF_HEAD_DIM
    dw = DIFF_HEADS * 2 * DIFF_HEAD_DIM
    nw = NA_HEADS * NA_HEAD_DIM
    scale = jnp.concatenate([jnp.full((dw,), DIFF_HEAD_DIM ** -0.5, F32), jnp.ones((2 * dw,), F32),
                             jnp.full((nw,), NA_HEAD_DIM ** -0.5, F32), jnp.ones((2 * nw,), F32)])
    w = w_in * scale
    return jnp.concatenate([w, _rot_partner(w[:, :n_rope], DIFF_HEAD_DIM)], axis=-1).astype(BF16), n_rope


def _odd_weights(w_in, w_uq, w_ukv):
    c0 = MLA_Q_RANK
    c1 = c0 + MLA_KV_RANK
    c2 = c1 + MLA_ROPE_DIM
    nq = RET_HEADS * RET_QK_DIM
    kr = w_in[:, c1:c2]
    w = jnp.concatenate([w_in[:, :c1],
                         _pad_cols(kr, MLA_NOPE_DIM), _pad_cols(_rot_partner(kr, MLA_ROPE_DIM), MLA_NOPE_DIM),
                         w_in[:, c2:c2 + nq], w_in[:, c2 + nq:c2 + 2 * nq] * (RET_QK_DIM ** -0.5),
                         w_in[:, c2 + 2 * nq:]], axis=-1).astype(BF16)
    r = w_uq.shape[0]
    uq = w_uq.reshape(r, MLA_HEADS, MLA_NOPE_DIM + MLA_ROPE_DIM)
    pad = jnp.zeros((r, MLA_HEADS, LANES - MLA_NOPE_DIM - MLA_ROPE_DIM), F32)
    uq_rot = _rot_partner(uq[:, :, MLA_NOPE_DIM:].reshape(r, -1), MLA_ROPE_DIM).reshape(r, MLA_HEADS, MLA_ROPE_DIM)
    wq = jnp.concatenate([uq, pad], axis=-1).reshape(r, -1)
    wq_rot = jnp.concatenate([jnp.zeros_like(uq[:, :, :MLA_NOPE_DIM]), uq_rot, pad], axis=-1).reshape(r, -1)
    rk = w_ukv.shape[0]
    ukv = w_ukv.reshape(rk, MLA_HEADS, MLA_NOPE_DIM + MLA_V_DIM)
    wk = jnp.concatenate([ukv[:, :, :MLA_NOPE_DIM], jnp.zeros((rk, MLA_HEADS, LANES - MLA_NOPE_DIM), F32)],
                         axis=-1).reshape(rk, -1)
    wv = ukv[:, :, MLA_NOPE_DIM:].reshape(rk, -1)
    return w, jnp.concatenate([wq, wq_rot], axis=-1).astype(BF16), jnp.concatenate([wk, wv], axis=-1).astype(BF16)


def _na_bias(rpb, n_lat):
    pat_of_group, idx_r, idx_c, valid = _na_patterns(n_lat // GRID_W)
    g = NA_GROUP_ROWS
    span = g + NA_WIN_ROWS - 1
    n_pat = idx_r.shape[0]
    col_sel = (idx_c[0, :GRID_W, :GRID_W, None] == np.arange(2 * NA_WIN_COLS - 1)).astype(np.float32)
    row_idx = idx_r.reshape(n_pat, g, GRID_W, span, GRID_W)[:, :, 0, :, 0]
    row_sel = (row_idx[..., None] == np.arange(2 * NA_WIN_ROWS - 1)).astype(np.float32)
    cols = jnp.einsum('hrc,qkc->hrqk', rpb, col_sel, precision=lax.Precision.HIGHEST)
    bias = jnp.einsum('pijr,hrqk->hpiqjk', row_sel, cols, precision=lax.Precision.HIGHEST)
    bias = bias.reshape(rpb.shape[0], n_pat, g * GRID_W, span * GRID_W)
    return jnp.where(valid[None], bias * LOG2E, NEG_BIG), pat_of_group


def _router_weights(w_router):
    return _pad_cols(w_router, 0)


def kernel(x, c, ctx, c_ctx, ada_w, ada_b, norm_mix, norm_ffn, final_norm, even_w_in, even_w_out,
           diff_lambda, diff_subln, na_rpb, odd_w_in, odd_w_out, mla_q_norm, mla_w_uq, mla_kv_norm,
           mla_w_ukv, ret_decay_logit, moe_router, moe_w1, moe_w3, moe_w2):
    b, n_lat, d = x.shape
    n_ctx = ctx.shape[1]
    depth = ada_w.shape[0]
    n_lat_tiles = n_lat // TOK_TILE

    cond = jnp.concatenate([c, c_ctx[None], jnp.zeros((7, d), F32)], axis=0)
    mods_all = _mods(cond, ada_w, ada_b)[:, :b + 1].reshape(depth, b + 1, 6, d)
    xa = jnp.concatenate([x, ctx], axis=1)

    for l in range(depth):
        mods = mods_all[l]
        need_ctx = l < depth - 1
        i = l // 2
        gain_mix = norm_mix[l][None]
        if l % 2 == 0:
            w, n_rope = _even_weights(even_w_in[i])
            cos, sin = _rope_tables(n_lat, n_ctx, DIFF_HEAD_DIM, 0)
            qkv = _proj_even(xa, mods, gain_mix, w, cos, sin, n_lat, n_rope, even_w_in.shape[-1])
            lam_init = 0.8 - 0.6 * math.exp(-0.3 * l)
            mix_a = _diff_attn(qkv, diff_lambda[i], diff_subln[i][None], lam_init, 0, n_lat, ATTN_Q_TILE,
                               0, n_lat + n_ctx)
            mix_a_ctx = _diff_attn(qkv, diff_lambda[i], diff_subln[i][None], lam_init, n_lat, n_ctx, n_ctx,
                                   n_lat, n_ctx)
            bias, offsets = _na_bias(na_rpb[i], n_lat)
            mix_b = _na_attn(qkv, bias, n_lat, offsets, 3 * DIFF_HEADS * 2 * DIFF_HEAD_DIM)
            w_out = even_w_out[i].astype(BF16)
        else:
            w, wq, wkv = _odd_weights(odd_w_in[i], mla_w_uq[i], mla_w_ukv[i])
            cos, sin = _rope_tables(n_lat, n_ctx, MLA_ROPE_DIM, MLA_NOPE_DIM)
            q, k, v, rq, rk, rv, rg = _proj_odd(xa, mods, gain_mix, w, mla_q_norm[i][None], mla_kv_norm[i][None],
                                                wq, wkv, cos, sin, n_lat)
            mix_a = _mla_attn(q, k, v, n_lat, ATTN_Q_TILE)
            mix_a_ctx = mix_a
            mix_b = _retention(rq, rk, rv, rg, ret_decay_logit[i], n_lat)
            w_out = odd_w_out[i].astype(BF16)

        tiles = (n_lat + n_ctx) // TOK_TILE if need_ctx else n_lat_tiles
        x_mid, h2, aff = _proj_out(xa, mix_a, mix_a_ctx, mix_b, w_out, mods, norm_ffn[l][None],
                                   _router_weights(moe_router[l]), tiles, n_lat_tiles)
        segments = ((0, n_lat), (n_lat, n_ctx)) if need_ctx else ((0, n_lat),)
        xa = _moe(x_mid, h2, aff, segments, mods, final_norm[None], n_lat_tiles, l == depth - 1,
                  moe_w1, moe_w3, moe_w2, l)
    return xa
```

```python
import functools
import math

import jax
import jax.numpy as jnp
import numpy as np
from jax import lax
from jax.experimental import pallas as pl
from jax.experimental.pallas import tpu as pltpu

GRID_W = 64
ROPE_BASE = 10000.0
RMS_EPS = 1e-6
DIFF_HEADS = 4
DIFF_HEAD_DIM = 64
NA_HEADS = 8
NA_HEAD_DIM = 64
NA_WIN_ROWS = 8
NA_WIN_COLS = 16
MLA_HEADS = 8
MLA_Q_RANK = 256
MLA_KV_RANK = 128
MLA_NOPE_DIM = 64
MLA_ROPE_DIM = 32
MLA_V_DIM = 64
RET_HEADS = 4
RET_QK_DIM = 64
RET_V_DIM = 128
RET_CHUNK = 128
N_EXPERTS = 16
EC_CAPACITY_FACTOR = 2

LANES = 128
VMEM_LIMIT = 56 * 1024 * 1024
NEG_BIG = -1e30
TOK_TILE = 256
ROW_TILES = 8
ATTN_Q_TILE = 512
FFN_ROW_TILES = 8
FFN_HIDDEN_BLOCK = 512
NA_GROUP_ROWS = 4
NA_GROUPS_PER_STEP = 2
PAIR_CHUNK = 256
LOG2E = math.log2(math.e)

BF16 = jnp.bfloat16
F32 = jnp.float32


def _cparams(*sem):
    return pltpu.CompilerParams(dimension_semantics=sem, vmem_limit_bytes=VMEM_LIMIT)


def _dot(a, b):
    return jnp.dot(a, b, preferred_element_type=F32)


def _dot_nt(a, b):
    return lax.dot_general(a, b, (((1,), (1,)), ((), ())), preferred_element_type=F32)


def _dot_tn(a, b):
    return lax.dot_general(a, b, (((0,), (0,)), ((), ())), preferred_element_type=F32)


def _rms(x):
    return x * lax.rsqrt(jnp.mean(x * x, axis=-1, keepdims=True) + RMS_EPS)


def _silu(x):
    return x * (1.0 / (1.0 + jnp.exp(-x)))


def _tile_lanes(t, n):
    return jnp.concatenate([t] * n, axis=-1)


def _mods_kernel(c_ref, w_ref, b_ref, o_ref):
    o_ref[0] = _dot(_silu(c_ref[...]), w_ref[0]) + b_ref[0]


def _mods(cond, ada_w, ada_b):
    depth, d, n = ada_w.shape
    r = cond.shape[0]
    tn = 1536
    return pl.pallas_call(
        _mods_kernel,
        grid=(depth, n // tn),
        in_specs=[pl.BlockSpec((r, d), lambda l, j: (0, 0)),
                  pl.BlockSpec((1, d, tn), lambda l, j: (l, 0, j)),
                  pl.BlockSpec((1, 1, tn), lambda l, j: (l, 0, j))],
        out_specs=pl.BlockSpec((1, r, tn), lambda l, j: (l, 0, j)),
        out_shape=jax.ShapeDtypeStruct((depth, r, n), F32),
        compiler_params=_cparams("arbitrary", "arbitrary"),
    )(cond, ada_w, ada_b.reshape(depth, 1, n))


def _normed(x, gain, mods, shift_row, scale_row):
    return _rms(x) * gain * (1.0 + mods[scale_row:scale_row + 1]) + mods[shift_row:shift_row + 1]


def _mixer_input(x, gain, mods_lat, mods_ctx, n_lat):
    tm = x.shape[0]
    rows = pl.program_id(1) * tm + lax.broadcasted_iota(jnp.int32, (tm, 1), 0)
    is_ctx = rows >= n_lat
    scale = jnp.where(is_ctx, mods_ctx[1:2], mods_lat[1:2])
    shift = jnp.where(is_ctx, mods_ctx[0:1], mods_lat[0:1])
    return _rms(x) * gain * (1.0 + scale) + shift


def _proj_even_kernel(x_ref, ml_ref, mc_ref, g_ref, w_ref, cos_ref, sin_ref, o_ref, *, n_rope, n_out, n_lat):
    na_q0 = n_rope + n_rope // 2
    na_q1 = na_q0 + NA_HEADS * NA_HEAD_DIM
    h = _mixer_input(x_ref[0], g_ref[...], ml_ref[0], mc_ref[0], n_lat)
    acc = _dot(h.astype(BF16), w_ref[...])
    reps = n_rope // LANES
    cos = _tile_lanes(cos_ref[...], reps)
    sin = _tile_lanes(sin_ref[...], reps)
    roped = acc[:, :n_rope] * cos + acc[:, n_out:] * sin
    n_q = n_rope // 2
    o_ref[0, :, :n_q] = (roped[:, :n_q] * LOG2E).astype(BF16)
    o_ref[0, :, n_q:n_rope] = roped[:, n_q:].astype(BF16)
    o_ref[0, :, n_rope:na_q0] = acc[:, n_rope:na_q0].astype(BF16)
    o_ref[0, :, na_q0:na_q1] = (acc[:, na_q0:na_q1] * LOG2E).astype(BF16)
    o_ref[0, :, na_q1:] = acc[:, na_q1:n_out].astype(BF16)


def _row_tile(t):
    tm = t // ROW_TILES
    assert tm * ROW_TILES == t and tm % 16 == 0
    return tm


def _proj_even(xa, mods, gain, w, cos, sin, n_lat, n_rope, n_out):
    b, t, d = xa.shape
    nb = mods.shape[0] - 1
    tm = _row_tile(t)
    kern = functools.partial(_proj_even_kernel, n_rope=n_rope, n_out=n_out, n_lat=n_lat)
    return pl.pallas_call(
        kern,
        grid=(b, t // tm),
        in_specs=[pl.BlockSpec((1, tm, d), lambda i, j: (i, j, 0)),
                  pl.BlockSpec((1, 6, d), lambda i, j: (i, 0, 0)),
                  pl.BlockSpec((1, 6, d), lambda i, j: (nb, 0, 0)),
                  pl.BlockSpec((1, d), lambda i, j: (0, 0)),
                  pl.BlockSpec(w.shape, lambda i, j: (0, 0)),
                  pl.BlockSpec((tm, LANES), lambda i, j: (j, 0)),
                  pl.BlockSpec((tm, LANES), lambda i, j: (j, 0))],
        out_specs=pl.BlockSpec((1, tm, n_out), lambda i, j: (i, j, 0)),
        out_shape=jax.ShapeDtypeStruct((b, t, n_out), BF16),
        compiler_params=_cparams("parallel", "arbitrary"),
    )(xa, mods, mods, gain, w, cos, sin)


def _fill_v_ones(vx_ref, v_ref):
    vx_ref[:, :LANES] = v_ref[0]
    vx_ref[:, LANES:] = jnp.ones((vx_ref.shape[0], LANES), BF16)


def _weights(s, m):
    return jnp.exp2(s - m).astype(BF16)


def _rowmax(*ss):
    m = jnp.max(ss[0], axis=-1, keepdims=True)
    for s in ss[1:]:
        m = jnp.maximum(m, jnp.max(s, axis=-1, keepdims=True))
    return m


def _normalised(ox):
    return ox[:, :LANES] / ox[:, LANES:LANES + 1]


def _scores(q, segs):
    return [_dot_nt(q, k) if bias is None else _dot_nt(q, k) + bias for k, bias in segs]


def _pv(ps, vxs):
    o = _dot(ps[0], vxs[0])
    for p, vx in zip(ps[1:], vxs[1:]):
        o = o + _dot(p, vx)
    return o


def _multi_stream_attention(streams):
    n = len(streams)
    s, m, p, o = [None] * n, [None] * n, [None] * n, [None] * n
    for t in range(n + 2):
        if t < n:
            s[t] = _scores(streams[t][0], streams[t][1])
        if 0 <= t - 2 < n:
            o[t - 2] = _normalised(_pv(p[t - 2], streams[t - 2][2]))
        if 0 <= t - 1 < n:
            p[t - 1] = [_weights(x, m[t - 1]) for x in s[t - 1]]
        if t < n:
            m[t] = _rowmax(*s[t])
    return o


def _two_stream_attention(qa, segs_a, qb, segs_b, vxs):
    return _multi_stream_attention([(qa, segs_a, vxs), (qb, segs_b, vxs)])


def _diff_attn_kernel(lam_ref, g_ref, q_ref, k_ref, v_ref, o_ref, vx_ref, *, lam_init):
    @pl.when(pl.program_id(2) == 0)
    def _():
        _fill_v_ones(vx_ref, v_ref)

    lp = lam_ref[...]
    s1 = jnp.sum(lp[0:1] * lp[1:2], axis=-1, keepdims=True)
    s2 = jnp.sum(lp[2:3] * lp[3:4], axis=-1, keepdims=True)
    lam = jnp.exp(s1) - jnp.exp(s2) + lam_init
    q = q_ref[0]
    lane = lax.broadcasted_iota(jnp.int32, (1, LANES), 1)
    q1 = jnp.where(lane < DIFF_HEAD_DIM, q, jnp.zeros_like(q))
    q2 = jnp.where(lane >= DIFF_HEAD_DIM, q, jnp.zeros_like(q))

    k = k_ref[0]
    o1, o2 = _two_stream_attention(q1, [(k, None)], q2, [(k, None)], [vx_ref[...]])
    o = o1 - lam * o2
    o_ref[0] = (_rms(o) * g_ref[...] * (1.0 - lam_init)).astype(BF16)


def _diff_attn(qkv, lam_params, subln, lam_init, q_row0, n_q, tq, k_row0, n_k):
    b = qkv.shape[0]
    nh = DIFF_HEADS
    q0, k0 = q_row0 // tq, k_row0 // n_k
    assert q0 * tq == q_row0 and k0 * n_k == k_row0 and n_q % tq == 0
    return pl.pallas_call(
        functools.partial(_diff_attn_kernel, lam_init=lam_init),
        grid=(b, nh, n_q // tq),
        in_specs=[pl.BlockSpec(lam_params.shape, lambda i, h, j: (0, 0)),
                  pl.BlockSpec((1, LANES), lambda i, h, j: (0, 0)),
                  pl.BlockSpec((1, tq, LANES), lambda i, h, j: (i, q0 + j, h)),
                  pl.BlockSpec((1, n_k, LANES), lambda i, h, j: (i, k0, nh + h)),
                  pl.BlockSpec((1, n_k, LANES), lambda i, h, j: (i, k0, 2 * nh + h))],
        out_specs=pl.BlockSpec((1, tq, LANES), lambda i, h, j: (i, j, h)),
        out_shape=jax.ShapeDtypeStruct((b, n_q, nh * LANES), BF16),
        scratch_shapes=[pltpu.VMEM((n_k, 2 * LANES), BF16)],
        compiler_params=_cparams("parallel", "parallel", "arbitrary"),
    )(lam_params, subln, qkv, qkv, qkv)


def _na_patterns(rows):
    g = NA_GROUP_ROWS
    span = g + NA_WIN_ROWS - 1
    nq, nk = g * GRID_W, span * GRID_W
    qi = np.arange(nq)[:, None]
    ki = np.arange(nk)[None, :]
    qcol, kcol = qi % GRID_W, ki % GRID_W
    cs = np.clip(qcol - NA_WIN_COLS // 2, 0, GRID_W - NA_WIN_COLS)
    col_ok = (kcol >= cs) & (kcol < cs + NA_WIN_COLS)
    idx_c = np.clip(kcol - qcol + NA_WIN_COLS - 1, 0, 2 * NA_WIN_COLS - 2) + 0 * qi
    seen, pat_of_group, idx_r, valid = {}, [], [], []
    for r0 in range(0, rows, g):
        ks = min(max(r0 - NA_WIN_ROWS // 2, 0), rows - span)
        qrow = r0 + qi // GRID_W
        krow = ks + ki // GRID_W
        rs = np.clip(qrow - NA_WIN_ROWS // 2, 0, rows - NA_WIN_ROWS)
        ok = (krow >= rs) & (krow < rs + NA_WIN_ROWS) & col_ok
        ir = np.clip(krow - qrow + NA_WIN_ROWS - 1, 0, 2 * NA_WIN_ROWS - 2) + 0 * kcol
        key = (ok.tobytes(), np.where(ok, ir, 0).tobytes())
        if key not in seen:
            seen[key] = len(idx_r)
            idx_r.append(ir)
            valid.append(ok)
        pat_of_group.append(seen[key])
    n_pat = len(idx_r)
    return tuple(pat_of_group), np.stack(idx_r), np.stack([idx_c] * n_pat), np.stack(valid)


def _na_kernel(q_ref, k_ref, v_ref, bias_ref, o_ref, vx_ref, *, n_lat, pat_of_group):
    g_rows = NA_GROUP_ROWS
    span = g_rows + NA_WIN_ROWS - 1
    rows = n_lat // GRID_W
    nq, nk = g_rows * GRID_W, span * GRID_W
    lane = lax.broadcasted_iota(jnp.int32, (1, LANES), 1)
    head_masks = (lane < NA_HEAD_DIM, lane >= NA_HEAD_DIM)
    _fill_v_ones(vx_ref, v_ref)
    kc = k_ref[0, n_lat:, :]
    vxc = vx_ref[n_lat:, :]

    def both_heads(blocks):
        streams = []
        for q, segs, vxs in blocks:
            for hh, hm in enumerate(head_masks):
                streams.append((jnp.where(hm, q, jnp.zeros_like(q)),
                                [(k, None if bi is None else bias_ref[hh, bi]) for k, bi in segs], vxs))
        outs = _multi_stream_attention(streams)
        return [jnp.where(head_masks[0], outs[2 * i], outs[2 * i + 1]).astype(BF16) for i in range(len(blocks))]

    def group_block(g):
        r0 = g * g_rows
        ks = jnp.clip(r0 - NA_WIN_ROWS // 2, 0, rows - span)
        common = max(set(pat_of_group), key=pat_of_group.count)
        pat = jnp.int32(common)
        for gi, p in enumerate(pat_of_group):
            if p != common:
                pat = jnp.where(g == gi, p, pat)
        qg = q_ref[0, pl.ds(pl.multiple_of(g * nq, nq), nq), :]
        kw = k_ref[0, pl.ds(pl.multiple_of(ks * GRID_W, GRID_W), nk), :]
        vxw = vx_ref[pl.ds(pl.multiple_of(ks * GRID_W, GRID_W), nk), :]
        return qg, [(kw, pat), (kc, None)], [vxw, vxc]

    def group_step(i, carry):
        gs = [i * NA_GROUPS_PER_STEP + u for u in range(NA_GROUPS_PER_STEP)]
        for g, out in zip(gs, both_heads([group_block(g) for g in gs])):
            o_ref[0, pl.ds(pl.multiple_of(g * nq, nq), nq), :] = out
        return carry

    lax.fori_loop(0, rows // g_rows // NA_GROUPS_PER_STEP, group_step, 0)

    o_ref[0, n_lat:, :] = both_heads([(q_ref[0, n_lat:, :], [(kc, None)], [vxc])])[0]


def _na_attn(qkv, bias, n_lat, pat_of_group, col0):
    b, t, _ = qkv.shape
    npairs = NA_HEADS // 2
    cb = col0 // LANES
    kern = functools.partial(_na_kernel, n_lat=n_lat, pat_of_group=pat_of_group)
    return pl.pallas_call(
        kern,
        grid=(b, npairs),
        in_specs=[pl.BlockSpec((1, t, LANES), lambda i, h: (i, 0, cb + h)),
                  pl.BlockSpec((1, t, LANES), lambda i, h: (i, 0, cb + npairs + h)),
                  pl.BlockSpec((1, t, LANES), lambda i, h: (i, 0, cb + 2 * npairs + h)),
                  pl.BlockSpec((2,) + bias.shape[1:], lambda i, h: (h, 0, 0, 0))],
        out_specs=pl.BlockSpec((1, t, LANES), lambda i, h: (i, 0, h)),
        out_shape=jax.ShapeDtypeStruct((b, t, npairs * LANES), BF16),
        scratch_shapes=[pltpu.VMEM((t, 2 * LANES), BF16)],
        compiler_params=_cparams("parallel", "arbitrary"),
    )(qkv, qkv, qkv, bias)


def _proj_out_kernel(x_ref, a_ref, ac_ref, b_ref, wa_ref, wb_ref, m_ref, g_ref, wr_ref, xo_ref, h_ref, aff_ref,
                     *, n_lat_tiles):
    mods = m_ref[0]
    a = jnp.where(pl.program_id(1) >= n_lat_tiles, ac_ref[0], a_ref[0])
    y = _dot(a, wa_ref[...]) + _dot(b_ref[0], wb_ref[...])
    x = x_ref[0] + mods[2:3] * y
    xo_ref[0] = x
    h2 = _normed(x, g_ref[...], mods, 3, 4)
    h_hi = h2.astype(BF16)
    h_ref[0] = h_hi
    h_lo = (h2 - h_hi.astype(F32)).astype(BF16)
    wr = wr_ref[...]
    w_hi = wr.astype(BF16)
    w_lo = (wr - w_hi.astype(F32)).astype(BF16)
    hh = _dot(h_hi, jnp.concatenate([w_hi, w_lo], axis=-1))
    logits = hh[:, :LANES] + (hh[:, LANES:] + _dot(h_lo, w_hi))
    lane = lax.broadcasted_iota(jnp.int32, logits.shape, 1)
    logits = jnp.where(lane < N_EXPERTS, logits, NEG_BIG)
    e = jnp.exp(logits - jnp.max(logits, axis=-1, keepdims=True))
    aff_ref[0] = e / jnp.sum(e, axis=-1, keepdims=True)


def _proj_out(xa, a, a_ctx, bb, w_out, mods, gain, w_router, n_tiles, n_lat_tiles):
    b, t, d = xa.shape
    nb = mods.shape[0] - 1
    tm = TOK_TILE
    wa, wb = w_out[:a.shape[-1]], w_out[a.shape[-1]:]
    tok = lambda i, j: (i, j, 0)
    const = lambda i, j: (0, 0)
    return pl.pallas_call(
        functools.partial(_proj_out_kernel, n_lat_tiles=n_lat_tiles),
        grid=(b, n_tiles),
        in_specs=[pl.BlockSpec((1, tm, d), tok),
                  pl.BlockSpec((1, tm, a.shape[-1]), lambda i, j: (i, jnp.minimum(j, n_lat_tiles - 1), 0)),
                  pl.BlockSpec((1, tm, a.shape[-1]), lambda i, j: (i, jnp.maximum(j - n_lat_tiles, 0), 0)),
                  pl.BlockSpec((1, tm, bb.shape[-1]), tok),
                  pl.BlockSpec(wa.shape, const),
                  pl.BlockSpec(wb.shape, const),
                  pl.BlockSpec((1, 6, d), lambda i, j: (jnp.where(j >= n_lat_tiles, nb, i), 0, 0)),
                  pl.BlockSpec((1, d), const),
                  pl.BlockSpec(w_router.shape, const)],
        out_specs=[pl.BlockSpec((1, tm, d), tok),
                   pl.BlockSpec((1, tm, d), tok),
                   pl.BlockSpec((1, tm, LANES), tok)],
        out_shape=[jax.ShapeDtypeStruct((b, n_tiles * tm, d), F32),
                   jax.ShapeDtypeStruct((b, n_tiles * tm, d), BF16),
                   jax.ShapeDtypeStruct((b, n_tiles * tm, LANES), F32)],
        compiler_params=_cparams("parallel", "arbitrary"),
    )(xa, a, a_ctx, bb, wa, wb, mods, gain, w_router)


def _ffn_kernel(x_ref, gate_ref, w1_ref, w3_ref, w2_ref, o_ref, w1b, w3b, w2b):
    @pl.when(pl.program_id(1) == 0)
    def _():
        w1b[...] = w1_ref[0].astype(BF16)
        w3b[...] = w3_ref[0].astype(BF16)
        w2b[...] = w2_ref[0].astype(BF16)

    x = x_ref[0]
    f = w1b.shape[1]
    y = None
    for f0 in range(0, f, FFN_HIDDEN_BLOCK):
        f1 = f0 + FFN_HIDDEN_BLOCK
        hid = _silu(_dot(x, w1b[:, f0:f1])) * _dot(x, w3b[:, f0:f1])
        part = _dot(hid.astype(BF16), w2b[f0:f1, :])
        y = part if y is None else y + part
    o_ref[0] = (y * gate_ref[0]).astype(BF16)


def _expert_ffn(xe, gate, w1, w3, w2, layer):
    e, m, d = xe.shape
    f = w1.shape[-1]
    tm = m // FFN_ROW_TILES
    assert tm * FFN_ROW_TILES == m and tm % 16 == 0
    wspec = lambda r, c: pl.BlockSpec((None, 1, r, c), lambda i, j: (layer, i, 0, 0))
    return pl.pallas_call(
        _ffn_kernel,
        grid=(e, m // tm),
        in_specs=[pl.BlockSpec((1, tm, d), lambda i, j: (i, j, 0)),
                  pl.BlockSpec((1, tm, 1), lambda i, j: (i, j, 0)),
                  wspec(d, f), wspec(d, f), wspec(f, d)],
        out_specs=pl.BlockSpec((1, tm, d), lambda i, j: (i, j, 0)),
        out_shape=jax.ShapeDtypeStruct((e, m, d), BF16),
        scratch_shapes=[pltpu.VMEM((d, f), BF16), pltpu.VMEM((d, f), BF16), pltpu.VMEM((f, d), BF16)],
        compiler_params=_cparams("parallel", "arbitrary"),
    )(xe, gate, w1, w3, w2)


def _route(aff, segments):
    b, r, _ = aff.shape
    e = N_EXPERTS
    gates, flats, toks = [], [], []
    for row0, n in segments:
        cap = (EC_CAPACITY_FACTOR * n) // e
        gate, idx = lax.top_k(jnp.swapaxes(aff[:, row0:row0 + n, :e], 1, 2), cap)
        tok = idx + row0
        gates.append(jnp.swapaxes(gate, 0, 1).reshape(e, b * cap))
        flats.append(jnp.swapaxes(tok + jnp.arange(b, dtype=idx.dtype)[:, None, None] * r, 0, 1).reshape(e, b * cap))
        toks.append(tok)
    return jnp.concatenate(gates, axis=1)[..., None], jnp.concatenate(flats, axis=1), toks


def _combine_kernel(lo_ref, x_ref, tok_ref, y_ref, m_ref, g_ref, o_ref, *, final):
    i, j = pl.program_id(0), pl.program_id(1)
    tm = x_ref.shape[1]
    n_tiles = pl.num_programs(1)
    lo = lo_ref[i * (n_tiles + 1) + j]
    hi = lo_ref[i * (n_tiles + 1) + j + 1]
    rows = j * tm + lax.broadcasted_iota(jnp.int32, (tm, 1), 0)
    o_ref[0] = jnp.zeros(o_ref.shape[1:], F32)

    def chunk(c, carry):
        sel = (tok_ref[0, pl.ds(c, 1), :] == rows).astype(BF16)
        o_ref[0] += _dot(sel, y_ref[0, pl.ds(pl.multiple_of(c * PAIR_CHUNK, PAIR_CHUNK), PAIR_CHUNK), :])
        return carry

    lax.fori_loop(lo // PAIR_CHUNK, (hi + PAIR_CHUNK - 1) // PAIR_CHUNK, chunk, 0)
    x = x_ref[0] + m_ref[0][5:6] * o_ref[0]
    if final:
        x = _rms(x) * g_ref[...]
    o_ref[0] = x


def _combine(x, tok_sorted, y_sorted, lo, mods, gain, n_lat_tiles, final):
    b, t, d = x.shape
    p = tok_sorted.shape[1]
    nb = mods.shape[0] - 1
    tm = TOK_TILE
    tok = lambda i, j, lo_ref: (i, j, 0)
    whole = lambda i, j, lo_ref: (i, 0, 0)
    grid_spec = pltpu.PrefetchScalarGridSpec(
        num_scalar_prefetch=1,
        grid=(b, t // tm),
        in_specs=[pl.BlockSpec((1, tm, d), tok),
                  pl.BlockSpec((1, p // PAIR_CHUNK, PAIR_CHUNK), whole),
                  pl.BlockSpec((1, p, d), whole),
                  pl.BlockSpec((1, 6, d), lambda i, j, lo_ref: (jnp.where(j >= n_lat_tiles, nb, i), 0, 0)),
                  pl.BlockSpec((1, d), lambda i, j, lo_ref: (0, 0))],
        out_specs=pl.BlockSpec((1, tm, d), tok))
    return pl.pallas_call(
        functools.partial(_combine_kernel, final=final),
        grid_spec=grid_spec,
        out_shape=jax.ShapeDtypeStruct((b, t, d), F32),
        compiler_params=_cparams("parallel", "arbitrary"),
    )(lo, x, tok_sorted.reshape(b, p // PAIR_CHUNK, PAIR_CHUNK), y_sorted, mods, gain)


def _moe(x_mid, h2, aff, segments, mods, gain, n_lat_tiles, final, w1, w3, w2, layer):
    b, r, d = h2.shape
    gate, flat, toks = _route(aff, segments)
    e, m = flat.shape
    xe = h2.reshape(b * r, d).at[flat].get(mode="promise_in_bounds")
    ye = _expert_ffn(xe, gate, w1, w3, w2, layer)
    srcs, off = [], 0
    for tok in toks:
        cap = tok.shape[-1]
        src = (jnp.arange(e, dtype=jnp.int32)[None, :, None] * m + off
               + jnp.arange(b, dtype=jnp.int32)[:, None, None] * cap + jnp.arange(cap, dtype=jnp.int32))
        srcs.append(src.reshape(b, e * cap))
        off += b * cap
    tok_all = jnp.concatenate([tok.reshape(b, -1) for tok in toks], axis=1)
    src_bits = max(1, (e * m - 1).bit_length())
    assert (r - 1).bit_length() + src_bits <= 31
    packed = jnp.sort(tok_all * (1 << src_bits) + jnp.concatenate(srcs, axis=1), axis=1)
    tok_sorted, src_sorted = packed >> src_bits, packed & ((1 << src_bits) - 1)
    y_sorted = ye.reshape(e * m, d).at[src_sorted].get(mode="promise_in_bounds")
    starts = jnp.arange(r // TOK_TILE + 1, dtype=jnp.int32) * TOK_TILE
    lo = jnp.sum((tok_sorted[:, None, :] < starts[None, :, None]).astype(jnp.int32), axis=-1)
    return _combine(x_mid, tok_sorted, y_sorted, lo.reshape(-1), mods, gain, n_lat_tiles, final)


def _proj_odd_kernel(x_ref, ml_ref, mc_ref, g_ref, w_ref, qn_ref, kvn_ref, wq_ref, wkv_ref, cos_ref, sin_ref,
                     q_ref, k_ref, v_ref, rq_ref, rk_ref, rv_ref, rg_ref, *, q_scale, n_lat):
    h = _mixer_input(x_ref[0], g_ref[...], ml_ref[0], mc_ref[0], n_lat)
    acc = _dot(h.astype(BF16), w_ref[...])
    cos, sin = cos_ref[...], sin_ref[...]
    c0 = MLA_Q_RANK
    c1 = c0 + MLA_KV_RANK
    cq = acc[:, :c0]
    ckv = acc[:, c0:c1]
    kr = acc[:, c1:c1 + LANES] * cos + acc[:, c1 + LANES:c1 + 2 * LANES] * sin
    c2 = c1 + 2 * LANES
    nq = RET_HEADS * RET_QK_DIM
    nv = RET_HEADS * RET_V_DIM
    rq_ref[0] = acc[:, c2:c2 + nq].astype(BF16)
    rk_ref[0] = acc[:, c2 + nq:c2 + 2 * nq].astype(BF16)
    rv_ref[0] = acc[:, c2 + 2 * nq:c2 + 2 * nq + nv].astype(BF16)
    rg_ref[0] = acc[:, c2 + 2 * nq + nv:]

    nqk = MLA_HEADS * LANES
    qq = _dot((_rms(cq) * qn_ref[...]).astype(BF16), wq_ref[...])
    q = qq[:, :nqk] * _tile_lanes(cos, MLA_HEADS) + qq[:, nqk:] * _tile_lanes(sin, MLA_HEADS)
    q_ref[0] = (q * q_scale).astype(BF16)
    kv = _dot((_rms(ckv) * kvn_ref[...]).astype(BF16), wkv_ref[...])
    k_ref[0] = (kv[:, :nqk] + _tile_lanes(kr, MLA_HEADS)).astype(BF16)
    v_ref[0] = kv[:, nqk:].astype(BF16)


def _proj_odd(xa, mods, gain, w, q_norm, kv_norm, wq, wkv, cos, sin, n_lat):
    b, t, d = xa.shape
    nb = mods.shape[0] - 1
    tm = _row_tile(t)
    tok = lambda i, j: (i, j, 0)
    const = lambda i, j: (0, 0)
    widths = (MLA_HEADS * LANES, MLA_HEADS * LANES, MLA_HEADS * MLA_V_DIM,
              RET_HEADS * RET_QK_DIM, RET_HEADS * RET_QK_DIM, RET_HEADS * RET_V_DIM, RET_HEADS * RET_V_DIM)
    dtypes = (BF16,) * 6 + (F32,)
    kern = functools.partial(_proj_odd_kernel, q_scale=float((MLA_NOPE_DIM + MLA_ROPE_DIM) ** -0.5) * LOG2E,
                             n_lat=n_lat)
    return pl.pallas_call(
        kern,
        grid=(b, t // tm),
        in_specs=[pl.BlockSpec((1, tm, d), tok),
                  pl.BlockSpec((1, 6, d), lambda i, j: (i, 0, 0)),
                  pl.BlockSpec((1, 6, d), lambda i, j: (nb, 0, 0)),
                  pl.BlockSpec((1, d), const),
                  pl.BlockSpec(w.shape, const),
                  pl.BlockSpec(q_norm.shape, const),
                  pl.BlockSpec(kv_norm.shape, const),
                  pl.BlockSpec(wq.shape, const),
                  pl.BlockSpec(wkv.shape, const),
                  pl.BlockSpec((tm, LANES), lambda i, j: (j, 0)),
                  pl.BlockSpec((tm, LANES), lambda i, j: (j, 0))],
        out_specs=[pl.BlockSpec((1, tm, wd), tok) for wd in widths],
        out_shape=[jax.ShapeDtypeStruct((b, t, wd), dt) for wd, dt in zip(widths, dtypes)],
        compiler_params=_cparams("parallel", "arbitrary"),
    )(xa, mods, mods, gain, w, q_norm, kv_norm, wq, wkv, cos, sin)


def _mla_kernel(q_ref, k_ref, v_ref, o_ref, vx_ref):
    @pl.when(pl.program_id(2) == 0)
    def _():
        _fill_v_ones(vx_ref, v_ref)

    o0, o1 = _two_stream_attention(q_ref[0, :, :LANES], [(k_ref[0, :, :LANES], None)],
                                   q_ref[0, :, LANES:], [(k_ref[0, :, LANES:], None)], [vx_ref[...]])
    lane = lax.broadcasted_iota(jnp.int32, (1, LANES), 1)
    o_ref[0] = jnp.where(lane < MLA_V_DIM, o0, o1).astype(BF16)


def _mla_attn(q, k, v, n_lat, tq):
    b, t, _ = k.shape
    npairs = MLA_HEADS // 2
    return pl.pallas_call(
        _mla_kernel,
        grid=(b, npairs, n_lat // tq),
        in_specs=[pl.BlockSpec((1, tq, 2 * LANES), lambda i, h, j: (i, j, h)),
                  pl.BlockSpec((1, t, 2 * LANES), lambda i, h, j: (i, 0, h)),
                  pl.BlockSpec((1, t, LANES), lambda i, h, j: (i, 0, h))],
        out_specs=pl.BlockSpec((1, tq, LANES), lambda i, h, j: (i, j, h)),
        out_shape=jax.ShapeDtypeStruct((b, n_lat, npairs * LANES), BF16),
        scratch_shapes=[pltpu.VMEM((t, 2 * LANES), BF16)],
        compiler_params=_cparams("parallel", "parallel", "arbitrary"),
    )(q, k, v)


def _retention_kernel(dl_ref, q_ref, k_ref, v_ref, g_ref, o_ref, f_ref, b_ref, st_ref, *, n_lat):
    c = RET_CHUNK
    t = k_ref.shape[1]
    n_chunks = n_lat // c
    n_ctx = (t - n_lat) // c
    hp = pl.program_id(1)
    lane = lax.broadcasted_iota(jnp.int32, (1, LANES), 1)
    pos_r = lax.broadcasted_iota(jnp.int32, (c, 1), 0).astype(F32)
    ii = lax.broadcasted_iota(jnp.int32, (c, c), 0)
    jj = lax.broadcasted_iota(jnp.int32, (c, c), 1)
    rel = (ii - jj).astype(F32)
    dl = dl_ref[...]
    hsel = lax.broadcasted_iota(jnp.int32, (1, RET_HEADS), 1)

    chains = []
    for hh in range(2):
        hmask = (lane >= hh * RET_QK_DIM) & (lane < (hh + 1) * RET_QK_DIM)
        for direction in range(2):
            logit = jnp.sum(jnp.where(hsel == 2 * hp + hh, dl[direction:direction + 1], 0.0),
                            axis=-1, keepdims=True)
            lg = jnp.minimum(logit, 0.0) - jnp.log(1.0 + jnp.exp(-jnp.abs(logit)))
            if direction == 0:
                mask = rel >= 0
                d_in = jnp.where(mask, jnp.exp(lg * jnp.where(mask, rel, 0.0)), 0.0)
                d_q = jnp.exp(lg * (pos_r + 1.0))
                d_k = jnp.exp(lg * (c - 1.0 - pos_r))
            else:
                mask = rel < 0
                d_in = jnp.where(mask, jnp.exp(lg * jnp.where(mask, -rel, 0.0)), 0.0)
                d_q = jnp.exp(lg * (c - pos_r))
                d_k = jnp.exp(lg * pos_r)
            chains.append(dict(idx=2 * hh + direction, v0=hh * RET_V_DIM, direction=direction, hmask=hmask,
                               d_in=d_in, d_q=d_q, d_k=d_k, d_chunk=jnp.exp(lg * c)))

    st_ref[...] = jnp.zeros(st_ref.shape, F32)

    def kv_update(ch, start):
        kb = k_ref[0, pl.ds(start, c), :].astype(F32)
        vb = v_ref[0, pl.ds(start, c), ch["v0"]:ch["v0"] + RET_V_DIM]
        kd = jnp.where(ch["hmask"], kb * ch["d_k"], 0.0).astype(BF16)
        st_ref[ch["idx"]] = ch["d_chunk"] * st_ref[ch["idx"]] + _dot_tn(kd, vb)

    for i in range(n_ctx):
        for ch in chains:
            ci = i if ch["direction"] == 0 else n_ctx - 1 - i
            kv_update(ch, n_lat + ci * c)

    def lat_step(i, carry):
        starts = [pl.multiple_of((i if ch["direction"] == 0 else n_chunks - 1 - i) * c, c) for ch in chains]
        qms, avs, crs = [], [], []
        for ch, start in zip(chains, starts):
            qb = q_ref[0, pl.ds(start, c), :]
            qm = jnp.where(ch["hmask"], qb, jnp.zeros_like(qb))
            qms.append(qm)
            avs.append(_dot_nt(qm, k_ref[0, pl.ds(start, c), :]))
        for ch, qm in zip(chains, qms):
            qd = (qm.astype(F32) * ch["d_q"]).astype(BF16)
            crs.append(_dot(qd, st_ref[ch["idx"]].astype(BF16)))
        for ch, start, a, cross in zip(chains, starts, avs, crs):
            vb = v_ref[0, pl.ds(start, c), ch["v0"]:ch["v0"] + RET_V_DIM]
            res = _dot((a * ch["d_in"]).astype(BF16), vb) + cross
            dst = f_ref if ch["direction"] == 0 else b_ref
            dst[pl.ds(start, c), ch["v0"]:ch["v0"] + RET_V_DIM] = res
        for ch, start in zip(chains, starts):
            kv_update(ch, start)
        return carry

    lax.fori_loop(0, n_chunks, lat_step, 0)

    def finish(i, carry):
        start = pl.multiple_of(i * c, c)
        for hh in range(2):
            v0 = hh * RET_V_DIM
            r = _rms(f_ref[pl.ds(start, c), v0:v0 + RET_V_DIM] + b_ref[pl.ds(start, c), v0:v0 + RET_V_DIM])
            gate = _silu(g_ref[0, pl.ds(start, c), v0:v0 + RET_V_DIM])
            o_ref[0, pl.ds(start, c), v0:v0 + RET_V_DIM] = (r * gate).astype(BF16)
        return carry

    lax.fori_loop(0, n_chunks, finish, 0)


def _retention(rq, rk, rv, rg, decay_logit, n_lat):
    b, t, _ = rq.shape
    npairs = RET_HEADS // 2
    kern = functools.partial(_retention_kernel, n_lat=n_lat)
    return pl.pallas_call(
        kern,
        grid=(b, npairs),
        in_specs=[pl.BlockSpec(decay_logit.shape, lambda i, h: (0, 0)),
                  pl.BlockSpec((1, n_lat, LANES), lambda i, h: (i, 0, h)),
                  pl.BlockSpec((1, t, LANES), lambda i, h: (i, 0, h)),
                  pl.BlockSpec((1, t, 2 * RET_V_DIM), lambda i, h: (i, 0, h)),
                  pl.BlockSpec((1, n_lat, 2 * RET_V_DIM), lambda i, h: (i, 0, h))],
        out_specs=pl.BlockSpec((1, n_lat, 2 * RET_V_DIM), lambda i, h: (i, 0, h)),
        out_shape=jax.ShapeDtypeStruct((b, n_lat, RET_HEADS * RET_V_DIM), BF16),
        scratch_shapes=[pltpu.VMEM((n_lat, 2 * RET_V_DIM), F32), pltpu.VMEM((n_lat, 2 * RET_V_DIM), F32),
                        pltpu.VMEM((4, LANES, RET_V_DIM), F32)],
        compiler_params=_cparams("parallel", "arbitrary"),
    )(decay_logit, rq, rk, rv, rg)


def _rope_tables(n_lat, n_ctx, rot_dim, lane0):
    t = np.arange(n_lat)
    rows = (t // GRID_W).astype(np.float32)
    cols = (t % GRID_W).astype(np.float32)
    m = rot_dim // 4
    freqs = jnp.asarray(ROPE_BASE, F32) ** (-jnp.arange(m, dtype=F32) / m)
    ang_r = jnp.asarray(rows)[:, None] * freqs
    ang_c = jnp.asarray(cols)[:, None] * freqs
    cos = jnp.concatenate([jnp.cos(ang_r)] * 2 + [jnp.cos(ang_c)] * 2, axis=-1)
    sin = jnp.concatenate([-jnp.sin(ang_r), jnp.sin(ang_r), -jnp.sin(ang_c), jnp.sin(ang_c)], axis=-1)
    reps = (LANES - lane0) // rot_dim if lane0 == 0 else 1
    cos = jnp.concatenate([jnp.ones((n_lat, lane0), F32)] + [cos] * reps
                          + [jnp.ones((n_lat, LANES - lane0 - reps * rot_dim), F32)], axis=-1)
    sin = jnp.concatenate([jnp.zeros((n_lat, lane0), F32)] + [sin] * reps
                          + [jnp.zeros((n_lat, LANES - lane0 - reps * rot_dim), F32)], axis=-1)
    cos = jnp.concatenate([cos, jnp.ones((n_ctx, LANES), F32)], axis=0)
    sin = jnp.concatenate([sin, jnp.zeros((n_ctx, LANES), F32)], axis=0)
    return cos, sin


def _rot_partner(w, rot_dim):
    k, n = w.shape
    q = rot_dim // 4
    return w.reshape(k, n // rot_dim, 2, 2, q)[:, :, :, ::-1, :].reshape(k, n)


def _pad_cols(w, lane0, width=LANES):
    k, n = w.shape
    return jnp.concatenate([jnp.zeros((k, lane0), w.dtype), w, jnp.zeros((k, width - lane0 - n), w.dtype)], axis=-1)


def _even_weights(w_in):
    n_rope = 2 * DIFF_HEADS * 2 * DIFF_HEAD_DIM
    dw = DIFF_HEADS * 2 * DIFF_HEAD_DIM
    nw = NA_HEADS * NA_HEAD_DIM
    scale = jnp.concatenate([jnp.full((dw,), DIFF_HEAD_DIM ** -0.5, F32), jnp.ones((2 * dw,), F32),
                             jnp.full((nw,), NA_HEAD_DIM ** -0.5, F32), jnp.ones((2 * nw,), F32)])
    w = w_in * scale
    return jnp.concatenate([w, _rot_partner(w[:, :n_rope], DIFF_HEAD_DIM)], axis=-1).astype(BF16), n_rope


def _odd_weights(w_in, w_uq, w_ukv):
    c0 = MLA_Q_RANK
    c1 = c0 + MLA_KV_RANK
    c2 = c1 + MLA_ROPE_DIM
    nq = RET_HEADS * RET_QK_DIM
    kr = w_in[:, c1:c2]
    w = jnp.concatenate([w_in[:, :c1],
                         _pad_cols(kr, MLA_NOPE_DIM), _pad_cols(_rot_partner(kr, MLA_ROPE_DIM), MLA_NOPE_DIM),
                         w_in[:, c2:c2 + nq], w_in[:, c2 + nq:c2 + 2 * nq] * (RET_QK_DIM ** -0.5),
                         w_in[:, c2 + 2 * nq:]], axis=-1).astype(BF16)
    r = w_uq.shape[0]
    uq = w_uq.reshape(r, MLA_HEADS, MLA_NOPE_DIM + MLA_ROPE_DIM)
    pad = jnp.zeros((r, MLA_HEADS, LANES - MLA_NOPE_DIM - MLA_ROPE_DIM), F32)
    uq_rot = _rot_partner(uq[:, :, MLA_NOPE_DIM:].reshape(r, -1), MLA_ROPE_DIM).reshape(r, MLA_HEADS, MLA_ROPE_DIM)
    wq = jnp.concatenate([uq, pad], axis=-1).reshape(r, -1)
    wq_rot = jnp.concatenate([jnp.zeros_like(uq[:, :, :MLA_NOPE_DIM]), uq_rot, pad], axis=-1).reshape(r, -1)
    rk = w_ukv.shape[0]
    ukv = w_ukv.reshape(rk, MLA_HEADS, MLA_NOPE_DIM + MLA_V_DIM)
    wk = jnp.concatenate([ukv[:, :, :MLA_NOPE_DIM], jnp.zeros((rk, MLA_HEADS, LANES - MLA_NOPE_DIM), F32)],
                         axis=-1).reshape(rk, -1)
    wv = ukv[:, :, MLA_NOPE_DIM:].reshape(rk, -1)
    return w, jnp.concatenate([wq, wq_rot], axis=-1).astype(BF16), jnp.concatenate([wk, wv], axis=-1).astype(BF16)


def _na_bias(rpb, n_lat):
    pat_of_group, idx_r, idx_c, valid = _na_patterns(n_lat // GRID_W)
    g = NA_GROUP_ROWS
    span = g + NA_WIN_ROWS - 1
    n_pat = idx_r.shape[0]
    col_sel = (idx_c[0, :GRID_W, :GRID_W, None] == np.arange(2 * NA_WIN_COLS - 1)).astype(np.float32)
    row_idx = idx_r.reshape(n_pat, g, GRID_W, span, GRID_W)[:, :, 0, :, 0]
    cols = jnp.einsum('hrc,qkc->hrqk', rpb, col_sel, precision=lax.Precision.HIGHEST)
    bias = jnp.take(cols, row_idx.reshape(-1), axis=1).reshape(rpb.shape[0], n_pat, g, span, GRID_W, GRID_W)
    bias = jnp.swapaxes(bias, 3, 4).reshape(rpb.shape[0], n_pat, g * GRID_W, span * GRID_W)
    return jnp.where(valid[None], bias * LOG2E, NEG_BIG), pat_of_group


def _router_weights(w_router):
    return _pad_cols(w_router, 0)


def kernel(x, c, ctx, c_ctx, ada_w, ada_b, norm_mix, norm_ffn, final_norm, even_w_in, even_w_out,
           diff_lambda, diff_subln, na_rpb, odd_w_in, odd_w_out, mla_q_norm, mla_w_uq, mla_kv_norm,
           mla_w_ukv, ret_decay_logit, moe_router, moe_w1, moe_w3, moe_w2):
    b, n_lat, d = x.shape
    n_ctx = ctx.shape[1]
    depth = ada_w.shape[0]
    n_lat_tiles = n_lat // TOK_TILE

    cond = jnp.concatenate([c, c_ctx[None], jnp.zeros((7, d), F32)], axis=0)
    mods_all = _mods(cond, ada_w, ada_b)[:, :b + 1].reshape(depth, b + 1, 6, d)
    xa = jnp.concatenate([x, ctx], axis=1)

    for l in range(depth):
        mods = mods_all[l]
        need_ctx = l < depth - 1
        i = l // 2
        gain_mix = norm_mix[l][None]
        if l % 2 == 0:
            w, n_rope = _even_weights(even_w_in[i])
            cos, sin = _rope_tables(n_lat, n_ctx, DIFF_HEAD_DIM, 0)
            qkv = _proj_even(xa, mods, gain_mix, w, cos, sin, n_lat, n_rope, even_w_in.shape[-1])
            lam_init = 0.8 - 0.6 * math.exp(-0.3 * l)
            mix_a = _diff_attn(qkv, diff_lambda[i], diff_subln[i][None], lam_init, 0, n_lat, ATTN_Q_TILE,
                               0, n_lat + n_ctx)
            mix_a_ctx = _diff_attn(qkv, diff_lambda[i], diff_subln[i][None], lam_init, n_lat, n_ctx, n_ctx,
                                   n_lat, n_ctx)
            bias, offsets = _na_bias(na_rpb[i], n_lat)
            mix_b = _na_attn(qkv, bias, n_lat, offsets, 3 * DIFF_HEADS * 2 * DIFF_HEAD_DIM)
            w_out = even_w_out[i].astype(BF16)
        else:
            w, wq, wkv = _odd_weights(odd_w_in[i], mla_w_uq[i], mla_w_ukv[i])
            cos, sin = _rope_tables(n_lat, n_ctx, MLA_ROPE_DIM, MLA_NOPE_DIM)
            q, k, v, rq, rk, rv, rg = _proj_odd(xa, mods, gain_mix, w, mla_q_norm[i][None], mla_kv_norm[i][None],
                                                wq, wkv, cos, sin, n_lat)
            mix_a = _mla_attn(q, k, v, n_lat, ATTN_Q_TILE)
            mix_a_ctx = mix_a
            mix_b = _retention(rq, rk, rv, rg, ret_decay_logit[i], n_lat)
            w_out = odd_w_out[i].astype(BF16)

        tiles = (n_lat + n_ctx) // TOK_TILE if need_ctx else n_lat_tiles
        x_mid, h2, aff = _proj_out(xa, mix_a, mix_a_ctx, mix_b, w_out, mods, norm_ffn[l][None],
                                   _router_weights(moe_router[l]), tiles, n_lat_tiles)
        segments = ((0, n_lat), (n_lat, n_ctx)) if need_ctx else ((0, n_lat),)
        xa = _moe(x_mid, h2, aff, segments, mods, final_norm[None], n_lat_tiles, l == depth - 1,
                  moe_w1, moe_w3, moe_w2, l)
    return xa
```

```python
import functools
import math

import jax
import jax.numpy as jnp
import numpy as np
from jax import lax
from jax.experimental import pallas as pl
from jax.experimental.pallas import tpu as pltpu

GRID_W = 64
ROPE_BASE = 10000.0
RMS_EPS = 1e-6
DIFF_HEADS = 4
DIFF_HEAD_DIM = 64
NA_HEADS = 8
NA_HEAD_DIM = 64
NA_WIN_ROWS = 8
NA_WIN_COLS = 16
MLA_HEADS = 8
MLA_Q_RANK = 256
MLA_KV_RANK = 128
MLA_NOPE_DIM = 64
MLA_ROPE_DIM = 32
MLA_V_DIM = 64
RET_HEADS = 4
RET_QK_DIM = 64
RET_V_DIM = 128
RET_CHUNK = 128
N_EXPERTS = 16
EC_CAPACITY_FACTOR = 2

LANES = 128
VMEM_LIMIT = 56 * 1024 * 1024
NEG_BIG = -1e30
TOK_TILE = 256
ROW_TILES = 8
ATTN_Q_TILE = 512
FFN_ROW_TILES = 8
FFN_HIDDEN_BLOCK = 512
NA_GROUP_ROWS = 4
NA_GROUPS_PER_STEP = 2
PAIR_CHUNK = 256
LOG2E = math.log2(math.e)

BF16 = jnp.bfloat16
F32 = jnp.float32


def _cparams(*sem):
    return pltpu.CompilerParams(dimension_semantics=sem, vmem_limit_bytes=VMEM_LIMIT)


def _dot(a, b):
    return jnp.dot(a, b, preferred_element_type=F32)


def _dot_nt(a, b):
    return lax.dot_general(a, b, (((1,), (1,)), ((), ())), preferred_element_type=F32)


def _dot_tn(a, b):
    return lax.dot_general(a, b, (((0,), (0,)), ((), ())), preferred_element_type=F32)


def _rms(x):
    return x * lax.rsqrt(jnp.mean(x * x, axis=-1, keepdims=True) + RMS_EPS)


def _silu(x):
    return x * (1.0 / (1.0 + jnp.exp(-x)))


def _tile_lanes(t, n):
    return jnp.concatenate([t] * n, axis=-1)


def _mods_kernel(c_ref, w_ref, b_ref, o_ref):
    o_ref[0] = _dot(_silu(c_ref[...]), w_ref[0]) + b_ref[0]


def _mods(cond, ada_w, ada_b):
    depth, d, n = ada_w.shape
    r = cond.shape[0]
    tn = 1536
    return pl.pallas_call(
        _mods_kernel,
        grid=(depth, n // tn),
        in_specs=[pl.BlockSpec((r, d), lambda l, j: (0, 0)),
                  pl.BlockSpec((1, d, tn), lambda l, j: (l, 0, j)),
                  pl.BlockSpec((1, 1, tn), lambda l, j: (l, 0, j))],
        out_specs=pl.BlockSpec((1, r, tn), lambda l, j: (l, 0, j)),
        out_shape=jax.ShapeDtypeStruct((depth, r, n), F32),
        compiler_params=_cparams("arbitrary", "arbitrary"),
    )(cond, ada_w, ada_b.reshape(depth, 1, n))


def _normed(x, gain, mods, shift_row, scale_row):
    return _rms(x) * gain * (1.0 + mods[scale_row:scale_row + 1]) + mods[shift_row:shift_row + 1]


def _mixer_input(x, gain, mods_lat, mods_ctx, n_lat):
    tm = x.shape[0]
    rows = pl.program_id(1) * tm + lax.broadcasted_iota(jnp.int32, (tm, 1), 0)
    is_ctx = rows >= n_lat
    scale = jnp.where(is_ctx, mods_ctx[1:2], mods_lat[1:2])
    shift = jnp.where(is_ctx, mods_ctx[0:1], mods_lat[0:1])
    return _rms(x) * gain * (1.0 + scale) + shift


def _proj_even_kernel(x_ref, ml_ref, mc_ref, g_ref, w_ref, cos_ref, sin_ref, o_ref, *, n_rope, n_out, n_lat):
    na_q0 = n_rope + n_rope // 2
    na_q1 = na_q0 + NA_HEADS * NA_HEAD_DIM
    h = _mixer_input(x_ref[0], g_ref[...], ml_ref[0], mc_ref[0], n_lat)
    acc = _dot(h.astype(BF16), w_ref[...])
    reps = n_rope // LANES
    cos = _tile_lanes(cos_ref[...], reps)
    sin = _tile_lanes(sin_ref[...], reps)
    roped = acc[:, :n_rope] * cos + acc[:, n_out:] * sin
    n_q = n_rope // 2
    o_ref[0, :, :n_q] = (roped[:, :n_q] * LOG2E).astype(BF16)
    o_ref[0, :, n_q:n_rope] = roped[:, n_q:].astype(BF16)
    o_ref[0, :, n_rope:na_q0] = acc[:, n_rope:na_q0].astype(BF16)
    o_ref[0, :, na_q0:na_q1] = (acc[:, na_q0:na_q1] * LOG2E).astype(BF16)
    o_ref[0, :, na_q1:] = acc[:, na_q1:n_out].astype(BF16)


def _row_tile(t):
    tm = t // ROW_TILES
    assert tm * ROW_TILES == t and tm % 16 == 0
    return tm


def _proj_even(xa, mods, gain, w, cos, sin, n_lat, n_rope, n_out):
    b, t, d = xa.shape
    nb = mods.shape[0] - 1
    tm = _row_tile(t)
    kern = functools.partial(_proj_even_kernel, n_rope=n_rope, n_out=n_out, n_lat=n_lat)
    return pl.pallas_call(
        kern,
        grid=(b, t // tm),
        in_specs=[pl.BlockSpec((1, tm, d), lambda i, j: (i, j, 0)),
                  pl.BlockSpec((1, 6, d), lambda i, j: (i, 0, 0)),
                  pl.BlockSpec((1, 6, d), lambda i, j: (nb, 0, 0)),
                  pl.BlockSpec((1, d), lambda i, j: (0, 0)),
                  pl.BlockSpec(w.shape, lambda i, j: (0, 0)),
                  pl.BlockSpec((tm, LANES), lambda i, j: (j, 0)),
                  pl.BlockSpec((tm, LANES), lambda i, j: (j, 0))],
        out_specs=pl.BlockSpec((1, tm, n_out), lambda i, j: (i, j, 0)),
        out_shape=jax.ShapeDtypeStruct((b, t, n_out), BF16),
        compiler_params=_cparams("parallel", "arbitrary"),
    )(xa, mods, mods, gain, w, cos, sin)


def _fill_v_ones(vx_ref, v_ref):
    vx_ref[:, :LANES] = v_ref[0]
    vx_ref[:, LANES:] = jnp.ones((vx_ref.shape[0], LANES), BF16)


def _weights(s, m):
    return jnp.exp2(s - m).astype(BF16)


def _rowmax(*ss):
    m = jnp.max(ss[0], axis=-1, keepdims=True)
    for s in ss[1:]:
        m = jnp.maximum(m, jnp.max(s, axis=-1, keepdims=True))
    return m


def _normalised(ox):
    return ox[:, :LANES] / ox[:, LANES:LANES + 1]


def _scores(q, segs):
    return [_dot_nt(q, k) if bias is None else _dot_nt(q, k) + bias for k, bias in segs]


def _pv(ps, vxs):
    o = _dot(ps[0], vxs[0])
    for p, vx in zip(ps[1:], vxs[1:]):
        o = o + _dot(p, vx)
    return o


def _multi_stream_attention(streams):
    n = len(streams)
    s, m, p, o = [None] * n, [None] * n, [None] * n, [None] * n
    for t in range(n + 2):
        if t < n:
            s[t] = _scores(streams[t][0], streams[t][1])
        if 0 <= t - 2 < n:
            o[t - 2] = _normalised(_pv(p[t - 2], streams[t - 2][2]))
        if 0 <= t - 1 < n:
            p[t - 1] = [_weights(x, m[t - 1]) for x in s[t - 1]]
        if t < n:
            m[t] = _rowmax(*s[t])
    return o


def _two_stream_attention(qa, segs_a, qb, segs_b, vxs):
    return _multi_stream_attention([(qa, segs_a, vxs), (qb, segs_b, vxs)])


def _diff_attn_kernel(lam_ref, g_ref, q_ref, k_ref, v_ref, o_ref, vx_ref, *, lam_init):
    @pl.when(pl.program_id(2) == 0)
    def _():
        _fill_v_ones(vx_ref, v_ref)

    lp = lam_ref[...]
    s1 = jnp.sum(lp[0:1] * lp[1:2], axis=-1, keepdims=True)
    s2 = jnp.sum(lp[2:3] * lp[3:4], axis=-1, keepdims=True)
    lam = jnp.exp(s1) - jnp.exp(s2) + lam_init
    q = q_ref[0]
    lane = lax.broadcasted_iota(jnp.int32, (1, LANES), 1)
    q1 = jnp.where(lane < DIFF_HEAD_DIM, q, jnp.zeros_like(q))
    q2 = jnp.where(lane >= DIFF_HEAD_DIM, q, jnp.zeros_like(q))

    k = k_ref[0]
    o1, o2 = _two_stream_attention(q1, [(k, None)], q2, [(k, None)], [vx_ref[...]])
    o = o1 - lam * o2
    o_ref[0] = (_rms(o) * g_ref[...] * (1.0 - lam_init)).astype(BF16)


def _diff_attn(qkv, lam_params, subln, lam_init, q_row0, n_q, tq, k_row0, n_k):
    b = qkv.shape[0]
    nh = DIFF_HEADS
    q0, k0 = q_row0 // tq, k_row0 // n_k
    assert q0 * tq == q_row0 and k0 * n_k == k_row0 and n_q % tq == 0
    return pl.pallas_call(
        functools.partial(_diff_attn_kernel, lam_init=lam_init),
        grid=(b, nh, n_q // tq),
        in_specs=[pl.BlockSpec(lam_params.shape, lambda i, h, j: (0, 0)),
                  pl.BlockSpec((1, LANES), lambda i, h, j: (0, 0)),
                  pl.BlockSpec((1, tq, LANES), lambda i, h, j: (i, q0 + j, h)),
                  pl.BlockSpec((1, n_k, LANES), lambda i, h, j: (i, k0, nh + h)),
                  pl.BlockSpec((1, n_k, LANES), lambda i, h, j: (i, k0, 2 * nh + h))],
        out_specs=pl.BlockSpec((1, tq, LANES), lambda i, h, j: (i, j, h)),
        out_shape=jax.ShapeDtypeStruct((b, n_q, nh * LANES), BF16),
        scratch_shapes=[pltpu.VMEM((n_k, 2 * LANES), BF16)],
        compiler_params=_cparams("parallel", "parallel", "arbitrary"),
    )(lam_params, subln, qkv, qkv, qkv)


def _na_patterns(rows):
    g = NA_GROUP_ROWS
    span = g + NA_WIN_ROWS - 1
    nq, nk = g * GRID_W, span * GRID_W
    qi = np.arange(nq)[:, None]
    ki = np.arange(nk)[None, :]
    qcol, kcol = qi % GRID_W, ki % GRID_W
    cs = np.clip(qcol - NA_WIN_COLS // 2, 0, GRID_W - NA_WIN_COLS)
    col_ok = (kcol >= cs) & (kcol < cs + NA_WIN_COLS)
    idx_c = np.clip(kcol - qcol + NA_WIN_COLS - 1, 0, 2 * NA_WIN_COLS - 2) + 0 * qi
    seen, pat_of_group, idx_r, valid = {}, [], [], []
    for r0 in range(0, rows, g):
        ks = min(max(r0 - NA_WIN_ROWS // 2, 0), rows - span)
        qrow = r0 + qi // GRID_W
        krow = ks + ki // GRID_W
        rs = np.clip(qrow - NA_WIN_ROWS // 2, 0, rows - NA_WIN_ROWS)
        ok = (krow >= rs) & (krow < rs + NA_WIN_ROWS) & col_ok
        ir = np.clip(krow - qrow + NA_WIN_ROWS - 1, 0, 2 * NA_WIN_ROWS - 2) + 0 * kcol
        key = (ok.tobytes(), np.where(ok, ir, 0).tobytes())
        if key not in seen:
            seen[key] = len(idx_r)
            idx_r.append(ir)
            valid.append(ok)
        pat_of_group.append(seen[key])
    n_pat = len(idx_r)
    return tuple(pat_of_group), np.stack(idx_r), np.stack([idx_c] * n_pat), np.stack(valid)


def _na_kernel(q_ref, k_ref, v_ref, bias_ref, o_ref, vx_ref, *, n_lat, pat_of_group):
    g_rows = NA_GROUP_ROWS
    span = g_rows + NA_WIN_ROWS - 1
    rows = n_lat // GRID_W
    nq, nk = g_rows * GRID_W, span * GRID_W
    lane = lax.broadcasted_iota(jnp.int32, (1, LANES), 1)
    head_masks = (lane < NA_HEAD_DIM, lane >= NA_HEAD_DIM)
    _fill_v_ones(vx_ref, v_ref)
    kc = k_ref[0, n_lat:, :]
    vxc = vx_ref[n_lat:, :]

    def both_heads(blocks):
        streams = []
        for q, segs, vxs in blocks:
            for hh, hm in enumerate(head_masks):
                streams.append((jnp.where(hm, q, jnp.zeros_like(q)),
                                [(k, None if bi is None else bias_ref[hh, bi]) for k, bi in segs], vxs))
        outs = _multi_stream_attention(streams)
        return [jnp.where(head_masks[0], outs[2 * i], outs[2 * i + 1]).astype(BF16) for i in range(len(blocks))]

    def group_block(g):
        r0 = g * g_rows
        ks = jnp.clip(r0 - NA_WIN_ROWS // 2, 0, rows - span)
        common = max(set(pat_of_group), key=pat_of_group.count)
        pat = jnp.int32(common)
        for gi, p in enumerate(pat_of_group):
            if p != common:
                pat = jnp.where(g == gi, p, pat)
        qg = q_ref[0, pl.ds(pl.multiple_of(g * nq, nq), nq), :]
        kw = k_ref[0, pl.ds(pl.multiple_of(ks * GRID_W, GRID_W), nk), :]
        vxw = vx_ref[pl.ds(pl.multiple_of(ks * GRID_W, GRID_W), nk), :]
        return qg, [(kw, pat), (kc, None)], [vxw, vxc]

    def group_step(i, carry):
        gs = [i * NA_GROUPS_PER_STEP + u for u in range(NA_GROUPS_PER_STEP)]
        for g, out in zip(gs, both_heads([group_block(g) for g in gs])):
            o_ref[0, pl.ds(pl.multiple_of(g * nq, nq), nq), :] = out
        return carry

    lax.fori_loop(0, rows // g_rows // NA_GROUPS_PER_STEP, group_step, 0)

    o_ref[0, n_lat:, :] = both_heads([(q_ref[0, n_lat:, :], [(kc, None)], [vxc])])[0]


def _na_attn(qkv, bias, n_lat, pat_of_group, col0):
    b, t, _ = qkv.shape
    npairs = NA_HEADS // 2
    cb = col0 // LANES
    kern = functools.partial(_na_kernel, n_lat=n_lat, pat_of_group=pat_of_group)
    return pl.pallas_call(
        kern,
        grid=(b, npairs),
        in_specs=[pl.BlockSpec((1, t, LANES), lambda i, h: (i, 0, cb + h)),
                  pl.BlockSpec((1, t, LANES), lambda i, h: (i, 0, cb + npairs + h)),
                  pl.BlockSpec((1, t, LANES), lambda i, h: (i, 0, cb + 2 * npairs + h)),
                  pl.BlockSpec((2,) + bias.shape[1:], lambda i, h: (h, 0, 0, 0))],
        out_specs=pl.BlockSpec((1, t, LANES), lambda i, h: (i, 0, h)),
        out_shape=jax.ShapeDtypeStruct((b, t, npairs * LANES), BF16),
        scratch_shapes=[pltpu.VMEM((t, 2 * LANES), BF16)],
        compiler_params=_cparams("parallel", "arbitrary"),
    )(qkv, qkv, qkv, bias)


def _proj_out_kernel(x_ref, a_ref, ac_ref, b_ref, wa_ref, wb_ref, m_ref, g_ref, wr_ref, xo_ref, h_ref, aff_ref,
                     *, n_lat_tiles):
    mods = m_ref[0]
    a = jnp.where(pl.program_id(1) >= n_lat_tiles, ac_ref[0], a_ref[0])
    y = _dot(a, wa_ref[...]) + _dot(b_ref[0], wb_ref[...])
    x = x_ref[0] + mods[2:3] * y
    xo_ref[0] = x
    h2 = _normed(x, g_ref[...], mods, 3, 4)
    h_hi = h2.astype(BF16)
    h_ref[0] = h_hi
    h_lo = (h2 - h_hi.astype(F32)).astype(BF16)
    wr = wr_ref[...]
    w_hi = wr.astype(BF16)
    w_lo = (wr - w_hi.astype(F32)).astype(BF16)
    hh = _dot(h_hi, jnp.concatenate([w_hi, w_lo], axis=-1))
    logits = hh[:, :LANES] + (hh[:, LANES:] + _dot(h_lo, w_hi))
    lane = lax.broadcasted_iota(jnp.int32, logits.shape, 1)
    logits = jnp.where(lane < N_EXPERTS, logits, NEG_BIG)
    e = jnp.exp(logits - jnp.max(logits, axis=-1, keepdims=True))
    aff_ref[0] = e / jnp.sum(e, axis=-1, keepdims=True)


def _proj_out(xa, a, a_ctx, bb, w_out, mods, gain, w_router, n_tiles, n_lat_tiles):
    b, t, d = xa.shape
    nb = mods.shape[0] - 1
    tm = TOK_TILE
    wa, wb = w_out[:a.shape[-1]], w_out[a.shape[-1]:]
    tok = lambda i, j: (i, j, 0)
    const = lambda i, j: (0, 0)
    return pl.pallas_call(
        functools.partial(_proj_out_kernel, n_lat_tiles=n_lat_tiles),
        grid=(b, n_tiles),
        in_specs=[pl.BlockSpec((1, tm, d), tok),
                  pl.BlockSpec((1, tm, a.shape[-1]), lambda i, j: (i, jnp.minimum(j, n_lat_tiles - 1), 0)),
                  pl.BlockSpec((1, tm, a.shape[-1]), lambda i, j: (i, jnp.maximum(j - n_lat_tiles, 0), 0)),
                  pl.BlockSpec((1, tm, bb.shape[-1]), tok),
                  pl.BlockSpec(wa.shape, const),
                  pl.BlockSpec(wb.shape, const),
                  pl.BlockSpec((1, 6, d), lambda i, j: (jnp.where(j >= n_lat_tiles, nb, i), 0, 0)),
                  pl.BlockSpec((1, d), const),
                  pl.BlockSpec(w_router.shape, const)],
        out_specs=[pl.BlockSpec((1, tm, d), tok),
                   pl.BlockSpec((1, tm, d), tok),
                   pl.BlockSpec((1, tm, LANES), tok)],
        out_shape=[jax.ShapeDtypeStruct((b, n_tiles * tm, d), F32),
                   jax.ShapeDtypeStruct((b, n_tiles * tm, d), BF16),
                   jax.ShapeDtypeStruct((b, n_tiles * tm, LANES), F32)],
        compiler_params=_cparams("parallel", "arbitrary"),
    )(xa, a, a_ctx, bb, wa, wb, mods, gain, w_router)


def _ffn_kernel(x_ref, gate_ref, w1_ref, w3_ref, w2_ref, o_ref, w1b, w3b, w2b):
    @pl.when(pl.program_id(1) == 0)
    def _():
        w1b[...] = w1_ref[0].astype(BF16)
        w3b[...] = w3_ref[0].astype(BF16)
        w2b[...] = w2_ref[0].astype(BF16)

    x = x_ref[0]
    f = w1b.shape[1]
    y = None
    for f0 in range(0, f, FFN_HIDDEN_BLOCK):
        f1 = f0 + FFN_HIDDEN_BLOCK
        hid = _silu(_dot(x, w1b[:, f0:f1])) * _dot(x, w3b[:, f0:f1])
        part = _dot(hid.astype(BF16), w2b[f0:f1, :])
        y = part if y is None else y + part
    o_ref[0] = (y * gate_ref[0]).astype(BF16)


def _expert_ffn(xe, gate, w1, w3, w2, layer):
    e, m, d = xe.shape
    f = w1.shape[-1]
    tm = m // FFN_ROW_TILES
    assert tm * FFN_ROW_TILES == m and tm % 16 == 0
    wspec = lambda r, c: pl.BlockSpec((None, 1, r, c), lambda i, j: (layer, i, 0, 0))
    return pl.pallas_call(
        _ffn_kernel,
        grid=(e, m // tm),
        in_specs=[pl.BlockSpec((1, tm, d), lambda i, j: (i, j, 0)),
                  pl.BlockSpec((1, tm, 1), lambda i, j: (i, j, 0)),
                  wspec(d, f), wspec(d, f), wspec(f, d)],
        out_specs=pl.BlockSpec((1, tm, d), lambda i, j: (i, j, 0)),
        out_shape=jax.ShapeDtypeStruct((e, m, d), BF16),
        scratch_shapes=[pltpu.VMEM((d, f), BF16), pltpu.VMEM((d, f), BF16), pltpu.VMEM((f, d), BF16)],
        compiler_params=_cparams("parallel", "arbitrary"),
    )(xe, gate, w1, w3, w2)


def _route(aff, segments):
    b, r, _ = aff.shape
    e = N_EXPERTS
    gates, flats, toks = [], [], []
    for row0, n in segments:
        cap = (EC_CAPACITY_FACTOR * n) // e
        gate, idx = lax.top_k(jnp.swapaxes(aff[:, row0:row0 + n, :e], 1, 2), cap)
        tok = idx + row0
        gates.append(jnp.swapaxes(gate, 0, 1).reshape(e, b * cap))
        flats.append(jnp.swapaxes(tok + jnp.arange(b, dtype=idx.dtype)[:, None, None] * r, 0, 1).reshape(e, b * cap))
        toks.append(tok)
    return jnp.concatenate(gates, axis=1)[..., None], jnp.concatenate(flats, axis=1), toks


def _combine_kernel(lo_ref, x_ref, tok_ref, y_ref, m_ref, g_ref, o_ref, *, final):
    i, j = pl.program_id(0), pl.program_id(1)
    tm = x_ref.shape[1]
    n_tiles = pl.num_programs(1)
    lo = lo_ref[i * (n_tiles + 1) + j]
    hi = lo_ref[i * (n_tiles + 1) + j + 1]
    rows = j * tm + lax.broadcasted_iota(jnp.int32, (tm, 1), 0)

    def chunk_sum(c):
        sel = (tok_ref[0, pl.ds(c, 1), :] == rows).astype(BF16)
        return _dot(sel, y_ref[0, pl.ds(pl.multiple_of(c * PAIR_CHUNK, PAIR_CHUNK), PAIR_CHUNK), :])

    def chunk(c, carry):
        o_ref[0] += chunk_sum(c)
        return carry

    c0 = jnp.minimum(lo // PAIR_CHUNK, tok_ref.shape[1] - 1)
    o_ref[0] = chunk_sum(c0)
    lax.fori_loop(c0 + 1, (hi + PAIR_CHUNK - 1) // PAIR_CHUNK, chunk, 0)
    x = x_ref[0] + m_ref[0][5:6] * o_ref[0]
    if final:
        x = _rms(x) * g_ref[...]
    o_ref[0] = x


def _combine(x, tok_sorted, y_sorted, lo, mods, gain, n_lat_tiles, final):
    b, t, d = x.shape
    p = tok_sorted.shape[1]
    nb = mods.shape[0] - 1
    tm = TOK_TILE
    tok = lambda i, j, lo_ref: (i, j, 0)
    whole = lambda i, j, lo_ref: (i, 0, 0)
    grid_spec = pltpu.PrefetchScalarGridSpec(
        num_scalar_prefetch=1,
        grid=(b, t // tm),
        in_specs=[pl.BlockSpec((1, tm, d), tok),
                  pl.BlockSpec((1, p // PAIR_CHUNK, PAIR_CHUNK), whole),
                  pl.BlockSpec((1, p, d), whole),
                  pl.BlockSpec((1, 6, d), lambda i, j, lo_ref: (jnp.where(j >= n_lat_tiles, nb, i), 0, 0)),
                  pl.BlockSpec((1, d), lambda i, j, lo_ref: (0, 0))],
        out_specs=pl.BlockSpec((1, tm, d), tok))
    return pl.pallas_call(
        functools.partial(_combine_kernel, final=final),
        grid_spec=grid_spec,
        out_shape=jax.ShapeDtypeStruct((b, t, d), F32),
        compiler_params=_cparams("parallel", "arbitrary"),
    )(lo, x, tok_sorted.reshape(b, p // PAIR_CHUNK, PAIR_CHUNK), y_sorted, mods, gain)


def _moe(x_mid, h2, aff, segments, mods, gain, n_lat_tiles, final, w1, w3, w2, layer):
    b, r, d = h2.shape
    gate, flat, toks = _route(aff, segments)
    e, m = flat.shape
    xe = h2.reshape(b * r, d).at[flat].get(mode="promise_in_bounds")
    ye = _expert_ffn(xe, gate, w1, w3, w2, layer)
    srcs, off = [], 0
    for tok in toks:
        cap = tok.shape[-1]
        src = (jnp.arange(e, dtype=jnp.int32)[None, :, None] * m + off
               + jnp.arange(b, dtype=jnp.int32)[:, None, None] * cap + jnp.arange(cap, dtype=jnp.int32))
        srcs.append(src.reshape(b, e * cap))
        off += b * cap
    tok_all = jnp.concatenate([tok.reshape(b, -1) for tok in toks], axis=1)
    src_bits = max(1, (e * m - 1).bit_length())
    assert (r - 1).bit_length() + src_bits <= 31
    packed = jnp.sort(tok_all * (1 << src_bits) + jnp.concatenate(srcs, axis=1), axis=1)
    tok_sorted, src_sorted = packed >> src_bits, packed & ((1 << src_bits) - 1)
    y_sorted = ye.reshape(e * m, d).at[src_sorted].get(mode="promise_in_bounds")
    starts = jnp.arange(r // TOK_TILE + 1, dtype=jnp.int32) * TOK_TILE
    lo = jnp.sum((tok_sorted[:, None, :] < starts[None, :, None]).astype(jnp.int32), axis=-1)
    return _combine(x_mid, tok_sorted, y_sorted, lo.reshape(-1), mods, gain, n_lat_tiles, final)


def _proj_odd_kernel(x_ref, ml_ref, mc_ref, g_ref, w_ref, qn_ref, kvn_ref, wq_ref, wkv_ref, cos_ref, sin_ref,
                     q_ref, k_ref, v_ref, rq_ref, rk_ref, rv_ref, rg_ref, *, q_scale, n_lat):
    h = _mixer_input(x_ref[0], g_ref[...], ml_ref[0], mc_ref[0], n_lat)
    acc = _dot(h.astype(BF16), w_ref[...])
    cos, sin = cos_ref[...], sin_ref[...]
    c0 = MLA_Q_RANK
    c1 = c0 + MLA_KV_RANK
    cq = acc[:, :c0]
    ckv = acc[:, c0:c1]
    kr = acc[:, c1:c1 + LANES] * cos + acc[:, c1 + LANES:c1 + 2 * LANES] * sin
    c2 = c1 + 2 * LANES
    nq = RET_HEADS * RET_QK_DIM
    nv = RET_HEADS * RET_V_DIM
    rq_ref[0] = acc[:, c2:c2 + nq].astype(BF16)
    rk_ref[0] = acc[:, c2 + nq:c2 + 2 * nq].astype(BF16)
    rv_ref[0] = acc[:, c2 + 2 * nq:c2 + 2 * nq + nv].astype(BF16)
    rg_ref[0] = acc[:, c2 + 2 * nq + nv:]

    nqk = MLA_HEADS * LANES
    qq = _dot((_rms(cq) * qn_ref[...]).astype(BF16), wq_ref[...])
    q = qq[:, :nqk] * _tile_lanes(cos, MLA_HEADS) + qq[:, nqk:] * _tile_lanes(sin, MLA_HEADS)
    q_ref[0] = (q * q_scale).astype(BF16)
    kv = _dot((_rms(ckv) * kvn_ref[...]).astype(BF16), wkv_ref[...])
    k_ref[0] = (kv[:, :nqk] + _tile_lanes(kr, MLA_HEADS)).astype(BF16)
    v_ref[0] = kv[:, nqk:].astype(BF16)


def _proj_odd(xa, mods, gain, w, q_norm, kv_norm, wq, wkv, cos, sin, n_lat):
    b, t, d = xa.shape
    nb = mods.shape[0] - 1
    tm = _row_tile(t)
    tok = lambda i, j: (i, j, 0)
    const = lambda i, j: (0, 0)
    widths = (MLA_HEADS * LANES, MLA_HEADS * LANES, MLA_HEADS * MLA_V_DIM,
              RET_HEADS * RET_QK_DIM, RET_HEADS * RET_QK_DIM, RET_HEADS * RET_V_DIM, RET_HEADS * RET_V_DIM)
    dtypes = (BF16,) * 6 + (F32,)
    kern = functools.partial(_proj_odd_kernel, q_scale=float((MLA_NOPE_DIM + MLA_ROPE_DIM) ** -0.5) * LOG2E,
                             n_lat=n_lat)
    return pl.pallas_call(
        kern,
        grid=(b, t // tm),
        in_specs=[pl.BlockSpec((1, tm, d), tok),
                  pl.BlockSpec((1, 6, d), lambda i, j: (i, 0, 0)),
                  pl.BlockSpec((1, 6, d), lambda i, j: (nb, 0, 0)),
                  pl.BlockSpec((1, d), const),
                  pl.BlockSpec(w.shape, const),
                  pl.BlockSpec(q_norm.shape, const),
                  pl.BlockSpec(kv_norm.shape, const),
                  pl.BlockSpec(wq.shape, const),
                  pl.BlockSpec(wkv.shape, const),
                  pl.BlockSpec((tm, LANES), lambda i, j: (j, 0)),
                  pl.BlockSpec((tm, LANES), lambda i, j: (j, 0))],
        out_specs=[pl.BlockSpec((1, tm, wd), tok) for wd in widths],
        out_shape=[jax.ShapeDtypeStruct((b, t, wd), dt) for wd, dt in zip(widths, dtypes)],
        compiler_params=_cparams("parallel", "arbitrary"),
    )(xa, mods, mods, gain, w, q_norm, kv_norm, wq, wkv, cos, sin)


def _mla_kernel(q_ref, k_ref, v_ref, o_ref, vx_ref):
    @pl.when(pl.program_id(2) == 0)
    def _():
        _fill_v_ones(vx_ref, v_ref)

    o0, o1 = _two_stream_attention(q_ref[0, :, :LANES], [(k_ref[0, :, :LANES], None)],
                                   q_ref[0, :, LANES:], [(k_ref[0, :, LANES:], None)], [vx_ref[...]])
    lane = lax.broadcasted_iota(jnp.int32, (1, LANES), 1)
    o_ref[0] = jnp.where(lane < MLA_V_DIM, o0, o1).astype(BF16)


def _mla_attn(q, k, v, n_lat, tq):
    b, t, _ = k.shape
    npairs = MLA_HEADS // 2
    return pl.pallas_call(
        _mla_kernel,
        grid=(b, npairs, n_lat // tq),
        in_specs=[pl.BlockSpec((1, tq, 2 * LANES), lambda i, h, j: (i, j, h)),
                  pl.BlockSpec((1, t, 2 * LANES), lambda i, h, j: (i, 0, h)),
                  pl.BlockSpec((1, t, LANES), lambda i, h, j: (i, 0, h))],
        out_specs=pl.BlockSpec((1, tq, LANES), lambda i, h, j: (i, j, h)),
        out_shape=jax.ShapeDtypeStruct((b, n_lat, npairs * LANES), BF16),
        scratch_shapes=[pltpu.VMEM((t, 2 * LANES), BF16)],
        compiler_params=_cparams("parallel", "parallel", "arbitrary"),
    )(q, k, v)


def _retention_kernel(dl_ref, q_ref, k_ref, v_ref, g_ref, o_ref, f_ref, b_ref, st_ref, *, n_lat):
    c = RET_CHUNK
    t = k_ref.shape[1]
    n_chunks = n_lat // c
    n_ctx = (t - n_lat) // c
    hp = pl.program_id(1)
    lane = lax.broadcasted_iota(jnp.int32, (1, LANES), 1)
    pos_r = lax.broadcasted_iota(jnp.int32, (c, 1), 0).astype(F32)
    ii = lax.broadcasted_iota(jnp.int32, (c, c), 0)
    jj = lax.broadcasted_iota(jnp.int32, (c, c), 1)
    rel = (ii - jj).astype(F32)
    dl = dl_ref[...]
    hsel = lax.broadcasted_iota(jnp.int32, (1, RET_HEADS), 1)

    chains = []
    for hh in range(2):
        hmask = (lane >= hh * RET_QK_DIM) & (lane < (hh + 1) * RET_QK_DIM)
        for direction in range(2):
            logit = jnp.sum(jnp.where(hsel == 2 * hp + hh, dl[direction:direction + 1], 0.0),
                            axis=-1, keepdims=True)
            lg = jnp.minimum(logit, 0.0) - jnp.log(1.0 + jnp.exp(-jnp.abs(logit)))
            if direction == 0:
                mask = rel >= 0
                d_in = jnp.where(mask, jnp.exp(lg * jnp.where(mask, rel, 0.0)), 0.0)
                d_q = jnp.exp(lg * (pos_r + 1.0))
                d_k = jnp.exp(lg * (c - 1.0 - pos_r))
            else:
                mask = rel < 0
                d_in = jnp.where(mask, jnp.exp(lg * jnp.where(mask, -rel, 0.0)), 0.0)
                d_q = jnp.exp(lg * (c - pos_r))
                d_k = jnp.exp(lg * pos_r)
            chains.append(dict(idx=2 * hh + direction, v0=hh * RET_V_DIM, direction=direction, hmask=hmask,
                               d_in=d_in, d_q=d_q, d_k=d_k, d_chunk=jnp.exp(lg * c)))

    st_ref[...] = jnp.zeros(st_ref.shape, F32)

    def kv_update(ch, start):
        kb = k_ref[0, pl.ds(start, c), :].astype(F32)
        vb = v_ref[0, pl.ds(start, c), ch["v0"]:ch["v0"] + RET_V_DIM]
        kd = jnp.where(ch["hmask"], kb * ch["d_k"], 0.0).astype(BF16)
        st_ref[ch["idx"]] = ch["d_chunk"] * st_ref[ch["idx"]] + _dot_tn(kd, vb)

    for i in range(n_ctx):
        for ch in chains:
            ci = i if ch["direction"] == 0 else n_ctx - 1 - i
            kv_update(ch, n_lat + ci * c)

    def finish_chunk(ci):
        start = pl.multiple_of(ci * c, c)
        for hh in range(2):
            v0 = hh * RET_V_DIM
            r = _rms(f_ref[pl.ds(start, c), v0:v0 + RET_V_DIM] + b_ref[pl.ds(start, c), v0:v0 + RET_V_DIM])
            gate = _silu(g_ref[0, pl.ds(start, c), v0:v0 + RET_V_DIM])
            o_ref[0, pl.ds(start, c), v0:v0 + RET_V_DIM] = (r * gate).astype(BF16)

    def lat_step(i, carry, *, finishing):
        starts = [pl.multiple_of((i if ch["direction"] == 0 else n_chunks - 1 - i) * c, c) for ch in chains]
        qms, avs, crs = [], [], []
        for ch, start in zip(chains, starts):
            qb = q_ref[0, pl.ds(start, c), :]
            qm = jnp.where(ch["hmask"], qb, jnp.zeros_like(qb))
            qms.append(qm)
            avs.append(_dot_nt(qm, k_ref[0, pl.ds(start, c), :]))
        for ch, qm in zip(chains, qms):
            qd = (qm.astype(F32) * ch["d_q"]).astype(BF16)
            crs.append(_dot(qd, st_ref[ch["idx"]].astype(BF16)))
        for ch, start, a, cross in zip(chains, starts, avs, crs):
            vb = v_ref[0, pl.ds(start, c), ch["v0"]:ch["v0"] + RET_V_DIM]
            res = _dot((a * ch["d_in"]).astype(BF16), vb) + cross
            dst = f_ref if ch["direction"] == 0 else b_ref
            dst[pl.ds(start, c), ch["v0"]:ch["v0"] + RET_V_DIM] = res
        for ch, start in zip(chains, starts):
            kv_update(ch, start)
        if finishing:
            finish_chunk(i)
            finish_chunk(n_chunks - 1 - i)
        return carry

    assert n_chunks % 2 == 0
    lax.fori_loop(0, n_chunks // 2, functools.partial(lat_step, finishing=False), 0)
    lax.fori_loop(n_chunks // 2, n_chunks, functools.partial(lat_step, finishing=True), 0)


def _retention(rq, rk, rv, rg, decay_logit, n_lat):
    b, t, _ = rq.shape
    npairs = RET_HEADS // 2
    kern = functools.partial(_retention_kernel, n_lat=n_lat)
    return pl.pallas_call(
        kern,
        grid=(b, npairs),
        in_specs=[pl.BlockSpec(decay_logit.shape, lambda i, h: (0, 0)),
                  pl.BlockSpec((1, n_lat, LANES), lambda i, h: (i, 0, h)),
                  pl.BlockSpec((1, t, LANES), lambda i, h: (i, 0, h)),
                  pl.BlockSpec((1, t, 2 * RET_V_DIM), lambda i, h: (i, 0, h)),
                  pl.BlockSpec((1, n_lat, 2 * RET_V_DIM), lambda i, h: (i, 0, h))],
        out_specs=pl.BlockSpec((1, n_lat, 2 * RET_V_DIM), lambda i, h: (i, 0, h)),
        out_shape=jax.ShapeDtypeStruct((b, n_lat, RET_HEADS * RET_V_DIM), BF16),
        scratch_shapes=[pltpu.VMEM((n_lat, 2 * RET_V_DIM), F32), pltpu.VMEM((n_lat, 2 * RET_V_DIM), F32),
                        pltpu.VMEM((4, LANES, RET_V_DIM), F32)],
        compiler_params=_cparams("parallel", "arbitrary"),
    )(decay_logit, rq, rk, rv, rg)


def _rope_tables(n_lat, n_ctx, rot_dim, lane0):
    t = np.arange(n_lat)
    rows = (t // GRID_W).astype(np.float32)
    cols = (t % GRID_W).astype(np.float32)
    m = rot_dim // 4
    freqs = jnp.asarray(ROPE_BASE, F32) ** (-jnp.arange(m, dtype=F32) / m)
    ang_r = jnp.asarray(rows)[:, None] * freqs
    ang_c = jnp.asarray(cols)[:, None] * freqs
    cos = jnp.concatenate([jnp.cos(ang_r)] * 2 + [jnp.cos(ang_c)] * 2, axis=-1)
    sin = jnp.concatenate([-jnp.sin(ang_r), jnp.sin(ang_r), -jnp.sin(ang_c), jnp.sin(ang_c)], axis=-1)
    reps = (LANES - lane0) // rot_dim if lane0 == 0 else 1
    cos = jnp.concatenate([jnp.ones((n_lat, lane0), F32)] + [cos] * reps
                          + [jnp.ones((n_lat, LANES - lane0 - reps * rot_dim), F32)], axis=-1)
    sin = jnp.concatenate([jnp.zeros((n_lat, lane0), F32)] + [sin] * reps
                          + [jnp.zeros((n_lat, LANES - lane0 - reps * rot_dim), F32)], axis=-1)
    cos = jnp.concatenate([cos, jnp.ones((n_ctx, LANES), F32)], axis=0)
    sin = jnp.concatenate([sin, jnp.zeros((n_ctx, LANES), F32)], axis=0)
    return cos, sin


def _rot_partner(w, rot_dim):
    k, n = w.shape
    q = rot_dim // 4
    return w.reshape(k, n // rot_dim, 2, 2, q)[:, :, :, ::-1, :].reshape(k, n)


def _pad_cols(w, lane0, width=LANES):
    k, n = w.shape
    return jnp.concatenate([jnp.zeros((k, lane0), w.dtype), w, jnp.zeros((k, width - lane0 - n), w.dtype)], axis=-1)


def _even_weights(w_in):
    n_rope = 2 * DIFF_HEADS * 2 * DIFF_HEAD_DIM
    dw = DIFF_HEADS * 2 * DIFF_HEAD_DIM
    nw = NA_HEADS * NA_HEAD_DIM
    scale = jnp.concatenate([jnp.full((dw,), DIFF_HEAD_DIM ** -0.5, F32), jnp.ones((2 * dw,), F32),
                             jnp.full((nw,), NA_HEAD_DIM ** -0.5, F32), jnp.ones((2 * nw,), F32)])
    w = w_in * scale
    return jnp.concatenate([w, _rot_partner(w[:, :n_rope], DIFF_HEAD_DIM)], axis=-1).astype(BF16), n_rope


def _odd_weights(w_in, w_uq, w_ukv):
    c0 = MLA_Q_RANK
    c1 = c0 + MLA_KV_RANK
    c2 = c1 + MLA_ROPE_DIM
    nq = RET_HEADS * RET_QK_DIM
    kr = w_in[:, c1:c2]
    w = jnp.concatenate([w_in[:, :c1],
                         _pad_cols(kr, MLA_NOPE_DIM), _pad_cols(_rot_partner(kr, MLA_ROPE_DIM), MLA_NOPE_DIM),
                         w_in[:, c2:c2 + nq], w_in[:, c2 + nq:c2 + 2 * nq] * (RET_QK_DIM ** -0.5),
                         w_in[:, c2 + 2 * nq:]], axis=-1).astype(BF16)
    r = w_uq.shape[0]
    uq = w_uq.reshape(r, MLA_HEADS, MLA_NOPE_DIM + MLA_ROPE_DIM)
    pad = jnp.zeros((r, MLA_HEADS, LANES - MLA_NOPE_DIM - MLA_ROPE_DIM), F32)
    uq_rot = _rot_partner(uq[:, :, MLA_NOPE_DIM:].reshape(r, -1), MLA_ROPE_DIM).reshape(r, MLA_HEADS, MLA_ROPE_DIM)
    wq = jnp.concatenate([uq, pad], axis=-1).reshape(r, -1)
    wq_rot = jnp.concatenate([jnp.zeros_like(uq[:, :, :MLA_NOPE_DIM]), uq_rot, pad], axis=-1).reshape(r, -1)
    rk = w_ukv.shape[0]
    ukv = w_ukv.reshape(rk, MLA_HEADS, MLA_NOPE_DIM + MLA_V_DIM)
    wk = jnp.concatenate([ukv[:, :, :MLA_NOPE_DIM], jnp.zeros((rk, MLA_HEADS, LANES - MLA_NOPE_DIM), F32)],
                         axis=-1).reshape(rk, -1)
    wv = ukv[:, :, MLA_NOPE_DIM:].reshape(rk, -1)
    return w, jnp.concatenate([wq, wq_rot], axis=-1).astype(BF16), jnp.concatenate([wk, wv], axis=-1).astype(BF16)


def _na_bias(rpb, n_lat):
    pat_of_group, idx_r, idx_c, valid = _na_patterns(n_lat // GRID_W)
    g = NA_GROUP_ROWS
    span = g + NA_WIN_ROWS - 1
    n_pat = idx_r.shape[0]
    col_sel = (idx_c[0, :GRID_W, :GRID_W, None] == np.arange(2 * NA_WIN_COLS - 1)).astype(np.float32)
    row_idx = idx_r.reshape(n_pat, g, GRID_W, span, GRID_W)[:, :, 0, :, 0]
    cols = jnp.einsum('hrc,qkc->hrqk', rpb, col_sel, precision=lax.Precision.HIGHEST)
    bias = jnp.take(cols, row_idx.reshape(-1), axis=1).reshape(rpb.shape[0], n_pat, g, span, GRID_W, GRID_W)
    bias = jnp.swapaxes(bias, 3, 4).reshape(rpb.shape[0], n_pat, g * GRID_W, span * GRID_W)
    return jnp.where(valid[None], bias * LOG2E, NEG_BIG), pat_of_group


def _router_weights(w_router):
    return _pad_cols(w_router, 0)


def kernel(x, c, ctx, c_ctx, ada_w, ada_b, norm_mix, norm_ffn, final_norm, even_w_in, even_w_out,
           diff_lambda, diff_subln, na_rpb, odd_w_in, odd_w_out, mla_q_norm, mla_w_uq, mla_kv_norm,
           mla_w_ukv, ret_decay_logit, moe_router, moe_w1, moe_w3, moe_w2):
    b, n_lat, d = x.shape
    n_ctx = ctx.shape[1]
    depth = ada_w.shape[0]
    n_lat_tiles = n_lat // TOK_TILE

    cond = jnp.concatenate([c, c_ctx[None], jnp.zeros((7, d), F32)], axis=0)
    mods_all = _mods(cond, ada_w, ada_b)[:, :b + 1].reshape(depth, b + 1, 6, d)
    xa = jnp.concatenate([x, ctx], axis=1)

    for l in range(depth):
        mods = mods_all[l]
        need_ctx = l < depth - 1
        i = l // 2
        gain_mix = norm_mix[l][None]
        if l % 2 == 0:
            w, n_rope = _even_weights(even_w_in[i])
            cos, sin = _rope_tables(n_lat, n_ctx, DIFF_HEAD_DIM, 0)
            qkv = _proj_even(xa, mods, gain_mix, w, cos, sin, n_lat, n_rope, even_w_in.shape[-1])
            lam_init = 0.8 - 0.6 * math.exp(-0.3 * l)
            mix_a = _diff_attn(qkv, diff_lambda[i], diff_subln[i][None], lam_init, 0, n_lat, ATTN_Q_TILE,
                               0, n_lat + n_ctx)
            mix_a_ctx = _diff_attn(qkv, diff_lambda[i], diff_subln[i][None], lam_init, n_lat, n_ctx, n_ctx,
                                   n_lat, n_ctx)
            bias, offsets = _na_bias(na_rpb[i], n_lat)
            mix_b = _na_attn(qkv, bias, n_lat, offsets, 3 * DIFF_HEADS * 2 * DIFF_HEAD_DIM)
            w_out = even_w_out[i].astype(BF16)
        else:
            w, wq, wkv = _odd_weights(odd_w_in[i], mla_w_uq[i], mla_w_ukv[i])
            cos, sin = _rope_tables(n_lat, n_ctx, MLA_ROPE_DIM, MLA_NOPE_DIM)
            q, k, v, rq, rk, rv, rg = _proj_odd(xa, mods, gain_mix, w, mla_q_norm[i][None], mla_kv_norm[i][None],
                                                wq, wkv, cos, sin, n_lat)
            mix_a = _mla_attn(q, k, v, n_lat, ATTN_Q_TILE)
            mix_a_ctx = mix_a
            mix_b = _retention(rq, rk, rv, rg, ret_decay_logit[i], n_lat)
            w_out = odd_w_out[i].astype(BF16)

        tiles = (n_lat + n_ctx) // TOK_TILE if need_ctx else n_lat_tiles
        x_mid, h2, aff = _proj_out(xa, mix_a, mix_a_ctx, mix_b, w_out, mods, norm_ffn[l][None],
                                   _router_weights(moe_router[l]), tiles, n_lat_tiles)
        segments = ((0, n_lat), (n_lat, n_ctx)) if need_ctx else ((0, n_lat),)
        xa = _moe(x_mid, h2, aff, segments, mods, final_norm[None], n_lat_tiles, l == depth - 1,
                  moe_w1, moe_w3, moe_w2, l)
    return xa
```

```python
import functools
import math

import jax
import jax.numpy as jnp
import numpy as np
from jax import lax
from jax.experimental import pallas as pl
from jax.experimental.pallas import tpu as pltpu

GRID_W = 64
ROPE_BASE = 10000.0
RMS_EPS = 1e-6
DIFF_HEADS = 4
DIFF_HEAD_DIM = 64
NA_HEADS = 8
NA_HEAD_DIM = 64
NA_WIN_ROWS = 8
NA_WIN_COLS = 16
MLA_HEADS = 8
MLA_Q_RANK = 256
MLA_KV_RANK = 128
MLA_NOPE_DIM = 64
MLA_ROPE_DIM = 32
MLA_V_DIM = 64
RET_HEADS = 4
RET_QK_DIM = 64
RET_V_DIM = 128
RET_CHUNK = 128
N_EXPERTS = 16
EC_CAPACITY_FACTOR = 2

LANES = 128
VMEM_LIMIT = 56 * 1024 * 1024
NEG_BIG = -1e30
TOK_TILE = 256
ROW_TILES = 8
ATTN_Q_TILE = 512
FFN_ROW_TILES = 8
FFN_HIDDEN_BLOCK = 512
NA_GROUP_ROWS = 4
NA_GROUPS_PER_STEP = 4
RET_UNROLL = 2
PAIR_CHUNK = 256
LOG2E = math.log2(math.e)

BF16 = jnp.bfloat16
F32 = jnp.float32


def _cparams(*sem):
    return pltpu.CompilerParams(dimension_semantics=sem, vmem_limit_bytes=VMEM_LIMIT)


def _dot(a, b):
    return jnp.dot(a, b, preferred_element_type=F32)


def _dot_nt(a, b):
    return lax.dot_general(a, b, (((1,), (1,)), ((), ())), preferred_element_type=F32)


def _dot_tn(a, b):
    return lax.dot_general(a, b, (((0,), (0,)), ((), ())), preferred_element_type=F32)


def _rms(x):
    return x * lax.rsqrt(jnp.mean(x * x, axis=-1, keepdims=True) + RMS_EPS)


def _silu(x):
    return x * (1.0 / (1.0 + jnp.exp(-x)))


def _tile_lanes(t, n):
    return jnp.concatenate([t] * n, axis=-1)


def _mods_kernel(c_ref, w_ref, b_ref, o_ref):
    o_ref[0] = _dot(_silu(c_ref[...]), w_ref[0]) + b_ref[0]


def _mods(cond, ada_w, ada_b):
    depth, d, n = ada_w.shape
    r = cond.shape[0]
    tn = 1536
    return pl.pallas_call(
        _mods_kernel,
        grid=(depth, n // tn),
        in_specs=[pl.BlockSpec((r, d), lambda l, j: (0, 0)),
                  pl.BlockSpec((1, d, tn), lambda l, j: (l, 0, j)),
                  pl.BlockSpec((1, 1, tn), lambda l, j: (l, 0, j))],
        out_specs=pl.BlockSpec((1, r, tn), lambda l, j: (l, 0, j)),
        out_shape=jax.ShapeDtypeStruct((depth, r, n), F32),
        compiler_params=_cparams("arbitrary", "arbitrary"),
    )(cond, ada_w, ada_b.reshape(depth, 1, n))


def _normed(x, gain, mods, shift_row, scale_row):
    return _rms(x) * gain * (1.0 + mods[scale_row:scale_row + 1]) + mods[shift_row:shift_row + 1]


def _mixer_input(x, gain, mods_lat, mods_ctx, n_lat):
    tm = x.shape[0]
    rows = pl.program_id(1) * tm + lax.broadcasted_iota(jnp.int32, (tm, 1), 0)
    is_ctx = rows >= n_lat
    scale = jnp.where(is_ctx, mods_ctx[1:2], mods_lat[1:2])
    shift = jnp.where(is_ctx, mods_ctx[0:1], mods_lat[0:1])
    return _rms(x) * gain * (1.0 + scale) + shift


def _proj_even_kernel(x_ref, ml_ref, mc_ref, g_ref, w_ref, cos_ref, sin_ref, o_ref, *, n_rope, n_out, n_lat):
    na_q0 = n_rope + n_rope // 2
    na_q1 = na_q0 + NA_HEADS * NA_HEAD_DIM
    h = _mixer_input(x_ref[0], g_ref[...], ml_ref[0], mc_ref[0], n_lat)
    acc = _dot(h.astype(BF16), w_ref[...])
    reps = n_rope // LANES
    cos = _tile_lanes(cos_ref[...], reps)
    sin = _tile_lanes(sin_ref[...], reps)
    roped = acc[:, :n_rope] * cos + acc[:, n_out:] * sin
    n_q = n_rope // 2
    o_ref[0, :, :n_q] = (roped[:, :n_q] * LOG2E).astype(BF16)
    o_ref[0, :, n_q:n_rope] = roped[:, n_q:].astype(BF16)
    o_ref[0, :, n_rope:na_q0] = acc[:, n_rope:na_q0].astype(BF16)
    o_ref[0, :, na_q0:na_q1] = (acc[:, na_q0:na_q1] * LOG2E).astype(BF16)
    o_ref[0, :, na_q1:] = acc[:, na_q1:n_out].astype(BF16)


def _row_tile(t):
    tm = t // ROW_TILES
    assert tm * ROW_TILES == t and tm % 16 == 0
    return tm


def _proj_even(xa, mods, gain, w, cos, sin, n_lat, n_rope, n_out):
    b, t, d = xa.shape
    nb = mods.shape[0] - 1
    tm = _row_tile(t)
    kern = functools.partial(_proj_even_kernel, n_rope=n_rope, n_out=n_out, n_lat=n_lat)
    return pl.pallas_call(
        kern,
        grid=(b, t // tm),
        in_specs=[pl.BlockSpec((1, tm, d), lambda i, j: (i, j, 0)),
                  pl.BlockSpec((1, 6, d), lambda i, j: (i, 0, 0)),
                  pl.BlockSpec((1, 6, d), lambda i, j: (nb, 0, 0)),
                  pl.BlockSpec((1, d), lambda i, j: (0, 0)),
                  pl.BlockSpec(w.shape, lambda i, j: (0, 0)),
                  pl.BlockSpec((tm, LANES), lambda i, j: (j, 0)),
                  pl.BlockSpec((tm, LANES), lambda i, j: (j, 0))],
        out_specs=pl.BlockSpec((1, tm, n_out), lambda i, j: (i, j, 0)),
        out_shape=jax.ShapeDtypeStruct((b, t, n_out), BF16),
        compiler_params=_cparams("parallel", "arbitrary"),
    )(xa, mods, mods, gain, w, cos, sin)


def _fill_v_ones(vx_ref, v_ref):
    vx_ref[:, :LANES] = v_ref[0]
    vx_ref[:, LANES:] = jnp.ones((vx_ref.shape[0], LANES), BF16)


def _weights(s, m):
    return jnp.exp2(s - m).astype(BF16)


def _rowmax(*ss):
    m = jnp.max(ss[0], axis=-1, keepdims=True)
    for s in ss[1:]:
        m = jnp.maximum(m, jnp.max(s, axis=-1, keepdims=True))
    return m


def _normalised(ox):
    return ox[:, :LANES] / ox[:, LANES:LANES + 1]


def _scores(q, segs):
    return [_dot_nt(q, k) if bias is None else _dot_nt(q, k) + bias for k, bias in segs]


def _pv(ps, vxs):
    o = _dot(ps[0], vxs[0])
    for p, vx in zip(ps[1:], vxs[1:]):
        o = o + _dot(p, vx)
    return o


def _multi_stream_attention(streams):
    n = len(streams)
    s, m, p, o = [None] * n, [None] * n, [None] * n, [None] * n
    for t in range(n + 2):
        if t < n:
            s[t] = _scores(streams[t][0], streams[t][1])
        if 0 <= t - 2 < n:
            o[t - 2] = _normalised(_pv(p[t - 2], streams[t - 2][2]))
        if 0 <= t - 1 < n:
            p[t - 1] = [_weights(x, m[t - 1]) for x in s[t - 1]]
        if t < n:
            m[t] = _rowmax(*s[t])
    return o


def _two_stream_attention(qa, segs_a, qb, segs_b, vxs):
    return _multi_stream_attention([(qa, segs_a, vxs), (qb, segs_b, vxs)])


def _diff_attn_kernel(lam_ref, g_ref, q_ref, k_ref, v_ref, o_ref, vx_ref, *, lam_init):
    @pl.when(pl.program_id(2) == 0)
    def _():
        _fill_v_ones(vx_ref, v_ref)

    lp = lam_ref[...]
    s1 = jnp.sum(lp[0:1] * lp[1:2], axis=-1, keepdims=True)
    s2 = jnp.sum(lp[2:3] * lp[3:4], axis=-1, keepdims=True)
    lam = jnp.exp(s1) - jnp.exp(s2) + lam_init
    q = q_ref[0]
    lane = lax.broadcasted_iota(jnp.int32, (1, LANES), 1)
    q1 = jnp.where(lane < DIFF_HEAD_DIM, q, jnp.zeros_like(q))
    q2 = jnp.where(lane >= DIFF_HEAD_DIM, q, jnp.zeros_like(q))

    k = k_ref[0]
    o1, o2 = _two_stream_attention(q1, [(k, None)], q2, [(k, None)], [vx_ref[...]])
    o = o1 - lam * o2
    o_ref[0] = (_rms(o) * g_ref[...] * (1.0 - lam_init)).astype(BF16)


def _diff_attn(qkv, lam_params, subln, lam_init, q_row0, n_q, tq, k_row0, n_k):
    b = qkv.shape[0]
    nh = DIFF_HEADS
    q0, k0 = q_row0 // tq, k_row0 // n_k
    assert q0 * tq == q_row0 and k0 * n_k == k_row0 and n_q % tq == 0
    return pl.pallas_call(
        functools.partial(_diff_attn_kernel, lam_init=lam_init),
        grid=(b, nh, n_q // tq),
        in_specs=[pl.BlockSpec(lam_params.shape, lambda i, h, j: (0, 0)),
                  pl.BlockSpec((1, LANES), lambda i, h, j: (0, 0)),
                  pl.BlockSpec((1, tq, LANES), lambda i, h, j: (i, q0 + j, h)),
                  pl.BlockSpec((1, n_k, LANES), lambda i, h, j: (i, k0, nh + h)),
                  pl.BlockSpec((1, n_k, LANES), lambda i, h, j: (i, k0, 2 * nh + h))],
        out_specs=pl.BlockSpec((1, tq, LANES), lambda i, h, j: (i, j, h)),
        out_shape=jax.ShapeDtypeStruct((b, n_q, nh * LANES), BF16),
        scratch_shapes=[pltpu.VMEM((n_k, 2 * LANES), BF16)],
        compiler_params=_cparams("parallel", "parallel", "arbitrary"),
    )(lam_params, subln, qkv, qkv, qkv)


def _na_patterns(rows):
    g = NA_GROUP_ROWS
    span = g + NA_WIN_ROWS - 1
    nq, nk = g * GRID_W, span * GRID_W
    qi = np.arange(nq)[:, None]
    ki = np.arange(nk)[None, :]
    qcol, kcol = qi % GRID_W, ki % GRID_W
    cs = np.clip(qcol - NA_WIN_COLS // 2, 0, GRID_W - NA_WIN_COLS)
    col_ok = (kcol >= cs) & (kcol < cs + NA_WIN_COLS)
    idx_c = np.clip(kcol - qcol + NA_WIN_COLS - 1, 0, 2 * NA_WIN_COLS - 2) + 0 * qi
    seen, pat_of_group, idx_r, valid = {}, [], [], []
    for r0 in range(0, rows, g):
        ks = min(max(r0 - NA_WIN_ROWS // 2, 0), rows - span)
        qrow = r0 + qi // GRID_W
        krow = ks + ki // GRID_W
        rs = np.clip(qrow - NA_WIN_ROWS // 2, 0, rows - NA_WIN_ROWS)
        ok = (krow >= rs) & (krow < rs + NA_WIN_ROWS) & col_ok
        ir = np.clip(krow - qrow + NA_WIN_ROWS - 1, 0, 2 * NA_WIN_ROWS - 2) + 0 * kcol
        key = (ok.tobytes(), np.where(ok, ir, 0).tobytes())
        if key not in seen:
            seen[key] = len(idx_r)
            idx_r.append(ir)
            valid.append(ok)
        pat_of_group.append(seen[key])
    n_pat = len(idx_r)
    return tuple(pat_of_group), np.stack(idx_r), np.stack([idx_c] * n_pat), np.stack(valid)


def _na_kernel(q_ref, k_ref, v_ref, bias_ref, o_ref, vx_ref, *, n_lat, pat_of_group):
    g_rows = NA_GROUP_ROWS
    span = g_rows + NA_WIN_ROWS - 1
    rows = n_lat // GRID_W
    nq, nk = g_rows * GRID_W, span * GRID_W
    lane = lax.broadcasted_iota(jnp.int32, (1, LANES), 1)
    head_masks = (lane < NA_HEAD_DIM, lane >= NA_HEAD_DIM)
    _fill_v_ones(vx_ref, v_ref)
    kc = k_ref[0, n_lat:, :]
    vxc = vx_ref[n_lat:, :]

    def both_heads(blocks):
        streams = []
        for q, segs, vxs in blocks:
            for hh, hm in enumerate(head_masks):
                streams.append((jnp.where(hm, q, jnp.zeros_like(q)),
                                [(k, None if bi is None else bias_ref[hh, bi]) for k, bi in segs], vxs))
        outs = _multi_stream_attention(streams)
        return [jnp.where(head_masks[0], outs[2 * i], outs[2 * i + 1]).astype(BF16) for i in range(len(blocks))]

    def group_block(g):
        r0 = g * g_rows
        ks = jnp.clip(r0 - NA_WIN_ROWS // 2, 0, rows - span)
        common = max(set(pat_of_group), key=pat_of_group.count)
        pat = jnp.int32(common)
        for gi, p in enumerate(pat_of_group):
            if p != common:
                pat = jnp.where(g == gi, p, pat)
        qg = q_ref[0, pl.ds(pl.multiple_of(g * nq, nq), nq), :]
        kw = k_ref[0, pl.ds(pl.multiple_of(ks * GRID_W, GRID_W), nk), :]
        vxw = vx_ref[pl.ds(pl.multiple_of(ks * GRID_W, GRID_W), nk), :]
        return qg, [(kw, pat), (kc, None)], [vxw, vxc]

    def group_step(i, carry):
        gs = [i * NA_GROUPS_PER_STEP + u for u in range(NA_GROUPS_PER_STEP)]
        for g, out in zip(gs, both_heads([group_block(g) for g in gs])):
            o_ref[0, pl.ds(pl.multiple_of(g * nq, nq), nq), :] = out
        return carry

    lax.fori_loop(0, rows // g_rows // NA_GROUPS_PER_STEP, group_step, 0)

    o_ref[0, n_lat:, :] = both_heads([(q_ref[0, n_lat:, :], [(kc, None)], [vxc])])[0]


def _na_attn(qkv, bias, n_lat, pat_of_group, col0):
    b, t, _ = qkv.shape
    npairs = NA_HEADS // 2
    cb = col0 // LANES
    kern = functools.partial(_na_kernel, n_lat=n_lat, pat_of_group=pat_of_group)
    return pl.pallas_call(
        kern,
        grid=(b, npairs),
        in_specs=[pl.BlockSpec((1, t, LANES), lambda i, h: (i, 0, cb + h)),
                  pl.BlockSpec((1, t, LANES), lambda i, h: (i, 0, cb + npairs + h)),
                  pl.BlockSpec((1, t, LANES), lambda i, h: (i, 0, cb + 2 * npairs + h)),
                  pl.BlockSpec((2,) + bias.shape[1:], lambda i, h: (h, 0, 0, 0))],
        out_specs=pl.BlockSpec((1, t, LANES), lambda i, h: (i, 0, h)),
        out_shape=jax.ShapeDtypeStruct((b, t, npairs * LANES), BF16),
        scratch_shapes=[pltpu.VMEM((t, 2 * LANES), BF16)],
        compiler_params=_cparams("parallel", "arbitrary"),
    )(qkv, qkv, qkv, bias)


def _proj_out_kernel(x_ref, a_ref, ac_ref, b_ref, wa_ref, wb_ref, m_ref, g_ref, wr_ref, xo_ref, h_ref, aff_ref,
                     *, n_lat_tiles):
    mods = m_ref[0]
    a = jnp.where(pl.program_id(1) >= n_lat_tiles, ac_ref[0], a_ref[0])
    y = _dot(a, wa_ref[...]) + _dot(b_ref[0], wb_ref[...])
    x = x_ref[0] + mods[2:3] * y
    xo_ref[0] = x
    h2 = _normed(x, g_ref[...], mods, 3, 4)
    h_hi = h2.astype(BF16)
    h_ref[0] = h_hi
    h_lo = (h2 - h_hi.astype(F32)).astype(BF16)
    wr = wr_ref[...]
    w_hi = wr.astype(BF16)
    w_lo = (wr - w_hi.astype(F32)).astype(BF16)
    hh = _dot(h_hi, jnp.concatenate([w_hi, w_lo], axis=-1))
    logits = hh[:, :LANES] + (hh[:, LANES:] + _dot(h_lo, w_hi))
    lane = lax.broadcasted_iota(jnp.int32, logits.shape, 1)
    logits = jnp.where(lane < N_EXPERTS, logits, NEG_BIG)
    e = jnp.exp(logits - jnp.max(logits, axis=-1, keepdims=True))
    aff_ref[0] = e / jnp.sum(e, axis=-1, keepdims=True)


def _proj_out(xa, a, a_ctx, bb, w_out, mods, gain, w_router, n_tiles, n_lat_tiles):
    b, t, d = xa.shape
    nb = mods.shape[0] - 1
    tm = TOK_TILE
    wa, wb = w_out[:a.shape[-1]], w_out[a.shape[-1]:]
    tok = lambda i, j: (i, j, 0)
    const = lambda i, j: (0, 0)
    return pl.pallas_call(
        functools.partial(_proj_out_kernel, n_lat_tiles=n_lat_tiles),
        grid=(b, n_tiles),
        in_specs=[pl.BlockSpec((1, tm, d), tok),
                  pl.BlockSpec((1, tm, a.shape[-1]), lambda i, j: (i, jnp.minimum(j, n_lat_tiles - 1), 0)),
                  pl.BlockSpec((1, tm, a.shape[-1]), lambda i, j: (i, jnp.maximum(j - n_lat_tiles, 0), 0)),
                  pl.BlockSpec((1, tm, bb.shape[-1]), tok),
                  pl.BlockSpec(wa.shape, const),
                  pl.BlockSpec(wb.shape, const),
                  pl.BlockSpec((1, 6, d), lambda i, j: (jnp.where(j >= n_lat_tiles, nb, i), 0, 0)),
                  pl.BlockSpec((1, d), const),
                  pl.BlockSpec(w_router.shape, const)],
        out_specs=[pl.BlockSpec((1, tm, d), tok),
                   pl.BlockSpec((1, tm, d), tok),
                   pl.BlockSpec((1, tm, LANES), tok)],
        out_shape=[jax.ShapeDtypeStruct((b, n_tiles * tm, d), F32),
                   jax.ShapeDtypeStruct((b, n_tiles * tm, d), BF16),
                   jax.ShapeDtypeStruct((b, n_tiles * tm, LANES), F32)],
        compiler_params=_cparams("parallel", "arbitrary"),
    )(xa, a, a_ctx, bb, wa, wb, mods, gain, w_router)


def _ffn_kernel(x_ref, gate_ref, w1_ref, w3_ref, w2_ref, o_ref, w1b, w3b, w2b):
    @pl.when(pl.program_id(1) == 0)
    def _():
        w1b[...] = w1_ref[0].astype(BF16)
        w3b[...] = w3_ref[0].astype(BF16)
        w2b[...] = w2_ref[0].astype(BF16)

    x = x_ref[0]
    f = w1b.shape[1]
    y = None
    for f0 in range(0, f, FFN_HIDDEN_BLOCK):
        f1 = f0 + FFN_HIDDEN_BLOCK
        hid = _silu(_dot(x, w1b[:, f0:f1])) * _dot(x, w3b[:, f0:f1])
        part = _dot(hid.astype(BF16), w2b[f0:f1, :])
        y = part if y is None else y + part
    o_ref[0] = (y * gate_ref[0]).astype(BF16)


def _expert_ffn(xe, gate, w1, w3, w2, layer):
    e, m, d = xe.shape
    f = w1.shape[-1]
    tm = m // FFN_ROW_TILES
    assert tm * FFN_ROW_TILES == m and tm % 16 == 0
    wspec = lambda r, c: pl.BlockSpec((None, 1, r, c), lambda i, j: (layer, i, 0, 0))
    return pl.pallas_call(
        _ffn_kernel,
        grid=(e, m // tm),
        in_specs=[pl.BlockSpec((1, tm, d), lambda i, j: (i, j, 0)),
                  pl.BlockSpec((1, tm, 1), lambda i, j: (i, j, 0)),
                  wspec(d, f), wspec(d, f), wspec(f, d)],
        out_specs=pl.BlockSpec((1, tm, d), lambda i, j: (i, j, 0)),
        out_shape=jax.ShapeDtypeStruct((e, m, d), BF16),
        scratch_shapes=[pltpu.VMEM((d, f), BF16), pltpu.VMEM((d, f), BF16), pltpu.VMEM((f, d), BF16)],
        compiler_params=_cparams("parallel", "arbitrary"),
    )(xe, gate, w1, w3, w2)


def _route(aff, segments):
    b, r, _ = aff.shape
    e = N_EXPERTS
    gates, flats, toks = [], [], []
    for row0, n in segments:
        cap = (EC_CAPACITY_FACTOR * n) // e
        gate, idx = lax.top_k(jnp.swapaxes(aff[:, row0:row0 + n, :e], 1, 2), cap)
        tok = idx + row0
        gates.append(jnp.swapaxes(gate, 0, 1).reshape(e, b * cap))
        flats.append(jnp.swapaxes(tok + jnp.arange(b, dtype=idx.dtype)[:, None, None] * r, 0, 1).reshape(e, b * cap))
        toks.append(tok)
    return jnp.concatenate(gates, axis=1)[..., None], jnp.concatenate(flats, axis=1), toks


def _combine_kernel(lo_ref, x_ref, tok_ref, y_ref, m_ref, g_ref, o_ref, *, final):
    i, j = pl.program_id(0), pl.program_id(1)
    tm = x_ref.shape[1]
    n_tiles = pl.num_programs(1)
    lo = lo_ref[i * (n_tiles + 1) + j]
    hi = lo_ref[i * (n_tiles + 1) + j + 1]
    rows = j * tm + lax.broadcasted_iota(jnp.int32, (tm, 1), 0)

    def chunk_sum(c):
        sel = (tok_ref[0, pl.ds(c, 1), :] == rows).astype(BF16)
        return _dot(sel, y_ref[0, pl.ds(pl.multiple_of(c * PAIR_CHUNK, PAIR_CHUNK), PAIR_CHUNK), :])

    def chunk(c, carry):
        o_ref[0] += chunk_sum(c)
        return carry

    c0 = jnp.minimum(lo // PAIR_CHUNK, tok_ref.shape[1] - 1)
    o_ref[0] = chunk_sum(c0)
    lax.fori_loop(c0 + 1, (hi + PAIR_CHUNK - 1) // PAIR_CHUNK, chunk, 0)
    x = x_ref[0] + m_ref[0][5:6] * o_ref[0]
    if final:
        x = _rms(x) * g_ref[...]
    o_ref[0] = x


def _combine(x, tok_sorted, y_sorted, lo, mods, gain, n_lat_tiles, final):
    b, t, d = x.shape
    p = tok_sorted.shape[1]
    nb = mods.shape[0] - 1
    tm = TOK_TILE
    tok = lambda i, j, lo_ref: (i, j, 0)
    whole = lambda i, j, lo_ref: (i, 0, 0)
    grid_spec = pltpu.PrefetchScalarGridSpec(
        num_scalar_prefetch=1,
        grid=(b, t // tm),
        in_specs=[pl.BlockSpec((1, tm, d), tok),
                  pl.BlockSpec((1, p // PAIR_CHUNK, PAIR_CHUNK), whole),
                  pl.BlockSpec((1, p, d), whole),
                  pl.BlockSpec((1, 6, d), lambda i, j, lo_ref: (jnp.where(j >= n_lat_tiles, nb, i), 0, 0)),
                  pl.BlockSpec((1, d), lambda i, j, lo_ref: (0, 0))],
        out_specs=pl.BlockSpec((1, tm, d), tok))
    return pl.pallas_call(
        functools.partial(_combine_kernel, final=final),
        grid_spec=grid_spec,
        out_shape=jax.ShapeDtypeStruct((b, t, d), F32),
        compiler_params=_cparams("parallel", "arbitrary"),
    )(lo, x, tok_sorted.reshape(b, p // PAIR_CHUNK, PAIR_CHUNK), y_sorted, mods, gain)


def _moe(x_mid, h2, aff, segments, mods, gain, n_lat_tiles, final, w1, w3, w2, layer):
    b, r, d = h2.shape
    gate, flat, toks = _route(aff, segments)
    e, m = flat.shape
    xe = h2.reshape(b * r, d).at[flat].get(mode="promise_in_bounds")
    ye = _expert_ffn(xe, gate, w1, w3, w2, layer)
    srcs, off = [], 0
    for tok in toks:
        cap = tok.shape[-1]
        src = (jnp.arange(e, dtype=jnp.int32)[None, :, None] * m + off
               + jnp.arange(b, dtype=jnp.int32)[:, None, None] * cap + jnp.arange(cap, dtype=jnp.int32))
        srcs.append(src.reshape(b, e * cap))
        off += b * cap
    tok_all = jnp.concatenate([tok.reshape(b, -1) for tok in toks], axis=1)
    src_bits = max(1, (e * m - 1).bit_length())
    assert (r - 1).bit_length() + src_bits <= 31
    packed = jnp.sort(tok_all * (1 << src_bits) + jnp.concatenate(srcs, axis=1), axis=1)
    tok_sorted, src_sorted = packed >> src_bits, packed & ((1 << src_bits) - 1)
    y_sorted = ye.reshape(e * m, d).at[src_sorted].get(mode="promise_in_bounds")
    starts = jnp.arange(r // TOK_TILE + 1, dtype=jnp.int32) * TOK_TILE
    lo = jnp.sum((tok_sorted[:, None, :] < starts[None, :, None]).astype(jnp.int32), axis=-1)
    return _combine(x_mid, tok_sorted, y_sorted, lo.reshape(-1), mods, gain, n_lat_tiles, final)


def _proj_odd_kernel(x_ref, ml_ref, mc_ref, g_ref, w_ref, qn_ref, kvn_ref, wq_ref, wkv_ref, cos_ref, sin_ref,
                     q_ref, k_ref, v_ref, rq_ref, rk_ref, rv_ref, rg_ref, *, q_scale, n_lat):
    h = _mixer_input(x_ref[0], g_ref[...], ml_ref[0], mc_ref[0], n_lat)
    acc = _dot(h.astype(BF16), w_ref[...])
    cos, sin = cos_ref[...], sin_ref[...]
    c0 = MLA_Q_RANK
    c1 = c0 + MLA_KV_RANK
    cq = acc[:, :c0]
    ckv = acc[:, c0:c1]
    kr = acc[:, c1:c1 + LANES] * cos + acc[:, c1 + LANES:c1 + 2 * LANES] * sin
    c2 = c1 + 2 * LANES
    nq = RET_HEADS * RET_QK_DIM
    nv = RET_HEADS * RET_V_DIM
    rq_ref[0] = acc[:, c2:c2 + nq].astype(BF16)
    rk_ref[0] = acc[:, c2 + nq:c2 + 2 * nq].astype(BF16)
    rv_ref[0] = acc[:, c2 + 2 * nq:c2 + 2 * nq + nv].astype(BF16)
    rg_ref[0] = acc[:, c2 + 2 * nq + nv:]

    nqk = MLA_HEADS * LANES
    qq = _dot((_rms(cq) * qn_ref[...]).astype(BF16), wq_ref[...])
    q = qq[:, :nqk] * _tile_lanes(cos, MLA_HEADS) + qq[:, nqk:] * _tile_lanes(sin, MLA_HEADS)
    q_ref[0] = (q * q_scale).astype(BF16)
    kv = _dot((_rms(ckv) * kvn_ref[...]).astype(BF16), wkv_ref[...])
    k_ref[0] = (kv[:, :nqk] + _tile_lanes(kr, MLA_HEADS)).astype(BF16)
    v_ref[0] = kv[:, nqk:].astype(BF16)


def _proj_odd(xa, mods, gain, w, q_norm, kv_norm, wq, wkv, cos, sin, n_lat):
    b, t, d = xa.shape
    nb = mods.shape[0] - 1
    tm = _row_tile(t)
    tok = lambda i, j: (i, j, 0)
    const = lambda i, j: (0, 0)
    widths = (MLA_HEADS * LANES, MLA_HEADS * LANES, MLA_HEADS * MLA_V_DIM,
              RET_HEADS * RET_QK_DIM, RET_HEADS * RET_QK_DIM, RET_HEADS * RET_V_DIM, RET_HEADS * RET_V_DIM)
    dtypes = (BF16,) * 6 + (F32,)
    kern = functools.partial(_proj_odd_kernel, q_scale=float((MLA_NOPE_DIM + MLA_ROPE_DIM) ** -0.5) * LOG2E,
                             n_lat=n_lat)
    return pl.pallas_call(
        kern,
        grid=(b, t // tm),
        in_specs=[pl.BlockSpec((1, tm, d), tok),
                  pl.BlockSpec((1, 6, d), lambda i, j: (i, 0, 0)),
                  pl.BlockSpec((1, 6, d), lambda i, j: (nb, 0, 0)),
                  pl.BlockSpec((1, d), const),
                  pl.BlockSpec(w.shape, const),
                  pl.BlockSpec(q_norm.shape, const),
                  pl.BlockSpec(kv_norm.shape, const),
                  pl.BlockSpec(wq.shape, const),
                  pl.BlockSpec(wkv.shape, const),
                  pl.BlockSpec((tm, LANES), lambda i, j: (j, 0)),
                  pl.BlockSpec((tm, LANES), lambda i, j: (j, 0))],
        out_specs=[pl.BlockSpec((1, tm, wd), tok) for wd in widths],
        out_shape=[jax.ShapeDtypeStruct((b, t, wd), dt) for wd, dt in zip(widths, dtypes)],
        compiler_params=_cparams("parallel", "arbitrary"),
    )(xa, mods, mods, gain, w, q_norm, kv_norm, wq, wkv, cos, sin)


def _mla_kernel(q_ref, k_ref, v_ref, o_ref, vx_ref):
    @pl.when(pl.program_id(2) == 0)
    def _():
        _fill_v_ones(vx_ref, v_ref)

    o0, o1 = _two_stream_attention(q_ref[0, :, :LANES], [(k_ref[0, :, :LANES], None)],
                                   q_ref[0, :, LANES:], [(k_ref[0, :, LANES:], None)], [vx_ref[...]])
    lane = lax.broadcasted_iota(jnp.int32, (1, LANES), 1)
    o_ref[0] = jnp.where(lane < MLA_V_DIM, o0, o1).astype(BF16)


def _mla_attn(q, k, v, n_lat, tq):
    b, t, _ = k.shape
    npairs = MLA_HEADS // 2
    return pl.pallas_call(
        _mla_kernel,
        grid=(b, npairs, n_lat // tq),
        in_specs=[pl.BlockSpec((1, tq, 2 * LANES), lambda i, h, j: (i, j, h)),
                  pl.BlockSpec((1, t, 2 * LANES), lambda i, h, j: (i, 0, h)),
                  pl.BlockSpec((1, t, LANES), lambda i, h, j: (i, 0, h))],
        out_specs=pl.BlockSpec((1, tq, LANES), lambda i, h, j: (i, j, h)),
        out_shape=jax.ShapeDtypeStruct((b, n_lat, npairs * LANES), BF16),
        scratch_shapes=[pltpu.VMEM((t, 2 * LANES), BF16)],
        compiler_params=_cparams("parallel", "parallel", "arbitrary"),
    )(q, k, v)


def _retention_kernel(dl_ref, q_ref, k_ref, v_ref, g_ref, o_ref, f_ref, b_ref, st_ref, *, n_lat):
    c = RET_CHUNK
    t = k_ref.shape[1]
    n_chunks = n_lat // c
    n_ctx = (t - n_lat) // c
    hp = pl.program_id(1)
    lane = lax.broadcasted_iota(jnp.int32, (1, LANES), 1)
    pos_r = lax.broadcasted_iota(jnp.int32, (c, 1), 0).astype(F32)
    ii = lax.broadcasted_iota(jnp.int32, (c, c), 0)
    jj = lax.broadcasted_iota(jnp.int32, (c, c), 1)
    rel = (ii - jj).astype(F32)
    dl = dl_ref[...]
    hsel = lax.broadcasted_iota(jnp.int32, (1, RET_HEADS), 1)

    chains = []
    for hh in range(2):
        hmask = (lane >= hh * RET_QK_DIM) & (lane < (hh + 1) * RET_QK_DIM)
        for direction in range(2):
            logit = jnp.sum(jnp.where(hsel == 2 * hp + hh, dl[direction:direction + 1], 0.0),
                            axis=-1, keepdims=True)
            lg = jnp.minimum(logit, 0.0) - jnp.log(1.0 + jnp.exp(-jnp.abs(logit)))
            if direction == 0:
                mask = rel >= 0
                d_in = jnp.where(mask, jnp.exp(lg * jnp.where(mask, rel, 0.0)), 0.0)
                d_q = jnp.exp(lg * (pos_r + 1.0))
                d_k = jnp.exp(lg * (c - 1.0 - pos_r))
            else:
                mask = rel < 0
                d_in = jnp.where(mask, jnp.exp(lg * jnp.where(mask, -rel, 0.0)), 0.0)
                d_q = jnp.exp(lg * (c - pos_r))
                d_k = jnp.exp(lg * pos_r)
            chains.append(dict(idx=2 * hh + direction, v0=hh * RET_V_DIM, direction=direction, hmask=hmask,
                               d_in=d_in, d_q=d_q, d_k=d_k, d_chunk=jnp.exp(lg * c)))

    st_ref[...] = jnp.zeros(st_ref.shape, F32)

    def kv_update(ch, start):
        kb = k_ref[0, pl.ds(start, c), :].astype(F32)
        vb = v_ref[0, pl.ds(start, c), ch["v0"]:ch["v0"] + RET_V_DIM]
        kd = jnp.where(ch["hmask"], kb * ch["d_k"], 0.0).astype(BF16)
        st_ref[ch["idx"]] = ch["d_chunk"] * st_ref[ch["idx"]] + _dot_tn(kd, vb)

    for i in range(n_ctx):
        for ch in chains:
            ci = i if ch["direction"] == 0 else n_ctx - 1 - i
            kv_update(ch, n_lat + ci * c)

    def finish_chunk(ci):
        start = pl.multiple_of(ci * c, c)
        for hh in range(2):
            v0 = hh * RET_V_DIM
            r = _rms(f_ref[pl.ds(start, c), v0:v0 + RET_V_DIM] + b_ref[pl.ds(start, c), v0:v0 + RET_V_DIM])
            gate = _silu(g_ref[0, pl.ds(start, c), v0:v0 + RET_V_DIM])
            o_ref[0, pl.ds(start, c), v0:v0 + RET_V_DIM] = (r * gate).astype(BF16)

    def lat_step(i, carry, *, finishing):
        starts = [pl.multiple_of((i if ch["direction"] == 0 else n_chunks - 1 - i) * c, c) for ch in chains]
        qms, avs, crs = [], [], []
        for ch, start in zip(chains, starts):
            qb = q_ref[0, pl.ds(start, c), :]
            qm = jnp.where(ch["hmask"], qb, jnp.zeros_like(qb))
            qms.append(qm)
            avs.append(_dot_nt(qm, k_ref[0, pl.ds(start, c), :]))
        for ch, qm in zip(chains, qms):
            qd = (qm.astype(F32) * ch["d_q"]).astype(BF16)
            crs.append(_dot(qd, st_ref[ch["idx"]].astype(BF16)))
        for ch, start, a, cross in zip(chains, starts, avs, crs):
            vb = v_ref[0, pl.ds(start, c), ch["v0"]:ch["v0"] + RET_V_DIM]
            res = _dot((a * ch["d_in"]).astype(BF16), vb) + cross
            dst = f_ref if ch["direction"] == 0 else b_ref
            dst[pl.ds(start, c), ch["v0"]:ch["v0"] + RET_V_DIM] = res
        for ch, start in zip(chains, starts):
            kv_update(ch, start)
        if finishing:
            finish_chunk(i)
            finish_chunk(n_chunks - 1 - i)
        return carry

    assert n_chunks % 2 == 0
    unroll = RET_UNROLL if (n_chunks // 2) % RET_UNROLL == 0 else 1
    lax.fori_loop(0, n_chunks // 2, functools.partial(lat_step, finishing=False), 0, unroll=unroll)
    lax.fori_loop(n_chunks // 2, n_chunks, functools.partial(lat_step, finishing=True), 0, unroll=unroll)


def _retention(rq, rk, rv, rg, decay_logit, n_lat):
    b, t, _ = rq.shape
    npairs = RET_HEADS // 2
    kern = functools.partial(_retention_kernel, n_lat=n_lat)
    return pl.pallas_call(
        kern,
        grid=(b, npairs),
        in_specs=[pl.BlockSpec(decay_logit.shape, lambda i, h: (0, 0)),
                  pl.BlockSpec((1, n_lat, LANES), lambda i, h: (i, 0, h)),
                  pl.BlockSpec((1, t, LANES), lambda i, h: (i, 0, h)),
                  pl.BlockSpec((1, t, 2 * RET_V_DIM), lambda i, h: (i, 0, h)),
                  pl.BlockSpec((1, n_lat, 2 * RET_V_DIM), lambda i, h: (i, 0, h))],
        out_specs=pl.BlockSpec((1, n_lat, 2 * RET_V_DIM), lambda i, h: (i, 0, h)),
        out_shape=jax.ShapeDtypeStruct((b, n_lat, RET_HEADS * RET_V_DIM), BF16),
        scratch_shapes=[pltpu.VMEM((n_lat, 2 * RET_V_DIM), F32), pltpu.VMEM((n_lat, 2 * RET_V_DIM), F32),
                        pltpu.VMEM((4, LANES, RET_V_DIM), F32)],
        compiler_params=_cparams("parallel", "arbitrary"),
    )(decay_logit, rq, rk, rv, rg)


def _rope_tables(n_lat, n_ctx, rot_dim, lane0):
    t = np.arange(n_lat)
    rows = (t // GRID_W).astype(np.float32)
    cols = (t % GRID_W).astype(np.float32)
    m = rot_dim // 4
    freqs = jnp.asarray(ROPE_BASE, F32) ** (-jnp.arange(m, dtype=F32) / m)
    ang_r = jnp.asarray(rows)[:, None] * freqs
    ang_c = jnp.asarray(cols)[:, None] * freqs
    cos = jnp.concatenate([jnp.cos(ang_r)] * 2 + [jnp.cos(ang_c)] * 2, axis=-1)
    sin = jnp.concatenate([-jnp.sin(ang_r), jnp.sin(ang_r), -jnp.sin(ang_c), jnp.sin(ang_c)], axis=-1)
    reps = (LANES - lane0) // rot_dim if lane0 == 0 else 1
    cos = jnp.concatenate([jnp.ones((n_lat, lane0), F32)] + [cos] * reps
                          + [jnp.ones((n_lat, LANES - lane0 - reps * rot_dim), F32)], axis=-1)
    sin = jnp.concatenate([jnp.zeros((n_lat, lane0), F32)] + [sin] * reps
                          + [jnp.zeros((n_lat, LANES - lane0 - reps * rot_dim), F32)], axis=-1)
    cos = jnp.concatenate([cos, jnp.ones((n_ctx, LANES), F32)], axis=0)
    sin = jnp.concatenate([sin, jnp.zeros((n_ctx, LANES), F32)], axis=0)
    return cos, sin


def _rot_partner(w, rot_dim):
    k, n = w.shape
    q = rot_dim // 4
    return w.reshape(k, n // rot_dim, 2, 2, q)[:, :, :, ::-1, :].reshape(k, n)


def _pad_cols(w, lane0, width=LANES):
    k, n = w.shape
    return jnp.concatenate([jnp.zeros((k, lane0), w.dtype), w, jnp.zeros((k, width - lane0 - n), w.dtype)], axis=-1)


def _even_weights(w_in):
    n_rope = 2 * DIFF_HEADS * 2 * DIFF_HEAD_DIM
    dw = DIFF_HEADS * 2 * DIFF_HEAD_DIM
    nw = NA_HEADS * NA_HEAD_DIM
    scale = jnp.concatenate([jnp.full((dw,), DIFF_HEAD_DIM ** -0.5, F32), jnp.ones((2 * dw,), F32),
                             jnp.full((nw,), NA_HEAD_DIM ** -0.5, F32), jnp.ones((2 * nw,), F32)])
    w = w_in * scale
    return jnp.concatenate([w, _rot_partner(w[:, :n_rope], DIFF_HEAD_DIM)], axis=-1).astype(BF16), n_rope


def _odd_weights(w_in, w_uq, w_ukv):
    c0 = MLA_Q_RANK
    c1 = c0 + MLA_KV_RANK
    c2 = c1 + MLA_ROPE_DIM
    nq = RET_HEADS * RET_QK_DIM
    kr = w_in[:, c1:c2]
    w = jnp.concatenate([w_in[:, :c1],
                         _pad_cols(kr, MLA_NOPE_DIM), _pad_cols(_rot_partner(kr, MLA_ROPE_DIM), MLA_NOPE_DIM),
                         w_in[:, c2:c2 + nq], w_in[:, c2 + nq:c2 + 2 * nq] * (RET_QK_DIM ** -0.5),
                         w_in[:, c2 + 2 * nq:]], axis=-1).astype(BF16)
    r = w_uq.shape[0]
    uq = w_uq.reshape(r, MLA_HEADS, MLA_NOPE_DIM + MLA_ROPE_DIM)
    pad = jnp.zeros((r, MLA_HEADS, LANES - MLA_NOPE_DIM - MLA_ROPE_DIM), F32)
    uq_rot = _rot_partner(uq[:, :, MLA_NOPE_DIM:].reshape(r, -1), MLA_ROPE_DIM).reshape(r, MLA_HEADS, MLA_ROPE_DIM)
    wq = jnp.concatenate([uq, pad], axis=-1).reshape(r, -1)
    wq_rot = jnp.concatenate([jnp.zeros_like(uq[:, :, :MLA_NOPE_DIM]), uq_rot, pad], axis=-1).reshape(r, -1)
    rk = w_ukv.shape[0]
    ukv = w_ukv.reshape(rk, MLA_HEADS, MLA_NOPE_DIM + MLA_V_DIM)
    wk = jnp.concatenate([ukv[:, :, :MLA_NOPE_DIM], jnp.zeros((rk, MLA_HEADS, LANES - MLA_NOPE_DIM), F32)],
                         axis=-1).reshape(rk, -1)
    wv = ukv[:, :, MLA_NOPE_DIM:].reshape(rk, -1)
    return w, jnp.concatenate([wq, wq_rot], axis=-1).astype(BF16), jnp.concatenate([wk, wv], axis=-1).astype(BF16)


def _na_bias(rpb, n_lat):
    pat_of_group, idx_r, idx_c, valid = _na_patterns(n_lat // GRID_W)
    g = NA_GROUP_ROWS
    span = g + NA_WIN_ROWS - 1
    n_pat = idx_r.shape[0]
    col_sel = (idx_c[0, :GRID_W, :GRID_W, None] == np.arange(2 * NA_WIN_COLS - 1)).astype(np.float32)
    row_idx = idx_r.reshape(n_pat, g, GRID_W, span, GRID_W)[:, :, 0, :, 0]
    cols = jnp.einsum('hrc,qkc->hrqk', rpb, col_sel, precision=lax.Precision.HIGHEST)
    bias = jnp.take(cols, row_idx.reshape(-1), axis=1).reshape(rpb.shape[0], n_pat, g, span, GRID_W, GRID_W)
    bias = jnp.swapaxes(bias, 3, 4).reshape(rpb.shape[0], n_pat, g * GRID_W, span * GRID_W)
    return jnp.where(valid[None], bias * LOG2E, NEG_BIG), pat_of_group


def _router_weights(w_router):
    return _pad_cols(w_router, 0)


def kernel(x, c, ctx, c_ctx, ada_w, ada_b, norm_mix, norm_ffn, final_norm, even_w_in, even_w_out,
           diff_lambda, diff_subln, na_rpb, odd_w_in, odd_w_out, mla_q_norm, mla_w_uq, mla_kv_norm,
           mla_w_ukv, ret_decay_logit, moe_router, moe_w1, moe_w3, moe_w2):
    b, n_lat, d = x.shape
    n_ctx = ctx.shape[1]
    depth = ada_w.shape[0]
    n_lat_tiles = n_lat // TOK_TILE

    cond = jnp.concatenate([c, c_ctx[None], jnp.zeros((7, d), F32)], axis=0)
    mods_all = _mods(cond, ada_w, ada_b)[:, :b + 1].reshape(depth, b + 1, 6, d)
    xa = jnp.concatenate([x, ctx], axis=1)

    for l in range(depth):
        mods = mods_all[l]
        need_ctx = l < depth - 1
        i = l // 2
        gain_mix = norm_mix[l][None]
        if l % 2 == 0:
            w, n_rope = _even_weights(even_w_in[i])
            cos, sin = _rope_tables(n_lat, n_ctx, DIFF_HEAD_DIM, 0)
            qkv = _proj_even(xa, mods, gain_mix, w, cos, sin, n_lat, n_rope, even_w_in.shape[-1])
            lam_init = 0.8 - 0.6 * math.exp(-0.3 * l)
            mix_a = _diff_attn(qkv, diff_lambda[i], diff_subln[i][None], lam_init, 0, n_lat, ATTN_Q_TILE,
                               0, n_lat + n_ctx)
            mix_a_ctx = _diff_attn(qkv, diff_lambda[i], diff_subln[i][None], lam_init, n_lat, n_ctx, n_ctx,
                                   n_lat, n_ctx)
            bias, offsets = _na_bias(na_rpb[i], n_lat)
            mix_b = _na_attn(qkv, bias, n_lat, offsets, 3 * DIFF_HEADS * 2 * DIFF_HEAD_DIM)
            w_out = even_w_out[i].astype(BF16)
        else:
            w, wq, wkv = _odd_weights(odd_w_in[i], mla_w_uq[i], mla_w_ukv[i])
            cos, sin = _rope_tables(n_lat, n_ctx, MLA_ROPE_DIM, MLA_NOPE_DIM)
            q, k, v, rq, rk, rv, rg = _proj_odd(xa, mods, gain_mix, w, mla_q_norm[i][None], mla_kv_norm[i][None],
                                                wq, wkv, cos, sin, n_lat)
            mix_a = _mla_attn(q, k, v, n_lat, ATTN_Q_TILE)
            mix_a_ctx = mix_a
            mix_b = _retention(rq, rk, rv, rg, ret_decay_logit[i], n_lat)
            w_out = odd_w_out[i].astype(BF16)

        tiles = (n_lat + n_ctx) // TOK_TILE if need_ctx else n_lat_tiles
        x_mid, h2, aff = _proj_out(xa, mix_a, mix_a_ctx, mix_b, w_out, mods, norm_ffn[l][None],
                                   _router_weights(moe_router[l]), tiles, n_lat_tiles)
        segments = ((0, n_lat), (n_lat, n_ctx)) if need_ctx else ((0, n_lat),)
        xa = _moe(x_mid, h2, aff, segments, mods, final_norm[None], n_lat_tiles, l == depth - 1,
                  moe_w1, moe_w3, moe_w2, l)
    return xa
```

```python
import functools
import math

import jax
import jax.numpy as jnp
import numpy as np
from jax import lax
from jax.experimental import pallas as pl
from jax.experimental.pallas import tpu as pltpu

GRID_W = 64
ROPE_BASE = 10000.0
RMS_EPS = 1e-6
DIFF_HEADS = 4
DIFF_HEAD_DIM = 64
NA_HEADS = 8
NA_HEAD_DIM = 64
NA_WIN_ROWS = 8
NA_WIN_COLS = 16
MLA_HEADS = 8
MLA_Q_RANK = 256
MLA_KV_RANK = 128
MLA_NOPE_DIM = 64
MLA_ROPE_DIM = 32
MLA_V_DIM = 64
RET_HEADS = 4
RET_QK_DIM = 64
RET_V_DIM = 128
RET_CHUNK = 128
N_EXPERTS = 16
EC_CAPACITY_FACTOR = 2

LANES = 128
VMEM_LIMIT = 56 * 1024 * 1024
NEG_BIG = -1e30
TOK_TILE = 256
ROW_TILES = 8
ATTN_Q_TILE = 512
FFN_ROW_TILES = 8
FFN_HIDDEN_BLOCK = 512
NA_GROUP_ROWS = 4
NA_GROUPS_PER_STEP = 4
RET_UNROLL = 4
PAIR_CHUNK = 256
LOG2E = math.log2(math.e)

BF16 = jnp.bfloat16
F32 = jnp.float32


def _cparams(*sem):
    return pltpu.CompilerParams(dimension_semantics=sem, vmem_limit_bytes=VMEM_LIMIT)


def _dot(a, b):
    return jnp.dot(a, b, preferred_element_type=F32)


def _dot_nt(a, b):
    return lax.dot_general(a, b, (((1,), (1,)), ((), ())), preferred_element_type=F32)


def _dot_tn(a, b):
    return lax.dot_general(a, b, (((0,), (0,)), ((), ())), preferred_element_type=F32)


def _rms(x):
    return x * lax.rsqrt(jnp.mean(x * x, axis=-1, keepdims=True) + RMS_EPS)


def _silu(x):
    return x * (1.0 / (1.0 + jnp.exp(-x)))


def _tile_lanes(t, n):
    return jnp.concatenate([t] * n, axis=-1)


def _mods_kernel(c_ref, w_ref, b_ref, o_ref):
    o_ref[0] = _dot(_silu(c_ref[...]), w_ref[0]) + b_ref[0]


def _mods(cond, ada_w, ada_b):
    depth, d, n = ada_w.shape
    r = cond.shape[0]
    tn = 1536
    return pl.pallas_call(
        _mods_kernel,
        grid=(depth, n // tn),
        in_specs=[pl.BlockSpec((r, d), lambda l, j: (0, 0)),
                  pl.BlockSpec((1, d, tn), lambda l, j: (l, 0, j)),
                  pl.BlockSpec((1, 1, tn), lambda l, j: (l, 0, j))],
        out_specs=pl.BlockSpec((1, r, tn), lambda l, j: (l, 0, j)),
        out_shape=jax.ShapeDtypeStruct((depth, r, n), F32),
        compiler_params=_cparams("arbitrary", "arbitrary"),
    )(cond, ada_w, ada_b.reshape(depth, 1, n))


def _normed(x, gain, mods, shift_row, scale_row):
    return _rms(x) * gain * (1.0 + mods[scale_row:scale_row + 1]) + mods[shift_row:shift_row + 1]


def _mixer_input(x, gain, mods_lat, mods_ctx, n_lat):
    tm = x.shape[0]
    rows = pl.program_id(1) * tm + lax.broadcasted_iota(jnp.int32, (tm, 1), 0)
    is_ctx = rows >= n_lat
    scale = jnp.where(is_ctx, mods_ctx[1:2], mods_lat[1:2])
    shift = jnp.where(is_ctx, mods_ctx[0:1], mods_lat[0:1])
    return _rms(x) * gain * (1.0 + scale) + shift


def _proj_even_kernel(x_ref, ml_ref, mc_ref, g_ref, w_ref, cos_ref, sin_ref, o_ref, *, n_rope, n_out, n_lat):
    na_q0 = n_rope + n_rope // 2
    na_q1 = na_q0 + NA_HEADS * NA_HEAD_DIM
    h = _mixer_input(x_ref[0], g_ref[...], ml_ref[0], mc_ref[0], n_lat)
    acc = _dot(h.astype(BF16), w_ref[...])
    reps = n_rope // LANES
    cos = _tile_lanes(cos_ref[...], reps)
    sin = _tile_lanes(sin_ref[...], reps)
    roped = acc[:, :n_rope] * cos + acc[:, n_out:] * sin
    n_q = n_rope // 2
    o_ref[0, :, :n_q] = (roped[:, :n_q] * LOG2E).astype(BF16)
    o_ref[0, :, n_q:n_rope] = roped[:, n_q:].astype(BF16)
    o_ref[0, :, n_rope:na_q0] = acc[:, n_rope:na_q0].astype(BF16)
    o_ref[0, :, na_q0:na_q1] = (acc[:, na_q0:na_q1] * LOG2E).astype(BF16)
    o_ref[0, :, na_q1:] = acc[:, na_q1:n_out].astype(BF16)


def _row_tile(t):
    tm = t // ROW_TILES
    assert tm * ROW_TILES == t and tm % 16 == 0
    return tm


def _proj_even(xa, mods, gain, w, cos, sin, n_lat, n_rope, n_out):
    b, t, d = xa.shape
    nb = mods.shape[0] - 1
    tm = _row_tile(t)
    kern = functools.partial(_proj_even_kernel, n_rope=n_rope, n_out=n_out, n_lat=n_lat)
    return pl.pallas_call(
        kern,
        grid=(b, t // tm),
        in_specs=[pl.BlockSpec((1, tm, d), lambda i, j: (i, j, 0)),
                  pl.BlockSpec((1, 6, d), lambda i, j: (i, 0, 0)),
                  pl.BlockSpec((1, 6, d), lambda i, j: (nb, 0, 0)),
                  pl.BlockSpec((1, d), lambda i, j: (0, 0)),
                  pl.BlockSpec(w.shape, lambda i, j: (0, 0)),
                  pl.BlockSpec((tm, LANES), lambda i, j: (j, 0)),
                  pl.BlockSpec((tm, LANES), lambda i, j: (j, 0))],
        out_specs=pl.BlockSpec((1, tm, n_out), lambda i, j: (i, j, 0)),
        out_shape=jax.ShapeDtypeStruct((b, t, n_out), BF16),
        compiler_params=_cparams("parallel", "arbitrary"),
    )(xa, mods, mods, gain, w, cos, sin)


def _fill_v_ones(vx_ref, v_ref):
    vx_ref[:, :LANES] = v_ref[0]
    vx_ref[:, LANES:] = jnp.ones((vx_ref.shape[0], LANES), BF16)


def _weights(s, m):
    return jnp.exp2(s - m).astype(BF16)


def _rowmax(*ss):
    m = jnp.max(ss[0], axis=-1, keepdims=True)
    for s in ss[1:]:
        m = jnp.maximum(m, jnp.max(s, axis=-1, keepdims=True))
    return m


def _normalised(ox):
    return ox[:, :LANES] / ox[:, LANES:LANES + 1]


def _scores(q, segs):
    return [_dot_nt(q, k) if bias is None else _dot_nt(q, k) + bias for k, bias in segs]


def _pv(ps, vxs):
    o = _dot(ps[0], vxs[0])
    for p, vx in zip(ps[1:], vxs[1:]):
        o = o + _dot(p, vx)
    return o


def _multi_stream_attention(streams):
    n = len(streams)
    s, m, p, o = [None] * n, [None] * n, [None] * n, [None] * n
    for t in range(n + 2):
        if t < n:
            s[t] = _scores(streams[t][0], streams[t][1])
        if 0 <= t - 2 < n:
            o[t - 2] = _normalised(_pv(p[t - 2], streams[t - 2][2]))
        if 0 <= t - 1 < n:
            p[t - 1] = [_weights(x, m[t - 1]) for x in s[t - 1]]
        if t < n:
            m[t] = _rowmax(*s[t])
    return o


def _two_stream_attention(qa, segs_a, qb, segs_b, vxs):
    return _multi_stream_attention([(qa, segs_a, vxs), (qb, segs_b, vxs)])


def _diff_attn_kernel(lam_ref, g_ref, q_ref, k_ref, v_ref, o_ref, vx_ref, *, lam_init):
    @pl.when(pl.program_id(2) == 0)
    def _():
        _fill_v_ones(vx_ref, v_ref)

    lp = lam_ref[...]
    s1 = jnp.sum(lp[0:1] * lp[1:2], axis=-1, keepdims=True)
    s2 = jnp.sum(lp[2:3] * lp[3:4], axis=-1, keepdims=True)
    lam = jnp.exp(s1) - jnp.exp(s2) + lam_init
    q = q_ref[0]
    lane = lax.broadcasted_iota(jnp.int32, (1, LANES), 1)
    q1 = jnp.where(lane < DIFF_HEAD_DIM, q, jnp.zeros_like(q))
    q2 = jnp.where(lane >= DIFF_HEAD_DIM, q, jnp.zeros_like(q))

    k = k_ref[0]
    o1, o2 = _two_stream_attention(q1, [(k, None)], q2, [(k, None)], [vx_ref[...]])
    o = o1 - lam * o2
    o_ref[0] = (_rms(o) * g_ref[...] * (1.0 - lam_init)).astype(BF16)


def _diff_attn(qkv, lam_params, subln, lam_init, q_row0, n_q, tq, k_row0, n_k):
    b = qkv.shape[0]
    nh = DIFF_HEADS
    q0, k0 = q_row0 // tq, k_row0 // n_k
    assert q0 * tq == q_row0 and k0 * n_k == k_row0 and n_q % tq == 0
    return pl.pallas_call(
        functools.partial(_diff_attn_kernel, lam_init=lam_init),
        grid=(b, nh, n_q // tq),
        in_specs=[pl.BlockSpec(lam_params.shape, lambda i, h, j: (0, 0)),
                  pl.BlockSpec((1, LANES), lambda i, h, j: (0, 0)),
                  pl.BlockSpec((1, tq, LANES), lambda i, h, j: (i, q0 + j, h)),
                  pl.BlockSpec((1, n_k, LANES), lambda i, h, j: (i, k0, nh + h)),
                  pl.BlockSpec((1, n_k, LANES), lambda i, h, j: (i, k0, 2 * nh + h))],
        out_specs=pl.BlockSpec((1, tq, LANES), lambda i, h, j: (i, j, h)),
        out_shape=jax.ShapeDtypeStruct((b, n_q, nh * LANES), BF16),
        scratch_shapes=[pltpu.VMEM((n_k, 2 * LANES), BF16)],
        compiler_params=_cparams("parallel", "parallel", "arbitrary"),
    )(lam_params, subln, qkv, qkv, qkv)


def _na_patterns(rows):
    g = NA_GROUP_ROWS
    span = g + NA_WIN_ROWS - 1
    nq, nk = g * GRID_W, span * GRID_W
    qi = np.arange(nq)[:, None]
    ki = np.arange(nk)[None, :]
    qcol, kcol = qi % GRID_W, ki % GRID_W
    cs = np.clip(qcol - NA_WIN_COLS // 2, 0, GRID_W - NA_WIN_COLS)
    col_ok = (kcol >= cs) & (kcol < cs + NA_WIN_COLS)
    idx_c = np.clip(kcol - qcol + NA_WIN_COLS - 1, 0, 2 * NA_WIN_COLS - 2) + 0 * qi
    seen, pat_of_group, idx_r, valid = {}, [], [], []
    for r0 in range(0, rows, g):
        ks = min(max(r0 - NA_WIN_ROWS // 2, 0), rows - span)
        qrow = r0 + qi // GRID_W
        krow = ks + ki // GRID_W
        rs = np.clip(qrow - NA_WIN_ROWS // 2, 0, rows - NA_WIN_ROWS)
        ok = (krow >= rs) & (krow < rs + NA_WIN_ROWS) & col_ok
        ir = np.clip(krow - qrow + NA_WIN_ROWS - 1, 0, 2 * NA_WIN_ROWS - 2) + 0 * kcol
        key = (ok.tobytes(), np.where(ok, ir, 0).tobytes())
        if key not in seen:
            seen[key] = len(idx_r)
            idx_r.append(ir)
            valid.append(ok)
        pat_of_group.append(seen[key])
    n_pat = len(idx_r)
    return tuple(pat_of_group), np.stack(idx_r), np.stack([idx_c] * n_pat), np.stack(valid)


def _na_kernel(q_ref, k_ref, v_ref, bias_ref, o_ref, vx_ref, *, n_lat, pat_of_group):
    g_rows = NA_GROUP_ROWS
    span = g_rows + NA_WIN_ROWS - 1
    rows = n_lat // GRID_W
    nq, nk = g_rows * GRID_W, span * GRID_W
    lane = lax.broadcasted_iota(jnp.int32, (1, LANES), 1)
    head_masks = (lane < NA_HEAD_DIM, lane >= NA_HEAD_DIM)
    _fill_v_ones(vx_ref, v_ref)
    kc = k_ref[0, n_lat:, :]
    vxc = vx_ref[n_lat:, :]

    def both_heads(blocks):
        streams = []
        for q, segs, vxs in blocks:
            for hh, hm in enumerate(head_masks):
                streams.append((jnp.where(hm, q, jnp.zeros_like(q)),
                                [(k, None if bi is None else bias_ref[hh, bi]) for k, bi in segs], vxs))
        outs = _multi_stream_attention(streams)
        return [jnp.where(head_masks[0], outs[2 * i], outs[2 * i + 1]).astype(BF16) for i in range(len(blocks))]

    def group_block(g):
        r0 = g * g_rows
        ks = jnp.clip(r0 - NA_WIN_ROWS // 2, 0, rows - span)
        common = max(set(pat_of_group), key=pat_of_group.count)
        pat = jnp.int32(common)
        for gi, p in enumerate(pat_of_group):
            if p != common:
                pat = jnp.where(g == gi, p, pat)
        qg = q_ref[0, pl.ds(pl.multiple_of(g * nq, nq), nq), :]
        kw = k_ref[0, pl.ds(pl.multiple_of(ks * GRID_W, GRID_W), nk), :]
        vxw = vx_ref[pl.ds(pl.multiple_of(ks * GRID_W, GRID_W), nk), :]
        return qg, [(kw, pat), (kc, None)], [vxw, vxc]

    def group_step(i, carry):
        gs = [i * NA_GROUPS_PER_STEP + u for u in range(NA_GROUPS_PER_STEP)]
        for g, out in zip(gs, both_heads([group_block(g) for g in gs])):
            o_ref[0, pl.ds(pl.multiple_of(g * nq, nq), nq), :] = out
        return carry

    lax.fori_loop(0, rows // g_rows // NA_GROUPS_PER_STEP, group_step, 0)

    o_ref[0, n_lat:, :] = both_heads([(q_ref[0, n_lat:, :], [(kc, None)], [vxc])])[0]


def _na_attn(qkv, bias, n_lat, pat_of_group, col0):
    b, t, _ = qkv.shape
    npairs = NA_HEADS // 2
    cb = col0 // LANES
    kern = functools.partial(_na_kernel, n_lat=n_lat, pat_of_group=pat_of_group)
    return pl.pallas_call(
        kern,
        grid=(b, npairs),
        in_specs=[pl.BlockSpec((1, t, LANES), lambda i, h: (i, 0, cb + h)),
                  pl.BlockSpec((1, t, LANES), lambda i, h: (i, 0, cb + npairs + h)),
                  pl.BlockSpec((1, t, LANES), lambda i, h: (i, 0, cb + 2 * npairs + h)),
                  pl.BlockSpec((2,) + bias.shape[1:], lambda i, h: (h, 0, 0, 0))],
        out_specs=pl.BlockSpec((1, t, LANES), lambda i, h: (i, 0, h)),
        out_shape=jax.ShapeDtypeStruct((b, t, npairs * LANES), BF16),
        scratch_shapes=[pltpu.VMEM((t, 2 * LANES), BF16)],
        compiler_params=_cparams("parallel", "arbitrary"),
    )(qkv, qkv, qkv, bias)


def _proj_out_kernel(x_ref, a_ref, ac_ref, b_ref, wa_ref, wb_ref, m_ref, g_ref, wr_ref, xo_ref, h_ref, aff_ref,
                     *, n_lat_tiles):
    mods = m_ref[0]
    a = jnp.where(pl.program_id(1) >= n_lat_tiles, ac_ref[0], a_ref[0])
    y = _dot(a, wa_ref[...]) + _dot(b_ref[0], wb_ref[...])
    x = x_ref[0] + mods[2:3] * y
    xo_ref[0] = x
    h2 = _normed(x, g_ref[...], mods, 3, 4)
    h_hi = h2.astype(BF16)
    h_ref[0] = h_hi
    h_lo = (h2 - h_hi.astype(F32)).astype(BF16)
    wr = wr_ref[...]
    w_hi = wr.astype(BF16)
    w_lo = (wr - w_hi.astype(F32)).astype(BF16)
    hh = _dot(h_hi, jnp.concatenate([w_hi, w_lo], axis=-1))
    logits = hh[:, :LANES] + (hh[:, LANES:] + _dot(h_lo, w_hi))
    lane = lax.broadcasted_iota(jnp.int32, logits.shape, 1)
    logits = jnp.where(lane < N_EXPERTS, logits, NEG_BIG)
    e = jnp.exp(logits - jnp.max(logits, axis=-1, keepdims=True))
    aff_ref[0] = e / jnp.sum(e, axis=-1, keepdims=True)


def _proj_out(xa, a, a_ctx, bb, w_out, mods, gain, w_router, n_tiles, n_lat_tiles):
    b, t, d = xa.shape
    nb = mods.shape[0] - 1
    tm = TOK_TILE
    wa, wb = w_out[:a.shape[-1]], w_out[a.shape[-1]:]
    tok = lambda i, j: (i, j, 0)
    const = lambda i, j: (0, 0)
    return pl.pallas_call(
        functools.partial(_proj_out_kernel, n_lat_tiles=n_lat_tiles),
        grid=(b, n_tiles),
        in_specs=[pl.BlockSpec((1, tm, d), tok),
                  pl.BlockSpec((1, tm, a.shape[-1]), lambda i, j: (i, jnp.minimum(j, n_lat_tiles - 1), 0)),
                  pl.BlockSpec((1, tm, a.shape[-1]), lambda i, j: (i, jnp.maximum(j - n_lat_tiles, 0), 0)),
                  pl.BlockSpec((1, tm, bb.shape[-1]), tok),
                  pl.BlockSpec(wa.shape, const),
                  pl.BlockSpec(wb.shape, const),
                  pl.BlockSpec((1, 6, d), lambda i, j: (jnp.where(j >= n_lat_tiles, nb, i), 0, 0)),
                  pl.BlockSpec((1, d), const),
                  pl.BlockSpec(w_router.shape, const)],
        out_specs=[pl.BlockSpec((1, tm, d), tok),
                   pl.BlockSpec((1, tm, d), tok),
                   pl.BlockSpec((1, tm, LANES), tok)],
        out_shape=[jax.ShapeDtypeStruct((b, n_tiles * tm, d), F32),
                   jax.ShapeDtypeStruct((b, n_tiles * tm, d), BF16),
                   jax.ShapeDtypeStruct((b, n_tiles * tm, LANES), F32)],
        compiler_params=_cparams("parallel", "arbitrary"),
    )(xa, a, a_ctx, bb, wa, wb, mods, gain, w_router)


def _ffn_kernel(x_ref, gate_ref, w1_ref, w3_ref, w2_ref, o_ref, w1b, w3b, w2b):
    @pl.when(pl.program_id(1) == 0)
    def _():
        w1b[...] = w1_ref[0].astype(BF16)
        w3b[...] = w3_ref[0].astype(BF16)
        w2b[...] = w2_ref[0].astype(BF16)

    x = x_ref[0]
    f = w1b.shape[1]
    y = None
    for f0 in range(0, f, FFN_HIDDEN_BLOCK):
        f1 = f0 + FFN_HIDDEN_BLOCK
        hid = _silu(_dot(x, w1b[:, f0:f1])) * _dot(x, w3b[:, f0:f1])
        part = _dot(hid.astype(BF16), w2b[f0:f1, :])
        y = part if y is None else y + part
    o_ref[0] = (y * gate_ref[0]).astype(BF16)


def _expert_ffn(xe, gate, w1, w3, w2, layer):
    e, m, d = xe.shape
    f = w1.shape[-1]
    tm = m // FFN_ROW_TILES
    assert tm * FFN_ROW_TILES == m and tm % 16 == 0
    wspec = lambda r, c: pl.BlockSpec((None, 1, r, c), lambda i, j: (layer, i, 0, 0))
    return pl.pallas_call(
        _ffn_kernel,
        grid=(e, m // tm),
        in_specs=[pl.BlockSpec((1, tm, d), lambda i, j: (i, j, 0)),
                  pl.BlockSpec((1, tm, 1), lambda i, j: (i, j, 0)),
                  wspec(d, f), wspec(d, f), wspec(f, d)],
        out_specs=pl.BlockSpec((1, tm, d), lambda i, j: (i, j, 0)),
        out_shape=jax.ShapeDtypeStruct((e, m, d), BF16),
        scratch_shapes=[pltpu.VMEM((d, f), BF16), pltpu.VMEM((d, f), BF16), pltpu.VMEM((f, d), BF16)],
        compiler_params=_cparams("parallel", "arbitrary"),
    )(xe, gate, w1, w3, w2)


def _route(aff, segments):
    b, r, _ = aff.shape
    e = N_EXPERTS
    gates, flats, toks = [], [], []
    for row0, n in segments:
        cap = (EC_CAPACITY_FACTOR * n) // e
        gate, idx = lax.top_k(jnp.swapaxes(aff[:, row0:row0 + n, :e], 1, 2), cap)
        tok = idx + row0
        gates.append(jnp.swapaxes(gate, 0, 1).reshape(e, b * cap))
        flats.append(jnp.swapaxes(tok + jnp.arange(b, dtype=idx.dtype)[:, None, None] * r, 0, 1).reshape(e, b * cap))
        toks.append(tok)
    return jnp.concatenate(gates, axis=1)[..., None], jnp.concatenate(flats, axis=1), toks


def _combine_kernel(lo_ref, x_ref, tok_ref, y_ref, m_ref, g_ref, o_ref, *, final):
    i, j = pl.program_id(0), pl.program_id(1)
    tm = x_ref.shape[1]
    n_tiles = pl.num_programs(1)
    lo = lo_ref[i * (n_tiles + 1) + j]
    hi = lo_ref[i * (n_tiles + 1) + j + 1]
    rows = j * tm + lax.broadcasted_iota(jnp.int32, (tm, 1), 0)

    def chunk_sum(c):
        sel = (tok_ref[0, pl.ds(c, 1), :] == rows).astype(BF16)
        return _dot(sel, y_ref[0, pl.ds(pl.multiple_of(c * PAIR_CHUNK, PAIR_CHUNK), PAIR_CHUNK), :])

    def chunk(c, carry):
        o_ref[0] += chunk_sum(c)
        return carry

    c0 = jnp.minimum(lo // PAIR_CHUNK, tok_ref.shape[1] - 1)
    o_ref[0] = chunk_sum(c0)
    lax.fori_loop(c0 + 1, (hi + PAIR_CHUNK - 1) // PAIR_CHUNK, chunk, 0)
    x = x_ref[0] + m_ref[0][5:6] * o_ref[0]
    if final:
        x = _rms(x) * g_ref[...]
    o_ref[0] = x


def _combine(x, tok_sorted, y_sorted, lo, mods, gain, n_lat_tiles, final):
    b, t, d = x.shape
    p = tok_sorted.shape[1]
    nb = mods.shape[0] - 1
    tm = TOK_TILE
    tok = lambda i, j, lo_ref: (i, j, 0)
    whole = lambda i, j, lo_ref: (i, 0, 0)
    grid_spec = pltpu.PrefetchScalarGridSpec(
        num_scalar_prefetch=1,
        grid=(b, t // tm),
        in_specs=[pl.BlockSpec((1, tm, d), tok),
                  pl.BlockSpec((1, p // PAIR_CHUNK, PAIR_CHUNK), whole),
                  pl.BlockSpec((1, p, d), whole),
                  pl.BlockSpec((1, 6, d), lambda i, j, lo_ref: (jnp.where(j >= n_lat_tiles, nb, i), 0, 0)),
                  pl.BlockSpec((1, d), lambda i, j, lo_ref: (0, 0))],
        out_specs=pl.BlockSpec((1, tm, d), tok))
    return pl.pallas_call(
        functools.partial(_combine_kernel, final=final),
        grid_spec=grid_spec,
        out_shape=jax.ShapeDtypeStruct((b, t, d), F32),
        compiler_params=_cparams("parallel", "arbitrary"),
    )(lo, x, tok_sorted.reshape(b, p // PAIR_CHUNK, PAIR_CHUNK), y_sorted, mods, gain)


def _moe(x_mid, h2, aff, segments, mods, gain, n_lat_tiles, final, w1, w3, w2, layer):
    b, r, d = h2.shape
    gate, flat, toks = _route(aff, segments)
    e, m = flat.shape
    xe = h2.reshape(b * r, d).at[flat].get(mode="promise_in_bounds")
    ye = _expert_ffn(xe, gate, w1, w3, w2, layer)
    srcs, off = [], 0
    for tok in toks:
        cap = tok.shape[-1]
        src = (jnp.arange(e, dtype=jnp.int32)[None, :, None] * m + off
               + jnp.arange(b, dtype=jnp.int32)[:, None, None] * cap + jnp.arange(cap, dtype=jnp.int32))
        srcs.append(src.reshape(b, e * cap))
        off += b * cap
    tok_all = jnp.concatenate([tok.reshape(b, -1) for tok in toks], axis=1)
    src_bits = max(1, (e * m - 1).bit_length())
    assert (r - 1).bit_length() + src_bits <= 31
    packed = jnp.sort(tok_all * (1 << src_bits) + jnp.concatenate(srcs, axis=1), axis=1)
    tok_sorted, src_sorted = packed >> src_bits, packed & ((1 << src_bits) - 1)
    y_sorted = ye.reshape(e * m, d).at[src_sorted].get(mode="promise_in_bounds")
    starts = jnp.arange(r // TOK_TILE + 1, dtype=jnp.int32) * TOK_TILE
    lo = jnp.sum((tok_sorted[:, None, :] < starts[None, :, None]).astype(jnp.int32), axis=-1)
    return _combine(x_mid, tok_sorted, y_sorted, lo.reshape(-1), mods, gain, n_lat_tiles, final)


def _proj_odd_kernel(x_ref, ml_ref, mc_ref, g_ref, w_ref, qn_ref, kvn_ref, wq_ref, wkv_ref, cos_ref, sin_ref,
                     q_ref, k_ref, v_ref, rq_ref, rk_ref, rv_ref, rg_ref, *, q_scale, n_lat):
    h = _mixer_input(x_ref[0], g_ref[...], ml_ref[0], mc_ref[0], n_lat)
    acc = _dot(h.astype(BF16), w_ref[...])
    cos, sin = cos_ref[...], sin_ref[...]
    c0 = MLA_Q_RANK
    c1 = c0 + MLA_KV_RANK
    cq = acc[:, :c0]
    ckv = acc[:, c0:c1]
    kr = acc[:, c1:c1 + LANES] * cos + acc[:, c1 + LANES:c1 + 2 * LANES] * sin
    c2 = c1 + 2 * LANES
    nq = RET_HEADS * RET_QK_DIM
    nv = RET_HEADS * RET_V_DIM
    rq_ref[0] = acc[:, c2:c2 + nq].astype(BF16)
    rk_ref[0] = acc[:, c2 + nq:c2 + 2 * nq].astype(BF16)
    rv_ref[0] = acc[:, c2 + 2 * nq:c2 + 2 * nq + nv].astype(BF16)
    rg_ref[0] = acc[:, c2 + 2 * nq + nv:]

    nqk = MLA_HEADS * LANES
    qq = _dot((_rms(cq) * qn_ref[...]).astype(BF16), wq_ref[...])
    q = qq[:, :nqk] * _tile_lanes(cos, MLA_HEADS) + qq[:, nqk:] * _tile_lanes(sin, MLA_HEADS)
    q_ref[0] = (q * q_scale).astype(BF16)
    kv = _dot((_rms(ckv) * kvn_ref[...]).astype(BF16), wkv_ref[...])
    k_ref[0] = (kv[:, :nqk] + _tile_lanes(kr, MLA_HEADS)).astype(BF16)
    v_ref[0] = kv[:, nqk:].astype(BF16)


def _proj_odd(xa, mods, gain, w, q_norm, kv_norm, wq, wkv, cos, sin, n_lat):
    b, t, d = xa.shape
    nb = mods.shape[0] - 1
    tm = _row_tile(t)
    tok = lambda i, j: (i, j, 0)
    const = lambda i, j: (0, 0)
    widths = (MLA_HEADS * LANES, MLA_HEADS * LANES, MLA_HEADS * MLA_V_DIM,
              RET_HEADS * RET_QK_DIM, RET_HEADS * RET_QK_DIM, RET_HEADS * RET_V_DIM, RET_HEADS * RET_V_DIM)
    dtypes = (BF16,) * 6 + (F32,)
    kern = functools.partial(_proj_odd_kernel, q_scale=float((MLA_NOPE_DIM + MLA_ROPE_DIM) ** -0.5) * LOG2E,
                             n_lat=n_lat)
    return pl.pallas_call(
        kern,
        grid=(b, t // tm),
        in_specs=[pl.BlockSpec((1, tm, d), tok),
                  pl.BlockSpec((1, 6, d), lambda i, j: (i, 0, 0)),
                  pl.BlockSpec((1, 6, d), lambda i, j: (nb, 0, 0)),
                  pl.BlockSpec((1, d), const),
                  pl.BlockSpec(w.shape, const),
                  pl.BlockSpec(q_norm.shape, const),
                  pl.BlockSpec(kv_norm.shape, const),
                  pl.BlockSpec(wq.shape, const),
                  pl.BlockSpec(wkv.shape, const),
                  pl.BlockSpec((tm, LANES), lambda i, j: (j, 0)),
                  pl.BlockSpec((tm, LANES), lambda i, j: (j, 0))],
        out_specs=[pl.BlockSpec((1, tm, wd), tok) for wd in widths],
        out_shape=[jax.ShapeDtypeStruct((b, t, wd), dt) for wd, dt in zip(widths, dtypes)],
        compiler_params=_cparams("parallel", "arbitrary"),
    )(xa, mods, mods, gain, w, q_norm, kv_norm, wq, wkv, cos, sin)


def _mla_kernel(q_ref, k_ref, v_ref, o_ref, vx_ref):
    @pl.when(pl.program_id(2) == 0)
    def _():
        _fill_v_ones(vx_ref, v_ref)

    o0, o1 = _two_stream_attention(q_ref[0, :, :LANES], [(k_ref[0, :, :LANES], None)],
                                   q_ref[0, :, LANES:], [(k_ref[0, :, LANES:], None)], [vx_ref[...]])
    lane = lax.broadcasted_iota(jnp.int32, (1, LANES), 1)
    o_ref[0] = jnp.where(lane < MLA_V_DIM, o0, o1).astype(BF16)


def _mla_attn(q, k, v, n_lat, tq):
    b, t, _ = k.shape
    npairs = MLA_HEADS // 2
    return pl.pallas_call(
        _mla_kernel,
        grid=(b, npairs, n_lat // tq),
        in_specs=[pl.BlockSpec((1, tq, 2 * LANES), lambda i, h, j: (i, j, h)),
                  pl.BlockSpec((1, t, 2 * LANES), lambda i, h, j: (i, 0, h)),
                  pl.BlockSpec((1, t, LANES), lambda i, h, j: (i, 0, h))],
        out_specs=pl.BlockSpec((1, tq, LANES), lambda i, h, j: (i, j, h)),
        out_shape=jax.ShapeDtypeStruct((b, n_lat, npairs * LANES), BF16),
        scratch_shapes=[pltpu.VMEM((t, 2 * LANES), BF16)],
        compiler_params=_cparams("parallel", "parallel", "arbitrary"),
    )(q, k, v)


def _retention_kernel(dl_ref, q_ref, k_ref, v_ref, g_ref, o_ref, f_ref, b_ref, st_ref, *, n_lat):
    c = RET_CHUNK
    t = k_ref.shape[1]
    n_chunks = n_lat // c
    n_ctx = (t - n_lat) // c
    hp = pl.program_id(1)
    lane = lax.broadcasted_iota(jnp.int32, (1, LANES), 1)
    pos_r = lax.broadcasted_iota(jnp.int32, (c, 1), 0).astype(F32)
    ii = lax.broadcasted_iota(jnp.int32, (c, c), 0)
    jj = lax.broadcasted_iota(jnp.int32, (c, c), 1)
    rel = (ii - jj).astype(F32)
    dl = dl_ref[...]
    hsel = lax.broadcasted_iota(jnp.int32, (1, RET_HEADS), 1)

    chains = []
    for hh in range(2):
        hmask = (lane >= hh * RET_QK_DIM) & (lane < (hh + 1) * RET_QK_DIM)
        for direction in range(2):
            logit = jnp.sum(jnp.where(hsel == 2 * hp + hh, dl[direction:direction + 1], 0.0),
                            axis=-1, keepdims=True)
            lg = jnp.minimum(logit, 0.0) - jnp.log(1.0 + jnp.exp(-jnp.abs(logit)))
            if direction == 0:
                mask = rel >= 0
                d_in = jnp.where(mask, jnp.exp(lg * jnp.where(mask, rel, 0.0)), 0.0)
                d_q = jnp.exp(lg * (pos_r + 1.0))
                d_k = jnp.exp(lg * (c - 1.0 - pos_r))
            else:
                mask = rel < 0
                d_in = jnp.where(mask, jnp.exp(lg * jnp.where(mask, -rel, 0.0)), 0.0)
                d_q = jnp.exp(lg * (c - pos_r))
                d_k = jnp.exp(lg * pos_r)
            chains.append(dict(idx=2 * hh + direction, v0=hh * RET_V_DIM, direction=direction, hmask=hmask,
                               d_in=d_in, d_q=d_q, d_k=d_k, d_chunk=jnp.exp(lg * c)))

    st_ref[...] = jnp.zeros(st_ref.shape, F32)

    def kv_update(ch, start):
        kb = k_ref[0, pl.ds(start, c), :].astype(F32)
        vb = v_ref[0, pl.ds(start, c), ch["v0"]:ch["v0"] + RET_V_DIM]
        kd = jnp.where(ch["hmask"], kb * ch["d_k"], 0.0).astype(BF16)
        st_ref[ch["idx"]] = ch["d_chunk"] * st_ref[ch["idx"]] + _dot_tn(kd, vb)

    for i in range(n_ctx):
        for ch in chains:
            ci = i if ch["direction"] == 0 else n_ctx - 1 - i
            kv_update(ch, n_lat + ci * c)

    def finish_chunk(ci):
        start = pl.multiple_of(ci * c, c)
        for hh in range(2):
            v0 = hh * RET_V_DIM
            r = _rms(f_ref[pl.ds(start, c), v0:v0 + RET_V_DIM] + b_ref[pl.ds(start, c), v0:v0 + RET_V_DIM])
            gate = _silu(g_ref[0, pl.ds(start, c), v0:v0 + RET_V_DIM])
            o_ref[0, pl.ds(start, c), v0:v0 + RET_V_DIM] = (r * gate).astype(BF16)

    def lat_step(i, carry, *, finishing):
        starts = [pl.multiple_of((i if ch["direction"] == 0 else n_chunks - 1 - i) * c, c) for ch in chains]
        qms, avs, crs = [], [], []
        for ch, start in zip(chains, starts):
            qb = q_ref[0, pl.ds(start, c), :]
            qm = jnp.where(ch["hmask"], qb, jnp.zeros_like(qb))
            qms.append(qm)
            avs.append(_dot_nt(qm, k_ref[0, pl.ds(start, c), :]))
        for ch, qm in zip(chains, qms):
            qd = (qm.astype(F32) * ch["d_q"]).astype(BF16)
            crs.append(_dot(qd, st_ref[ch["idx"]].astype(BF16)))
        for ch, start, a, cross in zip(chains, starts, avs, crs):
            vb = v_ref[0, pl.ds(start, c), ch["v0"]:ch["v0"] + RET_V_DIM]
            res = _dot((a * ch["d_in"]).astype(BF16), vb) + cross
            dst = f_ref if ch["direction"] == 0 else b_ref
            dst[pl.ds(start, c), ch["v0"]:ch["v0"] + RET_V_DIM] = res
        for ch, start in zip(chains, starts):
            kv_update(ch, start)
        if finishing:
            finish_chunk(i)
            finish_chunk(n_chunks - 1 - i)
        return carry

    assert n_chunks % 2 == 0
    unroll = RET_UNROLL if (n_chunks // 2) % RET_UNROLL == 0 else 1
    lax.fori_loop(0, n_chunks // 2, functools.partial(lat_step, finishing=False), 0, unroll=unroll)
    lax.fori_loop(n_chunks // 2, n_chunks, functools.partial(lat_step, finishing=True), 0, unroll=unroll)


def _retention(rq, rk, rv, rg, decay_logit, n_lat):
    b, t, _ = rq.shape
    npairs = RET_HEADS // 2
    kern = functools.partial(_retention_kernel, n_lat=n_lat)
    return pl.pallas_call(
        kern,
        grid=(b, npairs),
        in_specs=[pl.BlockSpec(decay_logit.shape, lambda i, h: (0, 0)),
                  pl.BlockSpec((1, n_lat, LANES), lambda i, h: (i, 0, h)),
                  pl.BlockSpec((1, t, LANES), lambda i, h: (i, 0, h)),
                  pl.BlockSpec((1, t, 2 * RET_V_DIM), lambda i, h: (i, 0, h)),
                  pl.BlockSpec((1, n_lat, 2 * RET_V_DIM), lambda i, h: (i, 0, h))],
        out_specs=pl.BlockSpec((1, n_lat, 2 * RET_V_DIM), lambda i, h: (i, 0, h)),
        out_shape=jax.ShapeDtypeStruct((b, n_lat, RET_HEADS * RET_V_DIM), BF16),
        scratch_shapes=[pltpu.VMEM((n_lat, 2 * RET_V_DIM), F32), pltpu.VMEM((n_lat, 2 * RET_V_DIM), F32),
                        pltpu.VMEM((4, LANES, RET_V_DIM), F32)],
        compiler_params=_cparams("parallel", "arbitrary"),
    )(decay_logit, rq, rk, rv, rg)


def _rope_tables(n_lat, n_ctx, rot_dim, lane0):
    t = np.arange(n_lat)
    rows = (t // GRID_W).astype(np.float32)
    cols = (t % GRID_W).astype(np.float32)
    m = rot_dim // 4
    freqs = jnp.asarray(ROPE_BASE, F32) ** (-jnp.arange(m, dtype=F32) / m)
    ang_r = jnp.asarray(rows)[:, None] * freqs
    ang_c = jnp.asarray(cols)[:, None] * freqs
    cos = jnp.concatenate([jnp.cos(ang_r)] * 2 + [jnp.cos(ang_c)] * 2, axis=-1)
    sin = jnp.concatenate([-jnp.sin(ang_r), jnp.sin(ang_r), -jnp.sin(ang_c), jnp.sin(ang_c)], axis=-1)
    reps = (LANES - lane0) // rot_dim if lane0 == 0 else 1
    cos = jnp.concatenate([jnp.ones((n_lat, lane0), F32)] + [cos] * reps
                          + [jnp.ones((n_lat, LANES - lane0 - reps * rot_dim), F32)], axis=-1)
    sin = jnp.concatenate([jnp.zeros((n_lat, lane0), F32)] + [sin] * reps
                          + [jnp.zeros((n_lat, LANES - lane0 - reps * rot_dim), F32)], axis=-1)
    cos = jnp.concatenate([cos, jnp.ones((n_ctx, LANES), F32)], axis=0)
    sin = jnp.concatenate([sin, jnp.zeros((n_ctx, LANES), F32)], axis=0)
    return cos, sin


def _rot_partner(w, rot_dim):
    k, n = w.shape
    q = rot_dim // 4
    return w.reshape(k, n // rot_dim, 2, 2, q)[:, :, :, ::-1, :].reshape(k, n)


def _pad_cols(w, lane0, width=LANES):
    k, n = w.shape
    return jnp.concatenate([jnp.zeros((k, lane0), w.dtype), w, jnp.zeros((k, width - lane0 - n), w.dtype)], axis=-1)


def _even_weights(w_in):
    n_rope = 2 * DIFF_HEADS * 2 * DIFF_HEAD_DIM
    dw = DIFF_HEADS * 2 * DIFF_HEAD_DIM
    nw = NA_HEADS * NA_HEAD_DIM
    scale = jnp.concatenate([jnp.full((dw,), DIFF_HEAD_DIM ** -0.5, F32), jnp.ones((2 * dw,), F32),
                             jnp.full((nw,), NA_HEAD_DIM ** -0.5, F32), jnp.ones((2 * nw,), F32)])
    w = w_in * scale
    return jnp.concatenate([w, _rot_partner(w[:, :n_rope], DIFF_HEAD_DIM)], axis=-1).astype(BF16), n_rope


def _odd_weights(w_in, w_uq, w_ukv):
    c0 = MLA_Q_RANK
    c1 = c0 + MLA_KV_RANK
    c2 = c1 + MLA_ROPE_DIM
    nq = RET_HEADS * RET_QK_DIM
    kr = w_in[:, c1:c2]
    w = jnp.concatenate([w_in[:, :c1],
                         _pad_cols(kr, MLA_NOPE_DIM), _pad_cols(_rot_partner(kr, MLA_ROPE_DIM), MLA_NOPE_DIM),
                         w_in[:, c2:c2 + nq], w_in[:, c2 + nq:c2 + 2 * nq] * (RET_QK_DIM ** -0.5),
                         w_in[:, c2 + 2 * nq:]], axis=-1).astype(BF16)
    r = w_uq.shape[0]
    uq = w_uq.reshape(r, MLA_HEADS, MLA_NOPE_DIM + MLA_ROPE_DIM)
    pad = jnp.zeros((r, MLA_HEADS, LANES - MLA_NOPE_DIM - MLA_ROPE_DIM), F32)
    uq_rot = _rot_partner(uq[:, :, MLA_NOPE_DIM:].reshape(r, -1), MLA_ROPE_DIM).reshape(r, MLA_HEADS, MLA_ROPE_DIM)
    wq = jnp.concatenate([uq, pad], axis=-1).reshape(r, -1)
    wq_rot = jnp.concatenate([jnp.zeros_like(uq[:, :, :MLA_NOPE_DIM]), uq_rot, pad], axis=-1).reshape(r, -1)
    rk = w_ukv.shape[0]
    ukv = w_ukv.reshape(rk, MLA_HEADS, MLA_NOPE_DIM + MLA_V_DIM)
    wk = jnp.concatenate([ukv[:, :, :MLA_NOPE_DIM], jnp.zeros((rk, MLA_HEADS, LANES - MLA_NOPE_DIM), F32)],
                         axis=-1).reshape(rk, -1)
    wv = ukv[:, :, MLA_NOPE_DIM:].reshape(rk, -1)
    return w, jnp.concatenate([wq, wq_rot], axis=-1).astype(BF16), jnp.concatenate([wk, wv], axis=-1).astype(BF16)


def _na_bias(rpb, n_lat):
    pat_of_group, idx_r, idx_c, valid = _na_patterns(n_lat // GRID_W)
    g = NA_GROUP_ROWS
    span = g + NA_WIN_ROWS - 1
    n_pat = idx_r.shape[0]
    col_sel = (idx_c[0, :GRID_W, :GRID_W, None] == np.arange(2 * NA_WIN_COLS - 1)).astype(np.float32)
    row_idx = idx_r.reshape(n_pat, g, GRID_W, span, GRID_W)[:, :, 0, :, 0]
    cols = jnp.einsum('hrc,qkc->hrqk', rpb, col_sel, precision=lax.Precision.HIGHEST)
    bias = jnp.take(cols, row_idx.reshape(-1), axis=1).reshape(rpb.shape[0], n_pat, g, span, GRID_W, GRID_W)
    bias = jnp.swapaxes(bias, 3, 4).reshape(rpb.shape[0], n_pat, g * GRID_W, span * GRID_W)
    return jnp.where(valid[None], bias * LOG2E, NEG_BIG), pat_of_group


def _router_weights(w_router):
    return _pad_cols(w_router, 0)


def kernel(x, c, ctx, c_ctx, ada_w, ada_b, norm_mix, norm_ffn, final_norm, even_w_in, even_w_out,
           diff_lambda, diff_subln, na_rpb, odd_w_in, odd_w_out, mla_q_norm, mla_w_uq, mla_kv_norm,
           mla_w_ukv, ret_decay_logit, moe_router, moe_w1, moe_w3, moe_w2):
    b, n_lat, d = x.shape
    n_ctx = ctx.shape[1]
    depth = ada_w.shape[0]
    n_lat_tiles = n_lat // TOK_TILE

    cond = jnp.concatenate([c, c_ctx[None], jnp.zeros((7, d), F32)], axis=0)
    mods_all = _mods(cond, ada_w, ada_b)[:, :b + 1].reshape(depth, b + 1, 6, d)
    xa = jnp.concatenate([x, ctx], axis=1)

    for l in range(depth):
        mods = mods_all[l]
        need_ctx = l < depth - 1
        i = l // 2
        gain_mix = norm_mix[l][None]
        if l % 2 == 0:
            w, n_rope = _even_weights(even_w_in[i])
            cos, sin = _rope_tables(n_lat, n_ctx, DIFF_HEAD_DIM, 0)
            qkv = _proj_even(xa, mods, gain_mix, w, cos, sin, n_lat, n_rope, even_w_in.shape[-1])
            lam_init = 0.8 - 0.6 * math.exp(-0.3 * l)
            mix_a = _diff_attn(qkv, diff_lambda[i], diff_subln[i][None], lam_init, 0, n_lat, ATTN_Q_TILE,
                               0, n_lat + n_ctx)
            mix_a_ctx = _diff_attn(qkv, diff_lambda[i], diff_subln[i][None], lam_init, n_lat, n_ctx, n_ctx,
                                   n_lat, n_ctx)
            bias, offsets = _na_bias(na_rpb[i], n_lat)
            mix_b = _na_attn(qkv, bias, n_lat, offsets, 3 * DIFF_HEADS * 2 * DIFF_HEAD_DIM)
            w_out = even_w_out[i].astype(BF16)
        else:
            w, wq, wkv = _odd_weights(odd_w_in[i], mla_w_uq[i], mla_w_ukv[i])
            cos, sin = _rope_tables(n_lat, n_ctx, MLA_ROPE_DIM, MLA_NOPE_DIM)
            q, k, v, rq, rk, rv, rg = _proj_odd(xa, mods, gain_mix, w, mla_q_norm[i][None], mla_kv_norm[i][None],
                                                wq, wkv, cos, sin, n_lat)
            mix_a = _mla_attn(q, k, v, n_lat, ATTN_Q_TILE)
            mix_a_ctx = mix_a
            mix_b = _retention(rq, rk, rv, rg, ret_decay_logit[i], n_lat)
            w_out = odd_w_out[i].astype(BF16)

        tiles = (n_lat + n_ctx) // TOK_TILE if need_ctx else n_lat_tiles
        x_mid, h2, aff = _proj_out(xa, mix_a, mix_a_ctx, mix_b, w_out, mods, norm_ffn[l][None],
                                   _router_weights(moe_router[l]), tiles, n_lat_tiles)
        segments = ((0, n_lat), (n_lat, n_ctx)) if need_ctx else ((0, n_lat),)
        xa = _moe(x_mid, h2, aff, segments, mods, final_norm[None], n_lat_tiles, l == depth - 1,
                  moe_w1, moe_w3, moe_w2, l)
    return xa
```
